```python
import math
import jax, jax.numpy as jnp
from jax import lax
import numpy as np

D_MODEL = 1024
BATCH = 32
SEQ = 2048
DEPTH = 2

GRID_W = 64
CTX_LEN = 256
HEAD_DIM = 64
A_HEADS = 6
A_KV_HEADS = 2
C_HEADS = 6
C_KV_HEADS = 2
HY_WIDTH = 256
HY_ORDER = 2
HY_BANDS = 16
HY_EMB = 1 + 2 * HY_BANDS
HY_HIDDEN = 64
HY_SHORT = 3
HY_FAST_DECAY = 0.3
HY_SLOW_DECAY = 1.5
HY_TARGET = 1e-2
BLOCK = 128
WINDOW = 128
ROPE_THETA = 10000.0
NORM_EPS = 1e-6
NEG_INF = -1e30
MIX_WIDTH = (A_HEADS + C_HEADS) * HEAD_DIM + HY_WIDTH
IN_SIZES = (A_HEADS * HEAD_DIM, A_KV_HEADS * HEAD_DIM, A_KV_HEADS * HEAD_DIM, A_HEADS * HEAD_DIM,
            HY_WIDTH, HY_WIDTH, HY_WIDTH, HY_WIDTH,
            C_HEADS * HEAD_DIM, C_KV_HEADS * HEAD_DIM, C_KV_HEADS * HEAD_DIM, C_HEADS * HEAD_DIM)
IN_WIDTH = sum(IN_SIZES)

kernel_name = "hymba_style_gqa_hyena_swa_diffusion_block"


def _in_bounds():
    return [int(v) for v in np.cumsum((0,) + IN_SIZES)]


def _split_in(p):
    return jnp.split(p, _in_bounds()[1:-1], axis=-1)


def _rmsnorm(x, g):
    xf = x.astype(jnp.float32)
    y = xf * lax.rsqrt(jnp.mean(xf * xf, axis=-1, keepdims=True) + NORM_EPS)
    return (y * g.astype(jnp.float32)).astype(x.dtype)


def _axial_rope_tables(L, dtype):
    rows = L // GRID_W
    row = jnp.repeat(jnp.arange(rows), GRID_W, total_repeat_length=rows * GRID_W)
    col = jnp.tile(jnp.arange(GRID_W), rows)
    n_freq = HEAD_DIM // 4
    inv_freq = ROPE_THETA ** (-jnp.arange(n_freq, dtype=jnp.float32) / n_freq)
    ang = jnp.stack([row.astype(jnp.float32)[:, None] * inv_freq,
                     col.astype(jnp.float32)[:, None] * inv_freq], axis=1)
    return jnp.cos(ang).astype(dtype), jnp.sin(ang).astype(dtype)


def _apply_rope(x, cos, sin):
    B, L, H, _ = x.shape
    xr = x.reshape(B, L, H, 2, 2, HEAD_DIM // 4)
    xa, xb = xr[..., 0, :], xr[..., 1, :]
    c, s = cos[None, :, None], sin[None, :, None]
    out = jnp.stack([xa * c - xb * s, xb * c + xa * s], axis=-2)
    return out.reshape(B, L, H, HEAD_DIM)


def _heads(t, n):
    return t.reshape(t.shape[0], t.shape[1], n, HEAD_DIM)


def _group(q, n_kv):
    B, L, H, D = q.shape
    return q.reshape(B, L, n_kv, H // n_kv, D)


def _global_attention(q, k, v):
    B, L, KV, G, D = q.shape
    nb = L // BLOCK
    scale = 1.0 / math.sqrt(D)
    qb = jnp.moveaxis(q.reshape(B, nb, BLOCK, KV, G, D), 1, 0)

    def block(qblk):
        s = jnp.einsum('bqkgd,bskd->bkgqs', qblk, k).astype(jnp.float32) * scale
        p = jax.nn.softmax(s, axis=-1).astype(v.dtype)
        return jnp.einsum('bkgqs,bskd->bqkgd', p, v)

    out = lax.map(block, qb)
    return jnp.moveaxis(out, 0, 1).reshape(B, L, KV * G * D)


def _sink_softmax(s, sink):
    B, K, G, Q, _ = s.shape
    sk = jnp.broadcast_to(sink.astype(jnp.float32).reshape(1, K, G, 1, 1), (B, K, G, Q, 1))
    return jax.nn.softmax(jnp.concatenate([s, sk], axis=-1), axis=-1)[..., :-1]


def _window_attention(q, k, v, k_ctx, v_ctx, sink):
    B, L, KV, G, D = q.shape
    nb = L // BLOCK
    band = BLOCK + 2 * WINDOW
    n_ctx = k_ctx.shape[1]
    scale = 1.0 / math.sqrt(D)
    pad = ((0, 0), (WINDOW, WINDOW), (0, 0), (0, 0))
    kp, vp = jnp.pad(k, pad), jnp.pad(v, pad)
    qb = jnp.moveaxis(q.reshape(B, nb, BLOCK, KV, G, D), 1, 0)

    def block(args):
        qblk, i = args
        start = i * BLOCK
        kb = lax.dynamic_slice_in_dim(kp, start, band, axis=1)
        vb = lax.dynamic_slice_in_dim(vp, start, band, axis=1)
        qpos = start + jnp.arange(BLOCK)
        kpos = start - WINDOW + jnp.arange(band)
        valid = ((jnp.abs(qpos[:, None] - kpos[None, :]) <= WINDOW)
                 & (kpos >= 0)[None, :] & (kpos < L)[None, :])
        s_c = jnp.einsum('bqkgd,bskd->bkgqs', qblk, k_ctx).astype(jnp.float32) * scale
        s_b = jnp.einsum('bqkgd,bskd->bkgqs', qblk, kb).astype(jnp.float32) * scale
        s_b = jnp.where(valid, s_b, NEG_INF)
        p = _sink_softmax(jnp.concatenate([s_c, s_b], axis=-1), sink).astype(v.dtype)
        return (jnp.einsum('bkgqs,bskd->bqkgd', p[..., :n_ctx], v_ctx)
                + jnp.einsum('bkgqs,bskd->bqkgd', p[..., n_ctx:], vb))

    out = lax.map(block, (qb, jnp.arange(nb)))
    return jnp.moveaxis(out, 0, 1).reshape(B, L, KV * G * D)


def _ctx_sink_attention(q, k, v, sink):
    B, S, KV, G, D = q.shape
    s = jnp.einsum('bqkgd,bskd->bkgqs', q, k).astype(jnp.float32) / math.sqrt(D)
    p = _sink_softmax(s, sink).astype(v.dtype)
    return jnp.einsum('bkgqs,bskd->bqkgd', p, v).reshape(B, S, KV * G * D)


def _hyena_filter_spectrum(L, w1, b1, w2, b2, w3, freq):
    f32 = jnp.float32
    t = jnp.linspace(0.0, 1.0, L, dtype=f32)[:, None]
    bands = jnp.linspace(1e-4, HY_BANDS - 1, HY_BANDS, dtype=f32)
    w = 2.0 * math.pi * jnp.arange(L, dtype=f32)[:, None] / L
    z = jnp.concatenate([t, jnp.cos(bands * w), jnp.sin(bands * w)], axis=-1)
    fr = freq.astype(f32)
    h = jnp.sin(fr[0] * (z @ w1.astype(f32) + b1.astype(f32)))
    h = jnp.sin(fr[1] * (h @ w2.astype(f32) + b2.astype(f32)))
    h = (h @ w3.astype(f32)).reshape(L, HY_ORDER, 2, HY_WIDTH)
    min_decay = math.log(HY_TARGET) / HY_SLOW_DECAY
    max_decay = math.log(HY_TARGET) / HY_FAST_DECAY
    deltas = jnp.linspace(min_decay, max_decay, HY_WIDTH, dtype=f32)
    h = h * jnp.exp(-t * jnp.abs(deltas))[:, None, None, :]
    fwd, bwd = h[:, :, 0], h[:, :, 1]
    filt = jnp.concatenate([fwd, jnp.zeros_like(fwd[:1]), bwd[:0:-1]], axis=0)
    filt = filt / jnp.sum(jnp.abs(filt), axis=0, keepdims=True)
    return jnp.fft.rfft(filt, axis=0)


def _fft_conv(u, spec_o, d_o):
    L = u.shape[1]
    uf = u.astype(jnp.float32)
    y = jnp.fft.irfft(jnp.fft.rfft(uf, n=2 * L, axis=1) * spec_o[None], n=2 * L, axis=1)[:, :L]
    return (y + uf * d_o.astype(jnp.float32)).astype(u.dtype)


def _short_conv(u, w, b):
    L = u.shape[1]
    r = HY_SHORT // 2
    up = jnp.pad(u, ((0, 0), (r, r), (0, 0)))
    out = b
    for j in range(HY_SHORT):
        out = out + up[:, j:j + L] * w[j]
    return out


def _hyena_mixer(v, x1, x2, conv_w, conv_b, spec, d_bias):
    u = _short_conv(jnp.concatenate([v, x1, x2], axis=-1), conv_w, conv_b)
    parts = jnp.split(u, HY_ORDER + 1, axis=-1)
    z = parts[0]
    for o in range(HY_ORDER):
        z = parts[o + 1] * _fft_conv(z, spec[:, o], d_bias[o])
    return z


def _layer(x, ctx, c, c_ctx, norm_g, w_mod, b_mod, w_in, w_out, qn_a, kn_a, qn_c, kn_c, sink_c,
           hy_conv_w, hy_conv_b, hy_w1, hy_b1, hy_w2, hy_b2, hy_w3, hy_freq, hy_bias, last):
    shift, scale, gate = jnp.split(jax.nn.silu(c) @ w_mod + b_mod, 3, axis=-1)
    shift_c, scale_c, gate_c = jnp.split(jax.nn.silu(c_ctx) @ w_mod + b_mod, 3, axis=-1)
    h = _rmsnorm(x, norm_g) * (1 + scale[:, None]) + shift[:, None]
    hc = _rmsnorm(ctx, norm_g) * (1 + scale_c) + shift_c
    aq, ak, av, ag, bv, bx1, bx2, bg, cq, ck, cv, cg = _split_in(h @ w_in)
    off = _in_bounds()
    if last:
        ak_c, av_c = jnp.split(hc @ w_in[:, off[1]:off[3]], 2, axis=-1)
        ck_c, cv_c = jnp.split(hc @ w_in[:, off[9]:off[11]], 2, axis=-1)
    else:
        (aq_c, ak_c, av_c, ag_c, bv_c, bx1_c, bx2_c, bg_c,
         cq_c, ck_c, cv_c, cg_c) = _split_in(hc @ w_in)
    L = x.shape[1]
    cos, sin = _axial_rope_tables(L, x.dtype)
    hy_p = (hy_w1, hy_b1, hy_w2, hy_b2, hy_w3, hy_freq)

    qa = _apply_rope(_rmsnorm(_heads(aq, A_HEADS), qn_a), cos, sin)
    ka = _apply_rope(_rmsnorm(_heads(ak, A_KV_HEADS), kn_a), cos, sin)
    ka_c = _rmsnorm(_heads(ak_c, A_KV_HEADS), kn_a)
    va, va_c = _heads(av, A_KV_HEADS), _heads(av_c, A_KV_HEADS)
    a_out = _global_attention(_group(qa, A_KV_HEADS), jnp.concatenate([ka_c, ka], axis=1),
                              jnp.concatenate([va_c, va], axis=1)) * jax.nn.silu(ag)
    spec = _hyena_filter_spectrum(L, *hy_p)
    b_out = _hyena_mixer(bv, bx1, bx2, hy_conv_w, hy_conv_b, spec, hy_bias) * jax.nn.silu(bg)
    qc = _apply_rope(_rmsnorm(_heads(cq, C_HEADS), qn_c), cos, sin)
    kc = _apply_rope(_rmsnorm(_heads(ck, C_KV_HEADS), kn_c), cos, sin)
    kc_c = _rmsnorm(_heads(ck_c, C_KV_HEADS), kn_c)
    vc, vc_c = _heads(cv, C_KV_HEADS), _heads(cv_c, C_KV_HEADS)
    c_out = _window_attention(_group(qc, C_KV_HEADS), kc, vc, kc_c, vc_c, sink_c) * jax.nn.silu(cg)

    x = x + gate[:, None] * (jnp.concatenate([a_out, b_out, c_out], axis=-1) @ w_out)
    if last:
        return x, None

    qa_c = _group(_rmsnorm(_heads(aq_c, A_HEADS), qn_a), A_KV_HEADS)
    a_c = _global_attention(qa_c, ka_c, va_c) * jax.nn.silu(ag_c)
    spec_c = _hyena_filter_spectrum(ctx.shape[1], *hy_p)
    b_c = _hyena_mixer(bv_c, bx1_c, bx2_c, hy_conv_w, hy_conv_b, spec_c, hy_bias) * jax.nn.silu(bg_c)
    qc_c = _group(_rmsnorm(_heads(cq_c, C_HEADS), qn_c), C_KV_HEADS)
    c_c = _ctx_sink_attention(qc_c, kc_c, vc_c, sink_c) * jax.nn.silu(cg_c)
    ctx = ctx + gate_c * (jnp.concatenate([a_c, b_c, c_c], axis=-1) @ w_out)
    return x, ctx


def setup_inputs(seed: int = 0) -> dict:
    key = jax.random.key(seed)
    ks = jax.random.split(key, 24)
    f32 = jnp.float32

    def nrm(k, shape, s):
        return jax.random.normal(k, shape, f32) * s

    D = D_MODEL
    return {
        "x": nrm(ks[0], (BATCH, SEQ, D), 1.0),
        "c": nrm(ks[1], (BATCH, D), 1.0),
        "ctx": nrm(ks[2], (BATCH, CTX_LEN, D), 1.0),
        "c_ctx": nrm(ks[3], (D,), 1.0),
        "norm_g": 1.0 + nrm(ks[4], (DEPTH, D), 0.1),
        "w_mod": nrm(ks[5], (DEPTH, D, 3 * D), D ** -0.5),
        "b_mod": nrm(ks[6], (DEPTH, 3 * D), 0.02),
        "w_in": nrm(ks[7], (DEPTH, D, IN_WIDTH), D ** -0.5),
        "w_out": nrm(ks[8], (DEPTH, MIX_WIDTH, D), MIX_WIDTH ** -0.5),
        "qn_a": 1.0 + nrm(ks[9], (DEPTH, HEAD_DIM), 0.1),
        "kn_a": 1.0 + nrm(ks[10], (DEPTH, HEAD_DIM), 0.1),
        "qn_c": 1.0 + nrm(ks[11], (DEPTH, HEAD_DIM), 0.1),
        "kn_c": 1.0 + nrm(ks[12], (DEPTH, HEAD_DIM), 0.1),
        "sink_c": nrm(ks[13], (DEPTH, C_HEADS), 0.5),
        "hy_conv_w": nrm(ks[14], (DEPTH, HY_SHORT, 3 * HY_WIDTH), HY_SHORT ** -0.5),
        "hy_conv_b": nrm(ks[15], (DEPTH, 3 * HY_WIDTH), 0.02),
        "hy_w1": nrm(ks[16], (DEPTH, HY_EMB, HY_HIDDEN), HY_EMB ** -0.5),
        "hy_b1": nrm(ks[17], (DEPTH, HY_HIDDEN), 0.1),
        "hy_w2": nrm(ks[18], (DEPTH, HY_HIDDEN, HY_HIDDEN), HY_HIDDEN ** -0.5),
        "hy_b2": nrm(ks[19], (DEPTH, HY_HIDDEN), 0.1),
        "hy_w3": nrm(ks[20], (DEPTH, HY_HIDDEN, HY_ORDER * 2 * HY_WIDTH), HY_HIDDEN ** -0.5),
        "hy_freq": 1.0 + nrm(ks[21], (DEPTH, 2, HY_HIDDEN), 0.1),
        "hy_bias": nrm(ks[22], (DEPTH, HY_ORDER, HY_WIDTH), 0.5),
    }


def reference(x, c, ctx, c_ctx, norm_g, w_mod, b_mod, w_in, w_out, qn_a, kn_a, qn_c, kn_c, sink_c,
              hy_conv_w, hy_conv_b, hy_w1, hy_b1, hy_w2, hy_b2, hy_w3, hy_freq, hy_bias):
    for l in range(DEPTH):
        x, ctx = _layer(x, ctx, c, c_ctx, norm_g[l], w_mod[l], b_mod[l], w_in[l], w_out[l],
                        qn_a[l], kn_a[l], qn_c[l], kn_c[l], sink_c[l],
                        hy_conv_w[l], hy_conv_b[l], hy_w1[l], hy_b1[l], hy_w2[l], hy_b2[l],
                        hy_w3[l], hy_freq[l], hy_bias[l], last=(l == DEPTH - 1))
    return x
```

```python
import functools
import math

import jax
import jax.numpy as jnp
from jax import lax
from jax.experimental import pallas as pl
from jax.experimental.pallas import tpu as pltpu

F32 = jnp.float32
BF16 = jnp.bfloat16

D_MODEL = 1024
HEAD_DIM = 64
N_HEADS = 6
N_KV = 2
Q_W = N_HEADS * HEAD_DIM
KV_W = N_KV * HEAD_DIM
HY_W = 256
GRID_W = 64
WINDOW = 128
ROPE_THETA = 10000.0
NORM_EPS = 1e-6
NEG_INF = -1e30
HY_BANDS = 16
HY_FAST_DECAY = 0.3
HY_SLOW_DECAY = 1.5
HY_TARGET = 1e-2
LANES = 128
TOEP = 256
FEAT_PAD = 128
VMEM_LIMIT = 56 * 1024 * 1024

_NT = (((1,), (1,)), ((), ()))


def _cparams(sem):
    return pltpu.CompilerParams(dimension_semantics=sem, vmem_limit_bytes=VMEM_LIMIT)


def _silu(x):
    return x * (1.0 / (1.0 + jnp.exp(-x)))


def _mod_kernel(c_ref, w_ref, b_ref, o_ref):
    s = _silu(c_ref[...])
    o_ref[0] = jnp.dot(s, w_ref[0], precision=lax.Precision.HIGHEST,
                       preferred_element_type=F32) + b_ref[0]


def _modulation(c_all, w_mod, b_mod):
    depth, d, n = w_mod.shape
    rows = c_all.shape[0]
    nb = n // d
    return pl.pallas_call(
        _mod_kernel,
        grid=(depth, nb),
        in_specs=[pl.BlockSpec((rows, d), lambda l, j: (0, 0)),
                  pl.BlockSpec((1, d, d), lambda l, j: (l, 0, j)),
                  pl.BlockSpec((1, 1, d), lambda l, j: (l, 0, j))],
        out_specs=pl.BlockSpec((1, rows, d), lambda l, j: (l, 0, j)),
        out_shape=jax.ShapeDtypeStruct((depth, rows, n), F32),
        compiler_params=_cparams(("arbitrary", "arbitrary")),
        name="modulation",
    )(c_all, w_mod, b_mod.reshape(depth, 1, n))


def _head_norm_rope(qk, gain, cos, sin, bd):
    width = qk.shape[1]
    sq = (qk * qk).astype(BF16)
    ss = jnp.concatenate(
        [jnp.dot(sq[:, s:s + TOEP], bd, preferred_element_type=F32)
         for s in range(0, width, TOEP)], axis=1)
    y = qk * lax.rsqrt(ss * (1.0 / HEAD_DIM) + NORM_EPS) * gain
    lane = lax.broadcasted_iota(jnp.int32, (1, LANES), 1)
    first_half = (lane % 32) < 16
    out = []
    for s in range(0, width, LANES):
        ys = y[:, s:s + LANES]
        partner = jnp.where(first_half, pltpu.roll(ys, LANES - 16, axis=1),
                            pltpu.roll(ys, 16, axis=1))
        out.append(ys * cos + partner * sin)
    return jnp.concatenate(out, axis=1)


def _pad_heads(t):
    lane = lax.broadcasted_iota(jnp.int32, (1, LANES), 1)
    lo = lane < HEAD_DIM
    sw = pltpu.roll(t, HEAD_DIM, axis=1)
    zero = jnp.zeros_like(t)
    parts = [jnp.where(lo, t, zero), jnp.where(lo, zero, sw),
             jnp.where(lo, sw, zero), jnp.where(lo, zero, t)]
    return jnp.concatenate(parts, axis=1).astype(BF16)


def _proj_kernel(x_ref, mod_ref, ng_ref, w_ref, cos_ref, sin_ref, bd_ref, gq_a_ref, gq_c_ref,
                 qa_ref, ka_ref, va_ref, ga_ref, hy_ref, qc_ref, kc_ref, vc_ref, gc_ref):
    x = x_ref[0]
    ms = jnp.mean(x * x, axis=-1, keepdims=True)
    y = x * lax.rsqrt(ms + NORM_EPS) * ng_ref[...]
    shift = mod_ref[0, :, 0:D_MODEL]
    scale = mod_ref[0, :, D_MODEL:2 * D_MODEL]
    h = (y * (1.0 + scale) + shift).astype(BF16)
    cos = cos_ref[...]
    sin = sin_ref[...]
    bd = bd_ref[...]

    def attn_branch(col0, gain_ref, q_ref, k_ref, v_ref, g_ref):
        acc = jnp.dot(h, w_ref[:, col0:col0 + D_MODEL], preferred_element_type=F32)
        qk = _head_norm_rope(acc[:, 0:Q_W + KV_W], gain_ref[...], cos, sin, bd)
        q_ref[0] = (qk[:, 0:Q_W] * (1.0 / math.sqrt(HEAD_DIM))).astype(BF16)
        k_ref[0] = _pad_heads(qk[:, Q_W:Q_W + KV_W])
        v_ref[0] = _pad_heads(acc[:, Q_W + KV_W:Q_W + 2 * KV_W])
        g_ref[0] = _silu(acc[:, Q_W + 2 * KV_W:]).astype(BF16)

    attn_branch(0, gq_a_ref, qa_ref, ka_ref, va_ref, ga_ref)
    hy_ref[0] = jnp.dot(h, w_ref[:, D_MODEL:2 * D_MODEL], preferred_element_type=F32)
    attn_branch(2 * D_MODEL, gq_c_ref, qc_ref, kc_ref, vc_ref, gc_ref)


def _projection(x, mod, norm_g, w_in, cos, sin, bd, gq_a, gq_c, tm):
    b, t, d = x.shape
    per_batch_mod = mod.shape[0] > 1
    tok = lambda w: pl.BlockSpec((1, tm, w), lambda i, j: (i, j, 0))
    full = lambda a: pl.BlockSpec(a.shape, lambda i, j: (0,) * a.ndim)
    widths = (Q_W, 4 * LANES, 4 * LANES, Q_W, D_MODEL, Q_W, 4 * LANES, 4 * LANES, Q_W)
    dtypes = (BF16, BF16, BF16, BF16, F32, BF16, BF16, BF16, BF16)
    return pl.pallas_call(
        _proj_kernel,
        grid=(b, t // tm),
        in_specs=[tok(d),
                  pl.BlockSpec((1, 1, mod.shape[2]),
                               (lambda i, j: (i, 0, 0)) if per_batch_mod else (lambda i, j: (0, 0, 0))),
                  full(norm_g), full(w_in),
                  pl.BlockSpec((tm, LANES), lambda i, j: (j, 0)),
                  pl.BlockSpec((tm, LANES), lambda i, j: (j, 0)),
                  full(bd), full(gq_a), full(gq_c)],
        out_specs=[tok(w) for w in widths],
        out_shape=[jax.ShapeDtypeStruct((b, t, w), dt) for w, dt in zip(widths, dtypes)],
        compiler_params=_cparams(("parallel", "arbitrary")),
        name="in_projection",
    )(x, mod, norm_g, w_in, cos, sin, bd, gq_a, gq_c)


def _attn_kernel(*refs, tq, lat_mode, has_sink, lat_len):
    refs = list(refs)
    sink_ref = refs.pop(0) if has_sink else None
    q_ref, g_ref, kc_ref, vc_ref = refs[:4]
    if lat_mode is None:
        o_ref = refs[4]
    else:
        kl_ref, vl_ref, o_ref = refs[4:7]
    i = pl.program_id(1)

    if lat_mode == "window":
        span = tq + 2 * WINDOW
        start = jnp.clip(i * tq - WINDOW, 0, lat_len - span)
        start = pl.multiple_of(start, WINDOW)
        qpos = i * tq + lax.broadcasted_iota(jnp.int32, (tq, span), 0)
        kpos = start + lax.broadcasted_iota(jnp.int32, (tq, span), 1)
        valid = jnp.abs(qpos - kpos) <= WINDOW
        rows = pl.ds(start, span)
    elif lat_mode == "full":
        rows = slice(None)

    for pair in range(N_HEADS // 2):
        lanes = slice(pair * LANES, (pair + 1) * LANES)
        qp = q_ref[0, :, lanes]
        acc = jnp.zeros((tq, LANES), F32)
        for parity in range(2):
            head = 2 * pair + parity
            kv = head // (N_HEADS // N_KV)
            cols = slice((2 * kv + parity) * LANES, (2 * kv + parity + 1) * LANES)
            s_c = lax.dot_general(qp, kc_ref[0, :, cols], _NT, preferred_element_type=F32)
            m = jnp.max(s_c, axis=-1, keepdims=True)
            if lat_mode is not None:
                s_l = lax.dot_general(qp, kl_ref[0, rows, cols], _NT, preferred_element_type=F32)
                if lat_mode == "window":
                    s_l = jnp.where(valid, s_l, NEG_INF)
                m = jnp.maximum(m, jnp.max(s_l, axis=-1, keepdims=True))
            if has_sink:
                sink = sink_ref[head]
                m = jnp.maximum(m, sink)
            p_c = jnp.exp(s_c - m)
            den = jnp.sum(p_c, axis=-1, keepdims=True)
            o = jnp.dot(p_c.astype(BF16), vc_ref[0, :, cols], preferred_element_type=F32)
            if lat_mode is not None:
                p_l = jnp.exp(s_l - m)
                den = den + jnp.sum(p_l, axis=-1, keepdims=True)
                o = o + jnp.dot(p_l.astype(BF16), vl_ref[0, rows, cols], preferred_element_type=F32)
            if has_sink:
                den = den + jnp.exp(sink - m)
            acc = acc + o * (1.0 / den)
        o_ref[0, :, lanes] = (acc * g_ref[0, :, lanes].astype(F32)).astype(BF16)


def _attention(q, g, k_ctx, v_ctx, k_lat, v_lat, sink, lat_mode, tq, name):
    b, t, _ = q.shape
    n_ctx = k_ctx.shape[1]
    has_sink = sink is not None
    tok = pl.BlockSpec((1, tq, Q_W), lambda i, j: (i, j, 0))
    whole = lambda n: pl.BlockSpec((1, n, 4 * LANES), lambda i, j: (i, 0, 0))
    args, specs = [], []
    if has_sink:
        args.append(sink)
        specs.append(pl.BlockSpec(memory_space=pltpu.SMEM))
    args += [q, g, k_ctx, v_ctx]
    specs += [tok, tok, whole(n_ctx), whole(n_ctx)]
    lat_len = 0
    if lat_mode is not None:
        lat_len = k_lat.shape[1]
        args += [k_lat, v_lat]
        specs += [whole(lat_len), whole(lat_len)]
    return pl.pallas_call(
        functools.partial(_attn_kernel, tq=tq, lat_mode=lat_mode, has_sink=has_sink,
                          lat_len=lat_len),
        grid=(b, t // tq),
        in_specs=specs,
        out_specs=tok,
        out_shape=jax.ShapeDtypeStruct((b, t, Q_W), BF16),
        compiler_params=_cparams(("parallel", "arbitrary")),
        name=name,
    )(*args)


def _filter_kernel(z_ref, w1_ref, b1_ref, w2_ref, b2_ref, w3_ref, fr_ref, dl_ref, o_ref, *, lf):
    hp = lax.Precision.HIGHEST
    z = z_ref[...]
    h = jnp.sin(fr_ref[0:1, :] * (jnp.dot(z, w1_ref[...], precision=hp,
                                          preferred_element_type=F32) + b1_ref[...]))
    h = jnp.sin(fr_ref[1:2, :] * (jnp.dot(h, w2_ref[...], precision=hp,
                                          preferred_element_type=F32) + b2_ref[...]))
    bwd = jnp.dot(h[:lf], w3_ref[:, HY_W:2 * HY_W], precision=hp, preferred_element_type=F32)
    fwd = jnp.dot(h[lf:], w3_ref[:, 0:HY_W], precision=hp, preferred_element_type=F32)
    taps = jnp.concatenate([bwd, fwd], axis=0) * jnp.exp(-z[:, 0:1] * dl_ref[...])
    row = lax.broadcasted_iota(jnp.int32, taps.shape, 0)
    taps = jnp.where(row == 0, 0.0, taps)
    taps = taps / jnp.sum(jnp.abs(taps), axis=0, keepdims=True)
    o_ref[0] = taps.T


def _hyena_filters(lf, w1, b1, w2, b2, w3, freq):
    t = jnp.linspace(0.0, 1.0, lf, dtype=F32)[:, None]
    bands = jnp.linspace(1e-4, HY_BANDS - 1, HY_BANDS, dtype=F32)
    w = 2.0 * math.pi * jnp.arange(lf, dtype=F32)[:, None] / lf
    z = jnp.concatenate([t, jnp.cos(bands * w), jnp.sin(bands * w)], axis=-1)
    z = jnp.concatenate([z[:1], z[:0:-1], z], axis=0)
    z = jnp.pad(z, ((0, 0), (0, FEAT_PAD - z.shape[1])))
    w1p = jnp.pad(w1, ((0, FEAT_PAD - w1.shape[0]), (0, 0)))
    min_decay = math.log(HY_TARGET) / HY_SLOW_DECAY
    max_decay = math.log(HY_TARGET) / HY_FAST_DECAY
    deltas = jnp.abs(jnp.linspace(min_decay, max_decay, HY_W, dtype=F32))[None, :]
    full = lambda a: pl.BlockSpec(a.shape, lambda o: (0,) * a.ndim)
    ins = (z, w1p, b1[None, :], w2, b2[None, :], w3, freq, deltas)
    specs = [full(a) for a in ins]
    specs[5] = pl.BlockSpec((w3.shape[0], 2 * HY_W), lambda o: (0, o))
    return pl.pallas_call(
        functools.partial(_filter_kernel, lf=lf),
        grid=(2,),
        in_specs=specs,
        out_specs=pl.BlockSpec((1, HY_W, 2 * lf), lambda o: (o, 0, 0)),
        out_shape=jax.ShapeDtypeStruct((2, HY_W, 2 * lf), F32),
        compiler_params=_cparams(("arbitrary",)),
        name="hyena_filter",
    )(*ins)


def _hyena_kernel(cw_ref, cb_ref, db_ref, v_ref, x1_ref, x2_ref, g_ref, taps_ref, o_ref,
                  *, cw, n_blk, batch):
    rows = n_blk * batch
    c0 = pl.program_id(0) * cw
    lane = lax.broadcasted_iota(jnp.int32, (1, TOEP), 1)
    zero_blk = jnp.zeros((batch, TOEP), F32)

    def short_conv(p, ch):
        prev = pltpu.roll(p, 1, axis=1)
        nxt = pltpu.roll(p, TOEP - 1, axis=1)
        if n_blk > 1:
            prev_edge = jnp.concatenate([zero_blk, prev[:rows - batch]], axis=0)
            next_edge = jnp.concatenate([nxt[batch:], zero_blk], axis=0)
        else:
            prev_edge = jnp.zeros_like(p)
            next_edge = jnp.zeros_like(p)
        prev = jnp.where(lane == 0, prev_edge, prev)
        nxt = jnp.where(lane == TOEP - 1, next_edge, nxt)
        return cb_ref[ch] + cw_ref[0, ch] * prev + cw_ref[1, ch] * p + cw_ref[2, ch] * nxt

    def long_conv(z, order, ci):
        taps = taps_ref[order, ci]
        skew = pltpu.roll(jnp.broadcast_to(taps, (TOEP, taps.shape[1])), 0, axis=1,
                          stride=1, stride_axis=0).astype(BF16)
        zb = z.astype(BF16)
        acc = [jnp.zeros((batch, TOEP), F32) for _ in range(n_blk)]
        for d in range(-(n_blk - 1), n_blk):
            n_out = n_blk - abs(d)
            src = max(0, -d) * batch
            col = (n_blk + d) * TOEP
            part = jnp.dot(zb[src:src + n_out * batch], skew[:, col:col + TOEP],
                           preferred_element_type=F32)
            for k in range(n_out):
                acc[max(0, d) + k] = acc[max(0, d) + k] + part[k * batch:(k + 1) * batch]
        return jnp.concatenate(acc, axis=0) if n_blk > 1 else acc[0]

    def body(ci, carry):
        ch = c0 + ci
        z = short_conv(v_ref[ci], ch)
        xs = (short_conv(x1_ref[ci], HY_W + ch), short_conv(x2_ref[ci], 2 * HY_W + ch))
        for order in range(2):
            z = xs[order] * (long_conv(z, order, ci) + z * db_ref[order, ch])
        o_ref[ci] = (z * _silu(g_ref[ci])).astype(o_ref.dtype)
        return carry

    lax.fori_loop(0, cw, body, 0)


def _hyena(hy_t, taps, conv_w, conv_b, d_bias, n_blk, batch, cw=8):
    rows = n_blk * batch
    smem = pl.BlockSpec(memory_space=pltpu.SMEM)
    slab = lambda off: pl.BlockSpec((cw, rows, TOEP), lambda c: (off // cw + c, 0, 0))
    taps4 = taps.reshape(2, HY_W, 1, taps.shape[-1])
    return pl.pallas_call(
        functools.partial(_hyena_kernel, cw=cw, n_blk=n_blk, batch=batch),
        grid=(HY_W // cw,),
        in_specs=[smem, smem, smem, slab(0), slab(HY_W), slab(2 * HY_W), slab(3 * HY_W),
                  pl.BlockSpec((2, cw, 1, taps.shape[-1]), lambda c: (0, c, 0, 0))],
        out_specs=pl.BlockSpec((cw, rows, TOEP), lambda c: (c, 0, 0)),
        out_shape=jax.ShapeDtypeStruct((HY_W, rows, TOEP), BF16),
        compiler_params=_cparams(("arbitrary",)),
        name="hyena_mixer",
    )(conv_w, conv_b, d_bias, hy_t, hy_t, hy_t, hy_t, taps4)


def _out_kernel(x_ref, gate_ref, a_ref, b_ref, c_ref, w_ref, o_ref):
    acc = jnp.dot(a_ref[0], w_ref[0:Q_W], preferred_element_type=F32)
    acc = acc + jnp.dot(b_ref[0], w_ref[Q_W:Q_W + HY_W], preferred_element_type=F32)
    acc = acc + jnp.dot(c_ref[0], w_ref[Q_W + HY_W:], preferred_element_type=F32)
    o_ref[0] = x_ref[0] + gate_ref[0] * acc


def _out_projection(x, gate, a, bmix, c, w_out, tm):
    b, t, d = x.shape
    per_batch = gate.shape[0] > 1
    tok = lambda w: pl.BlockSpec((1, tm, w), lambda i, j: (i, j, 0))
    return pl.pallas_call(
        _out_kernel,
        grid=(b, t // tm),
        in_specs=[tok(d),
                  pl.BlockSpec((1, 1, d), (lambda i, j: (i, 0, 0)) if per_batch
                               else (lambda i, j: (0, 0, 0))),
                  tok(Q_W), tok(HY_W), tok(Q_W),
                  pl.BlockSpec(w_out.shape, lambda i, j: (0, 0))],
        out_specs=tok(d),
        out_shape=jax.ShapeDtypeStruct((b, t, d), F32),
        compiler_params=_cparams(("parallel", "arbitrary")),
        name="out_projection",
    )(x, gate, a, bmix, c, w_out)


def _rope_tables(t_len):
    pos = jnp.arange(t_len)
    n_freq = HEAD_DIM // 4
    inv_freq = ROPE_THETA ** (-jnp.arange(n_freq, dtype=F32) / n_freq)
    ang = jnp.stack([(pos // GRID_W).astype(F32)[:, None] * inv_freq,
                     (pos % GRID_W).astype(F32)[:, None] * inv_freq], axis=1)
    cos = jnp.cos(ang)[:, :, None, :]
    sin = jnp.sin(ang)[:, :, None, :]
    cos = jnp.broadcast_to(cos, (t_len, 2, 2, n_freq)).reshape(t_len, HEAD_DIM)
    sin = jnp.concatenate([-sin, sin], axis=2).reshape(t_len, HEAD_DIM)
    return jnp.tile(cos, (1, 2)), jnp.tile(sin, (1, 2))


def _to_channel_major(hy, n_blk):
    b, t, c = hy.shape
    return hy.reshape(b, n_blk, TOEP, c).transpose(3, 1, 0, 2).reshape(c, n_blk * b, TOEP)


def _to_token_major(y, n_blk, batch):
    c = y.shape[0]
    return y.reshape(c, n_blk, batch, TOEP).transpose(2, 1, 3, 0).reshape(batch, n_blk * TOEP, c)


def kernel(x, c, ctx, c_ctx, norm_g, w_mod, b_mod, w_in, w_out, qn_a, kn_a, qn_c, kn_c, sink_c,
           hy_conv_w, hy_conv_b, hy_w1, hy_b1, hy_w2, hy_b2, hy_w3, hy_freq, hy_bias):
    depth = w_in.shape[0]
    batch, seq, _ = x.shape
    n_ctx = ctx.shape[1]
    n_blk = seq // TOEP

    rows = -(-(batch + 1) // 8) * 8
    c_all = jnp.concatenate([c, c_ctx[None]], axis=0)
    c_all = jnp.pad(c_all, ((0, rows - batch - 1), (0, 0)))
    mod = _modulation(c_all, w_mod, b_mod)

    cos_l, sin_l = _rope_tables(seq)
    cos_c = jnp.ones((n_ctx, LANES), F32)
    sin_c = jnp.zeros((n_ctx, LANES), F32)
    blk = jnp.arange(TOEP) // HEAD_DIM
    bd = (blk[:, None] == blk[None, :]).astype(BF16)
    w_in_b = w_in.astype(BF16)
    w_out_b = w_out.astype(BF16)

    for l in range(depth):
        last = l == depth - 1
        mod_x = mod[l, :batch, None, :]
        mod_c = mod[l, batch:batch + 1, None, :]
        ng = norm_g[l][None, :]
        gq_a = jnp.concatenate([jnp.tile(qn_a[l], N_HEADS), jnp.tile(kn_a[l], N_KV)])[None, :]
        gq_c = jnp.concatenate([jnp.tile(qn_c[l], N_HEADS), jnp.tile(kn_c[l], N_KV)])[None, :]

        qa, ka, va, ga, hy, qc, kc, vc, gc = _projection(
            x, mod_x, ng, w_in_b[l], cos_l, sin_l, bd, gq_a, gq_c, tm=512)
        qa_c, ka_c, va_c, ga_c, hy_c, qc_c, kc_c, vc_c, gc_c = _projection(
            ctx, mod_c, ng, w_in_b[l], cos_c, sin_c, bd, gq_a, gq_c, tm=256)

        a_out = _attention(qa, ga, ka_c, va_c, ka, va, None, "full", 256, "attn_global")
        c_out = _attention(qc, gc, kc_c, vc_c, kc, vc, sink_c[l], "window", 256, "attn_window")
        taps = _hyena_filters(seq, hy_w1[l], hy_b1[l], hy_w2[l], hy_b2[l], hy_w3[l], hy_freq[l])
        b_out = _hyena(_to_channel_major(hy, n_blk), taps, hy_conv_w[l], hy_conv_b[l],
                       hy_bias[l], n_blk, batch)
        b_out = _to_token_major(b_out, n_blk, batch)
        x_new = _out_projection(x, mod_x[:, :, 2 * D_MODEL:], a_out, b_out, c_out, w_out_b[l], 512)

        if not last:
            a_c = _attention(qa_c, ga_c, ka_c, va_c, None, None, None, None, 256, "attn_ctx_a")
            c_c = _attention(qc_c, gc_c, kc_c, vc_c, None, None, sink_c[l], None, 256,
                             "attn_ctx_c")
            taps_c = _hyena_filters(n_ctx, hy_w1[l], hy_b1[l], hy_w2[l], hy_b2[l], hy_w3[l],
                                    hy_freq[l])
            b_c = _hyena(_to_channel_major(hy_c, 1), taps_c, hy_conv_w[l], hy_conv_b[l],
                         hy_bias[l], 1, batch)
            b_c = _to_token_major(b_c, 1, batch)
            ctx = _out_projection(ctx, mod_c[:, :, 2 * D_MODEL:], a_c, b_c, c_c, w_out_b[l], 256)
        x = x_new
    return x
```

```python
import functools
import math

import jax
import jax.numpy as jnp
from jax import lax
from jax.experimental import pallas as pl
from jax.experimental.pallas import tpu as pltpu

F32 = jnp.float32
BF16 = jnp.bfloat16

D_MODEL = 1024
HEAD_DIM = 64
N_HEADS = 6
N_KV = 2
Q_W = N_HEADS * HEAD_DIM
KV_W = N_KV * HEAD_DIM
HY_W = 256
GRID_W = 64
WINDOW = 128
ROPE_THETA = 10000.0
NORM_EPS = 1e-6
NEG_INF = -1e30
HY_BANDS = 16
HY_FAST_DECAY = 0.3
HY_SLOW_DECAY = 1.5
HY_TARGET = 1e-2
LANES = 128
TOEP = 256
FEAT_PAD = 128
VMEM_LIMIT = 56 * 1024 * 1024

_NT = (((1,), (1,)), ((), ()))
LOG2E = 1.4426950408889634
Q_SCALE = LOG2E / math.sqrt(HEAD_DIM)


def _cparams(sem):
    return pltpu.CompilerParams(dimension_semantics=sem, vmem_limit_bytes=VMEM_LIMIT)


def _silu(x):
    return x * (1.0 / (1.0 + jnp.exp(-x)))


def _mod_kernel(c_ref, w_ref, b_ref, o_ref):
    s = _silu(c_ref[...])
    o_ref[0] = jnp.dot(s, w_ref[0], precision=lax.Precision.HIGHEST,
                       preferred_element_type=F32) + b_ref[0]


def _modulation(c_all, w_mod, b_mod):
    depth, d, n = w_mod.shape
    rows = c_all.shape[0]
    nb = n // d
    return pl.pallas_call(
        _mod_kernel,
        grid=(depth, nb),
        in_specs=[pl.BlockSpec((rows, d), lambda l, j: (0, 0)),
                  pl.BlockSpec((1, d, d), lambda l, j: (l, 0, j)),
                  pl.BlockSpec((1, 1, d), lambda l, j: (l, 0, j))],
        out_specs=pl.BlockSpec((1, rows, d), lambda l, j: (l, 0, j)),
        out_shape=jax.ShapeDtypeStruct((depth, rows, n), F32),
        compiler_params=_cparams(("arbitrary", "arbitrary")),
        name="modulation",
    )(c_all, w_mod, b_mod.reshape(depth, 1, n))


def _head_norm_rope(qk, gain, cos, sin, bd):
    width = qk.shape[1]
    sq = (qk * qk).astype(BF16)
    ss = jnp.concatenate(
        [jnp.dot(sq[:, s:s + TOEP], bd, preferred_element_type=F32)
         for s in range(0, width, TOEP)], axis=1)
    y = qk * lax.rsqrt(ss * (1.0 / HEAD_DIM) + NORM_EPS) * gain
    lane = lax.broadcasted_iota(jnp.int32, (1, LANES), 1)
    first_half = (lane % 32) < 16
    out = []
    for s in range(0, width, LANES):
        ys = y[:, s:s + LANES]
        partner = jnp.where(first_half, pltpu.roll(ys, LANES - 16, axis=1),
                            pltpu.roll(ys, 16, axis=1))
        out.append(ys * cos + partner * sin)
    return jnp.concatenate(out, axis=1)


def _with_swapped(t):
    return jnp.concatenate([t, pltpu.roll(t, HEAD_DIM, axis=1)], axis=1).astype(BF16)


def _proj_kernel(x_ref, mod_ref, ng_ref, w_ref, cos_ref, sin_ref, bd_ref, gq_a_ref, gq_c_ref,
                 qa_ref, ka_ref, va_ref, ga_ref, hy_ref, qc_ref, kc_ref, vc_ref, gc_ref):
    x = x_ref[0]
    ms = jnp.mean(x * x, axis=-1, keepdims=True)
    y = x * lax.rsqrt(ms + NORM_EPS) * ng_ref[...]
    shift = mod_ref[0, :, 0:D_MODEL]
    scale = mod_ref[0, :, D_MODEL:2 * D_MODEL]
    h = (y * (1.0 + scale) + shift).astype(BF16)
    cos = cos_ref[...]
    sin = sin_ref[...]
    bd = bd_ref[...]

    def attn_branch(col0, gain_ref, q_ref, k_ref, v_ref, g_ref):
        acc = jnp.dot(h, w_ref[:, col0:col0 + D_MODEL], preferred_element_type=F32)
        qk = _head_norm_rope(acc[:, 0:Q_W + KV_W], gain_ref[...], cos, sin, bd)
        q_ref[0] = (qk[:, 0:Q_W] * Q_SCALE).astype(BF16)
        k_ref[0] = _with_swapped(qk[:, Q_W:Q_W + KV_W])
        v_ref[0] = _with_swapped(acc[:, Q_W + KV_W:Q_W + 2 * KV_W])
        g_ref[0] = _silu(acc[:, Q_W + 2 * KV_W:]).astype(BF16)

    attn_branch(0, gq_a_ref, qa_ref, ka_ref, va_ref, ga_ref)
    hy_ref[0] = jnp.dot(h, w_ref[:, D_MODEL:2 * D_MODEL],
                        preferred_element_type=F32).astype(BF16)
    attn_branch(2 * D_MODEL, gq_c_ref, qc_ref, kc_ref, vc_ref, gc_ref)


def _projection(x, mod, norm_g, w_in, cos, sin, bd, gq_a, gq_c, tm):
    b, t, d = x.shape
    per_batch_mod = mod.shape[0] > 1
    tok = lambda w: pl.BlockSpec((1, tm, w), lambda i, j: (i, j, 0))
    full = lambda a: pl.BlockSpec(a.shape, lambda i, j: (0,) * a.ndim)
    widths = (Q_W, 2 * LANES, 2 * LANES, Q_W, D_MODEL, Q_W, 2 * LANES, 2 * LANES, Q_W)
    return pl.pallas_call(
        _proj_kernel,
        grid=(b, t // tm),
        in_specs=[tok(d),
                  pl.BlockSpec((1, 1, mod.shape[2]),
                               (lambda i, j: (i, 0, 0)) if per_batch_mod else (lambda i, j: (0, 0, 0))),
                  full(norm_g), full(w_in),
                  pl.BlockSpec((tm, LANES), lambda i, j: (j, 0)),
                  pl.BlockSpec((tm, LANES), lambda i, j: (j, 0)),
                  full(bd), full(gq_a), full(gq_c)],
        out_specs=[tok(w) for w in widths],
        out_shape=[jax.ShapeDtypeStruct((b, t, w), BF16) for w in widths],
        compiler_params=_cparams(("parallel", "arbitrary")),
        name="in_projection",
    )(x, mod, norm_g, w_in, cos, sin, bd, gq_a, gq_c)


def _attn_kernel(*refs, tq, n_sub, lat_mode, has_sink, n_ctx, lat_len):
    refs = list(refs)
    sink_ref = refs.pop(0) if has_sink else None
    q_ref, g_ref, kc_ref, vc_ref = refs[:4]
    refs = refs[4:]
    if lat_mode is not None:
        kl_ref, vl_ref = refs[:2]
        refs = refs[2:]
    o_ref, kpad_s, vaug_s = refs
    i = pl.program_id(1)

    @pl.when(i == 0)
    def _stage_keys():
        lo = lax.broadcasted_iota(jnp.int32, (1, LANES), 1) < HEAD_DIM

        def fill(row0, n, k_ref, v_ref):
            rows = slice(row0, row0 + n)
            for dst, src, width in ((kpad_s, k_ref, LANES), (vaug_s, v_ref, 2 * LANES)):
                t = src[0, :, 0:LANES]
                sw = src[0, :, LANES:2 * LANES]
                zero = jnp.zeros_like(t)
                for blk, val in enumerate((jnp.where(lo, t, zero), jnp.where(lo, zero, sw),
                                           jnp.where(lo, sw, zero), jnp.where(lo, zero, t))):
                    dst[rows, blk * width:blk * width + LANES] = val
                    if width > LANES:
                        dst[rows, blk * width + LANES:(blk + 1) * width] = jnp.ones_like(t)

        fill(0, n_ctx, kc_ref, vc_ref)
        if lat_mode is not None:
            fill(n_ctx, lat_len, kl_ref, vl_ref)

    for sub, pair in [(s_, p_) for s_ in range(n_sub) for p_ in range(N_HEADS // 2)]:
        q_rows = slice(sub * tq, (sub + 1) * tq)
        if lat_mode == "window" and pair == 0:
            span = tq + 2 * WINDOW
            q0 = (i * n_sub + sub) * tq
            start = jnp.clip(q0 - WINDOW, 0, lat_len - span)
            qpos = q0 + lax.broadcasted_iota(jnp.int32, (tq, span), 0)
            kpos = start + lax.broadcasted_iota(jnp.int32, (tq, span), 1)
            valid = jnp.abs(qpos - kpos) <= WINDOW
            band = pl.ds(pl.multiple_of(n_ctx + start, WINDOW), span)
            ctx_rows = slice(0, n_ctx)
        lanes = slice(pair * LANES, (pair + 1) * LANES)
        qp = q_ref[0, q_rows, lanes]
        acc = jnp.zeros((tq, LANES), F32)
        for parity in range(2):
            head = 2 * pair + parity
            blk = 2 * (head // (N_HEADS // N_KV)) + parity
            kcols = slice(blk * LANES, (blk + 1) * LANES)
            vcols = slice(2 * blk * LANES, 2 * (blk + 1) * LANES)
            if lat_mode == "window":
                s_c = lax.dot_general(qp, kpad_s[ctx_rows, kcols], _NT,
                                      preferred_element_type=F32)
                s_l = lax.dot_general(qp, kpad_s[band, kcols], _NT, preferred_element_type=F32)
                s_l = jnp.where(valid, s_l, NEG_INF)
                m = jnp.maximum(jnp.max(s_c, axis=-1, keepdims=True),
                                jnp.max(s_l, axis=-1, keepdims=True))
            else:
                s = lax.dot_general(qp, kpad_s[:, kcols], _NT, preferred_element_type=F32)
                m = jnp.max(s, axis=-1, keepdims=True)
            if has_sink:
                sink = sink_ref[head] * LOG2E
                m = jnp.maximum(m, sink)
            if lat_mode == "window":
                o = jnp.dot(jnp.exp2(s_c - m).astype(BF16), vaug_s[ctx_rows, vcols],
                            preferred_element_type=F32)
                o = o + jnp.dot(jnp.exp2(s_l - m).astype(BF16), vaug_s[band, vcols],
                                preferred_element_type=F32)
            else:
                o = jnp.dot(jnp.exp2(s - m).astype(BF16), vaug_s[:, vcols],
                            preferred_element_type=F32)
            den = o[:, LANES:]
            if has_sink:
                den = den + jnp.exp2(sink - m)
            acc = acc + o[:, :LANES] / den
        o_ref[0, q_rows, lanes] = (acc * g_ref[0, q_rows, lanes].astype(F32)).astype(BF16)


def _attention(q, g, k_ctx, v_ctx, k_lat, v_lat, sink, lat_mode, tq, n_sub, name):
    b, t, _ = q.shape
    n_ctx = k_ctx.shape[1]
    has_sink = sink is not None
    tok = pl.BlockSpec((1, n_sub * tq, Q_W), lambda i, j: (i, j, 0))
    whole = lambda n: pl.BlockSpec((1, n, 2 * LANES), lambda i, j: (i, 0, 0))
    args, specs = [], []
    if has_sink:
        args.append(sink)
        specs.append(pl.BlockSpec(memory_space=pltpu.SMEM))
    args += [q, g, k_ctx, v_ctx]
    specs += [tok, tok, whole(n_ctx), whole(n_ctx)]
    lat_len = 0
    if lat_mode is not None:
        lat_len = k_lat.shape[1]
        args += [k_lat, v_lat]
        specs += [whole(lat_len), whole(lat_len)]
    n_keys = n_ctx + lat_len
    return pl.pallas_call(
        functools.partial(_attn_kernel, tq=tq, n_sub=n_sub, lat_mode=lat_mode,
                          has_sink=has_sink, n_ctx=n_ctx, lat_len=lat_len),
        grid=(b, t // (n_sub * tq)),
        in_specs=specs,
        out_specs=tok,
        out_shape=jax.ShapeDtypeStruct((b, t, Q_W), BF16),
        scratch_shapes=[pltpu.VMEM((n_keys, 2 * N_KV * LANES), BF16),
                        pltpu.VMEM((n_keys, 4 * N_KV * LANES), BF16)],
        compiler_params=_cparams(("arbitrary", "arbitrary")),
        name=name,
    )(*args)


def _filter_kernel(z_ref, w1_ref, b1_ref, w2_ref, b2_ref, w3_ref, fr_ref, dl_ref, o_ref, *, lf):
    hp = lax.Precision.HIGHEST
    z = z_ref[...]
    h = jnp.sin(fr_ref[0:1, :] * (jnp.dot(z, w1_ref[...], precision=hp,
                                          preferred_element_type=F32) + b1_ref[...]))
    h = jnp.sin(fr_ref[1:2, :] * (jnp.dot(h, w2_ref[...], precision=hp,
                                          preferred_element_type=F32) + b2_ref[...]))
    bwd = jnp.dot(h[:lf], w3_ref[:, HY_W:2 * HY_W], precision=hp, preferred_element_type=F32)
    fwd = jnp.dot(h[lf:], w3_ref[:, 0:HY_W], precision=hp, preferred_element_type=F32)
    taps = jnp.concatenate([bwd, fwd], axis=0) * jnp.exp(-z[:, 0:1] * dl_ref[...])
    row = lax.broadcasted_iota(jnp.int32, taps.shape, 0)
    taps = jnp.where(row == 0, 0.0, taps)
    taps = taps / jnp.sum(jnp.abs(taps), axis=0, keepdims=True)
    o_ref[0] = taps.T


def _hyena_filters(lf, w1, b1, w2, b2, w3, freq):
    t = jnp.linspace(0.0, 1.0, lf, dtype=F32)[:, None]
    bands = jnp.linspace(1e-4, HY_BANDS - 1, HY_BANDS, dtype=F32)
    w = 2.0 * math.pi * jnp.arange(lf, dtype=F32)[:, None] / lf
    z = jnp.concatenate([t, jnp.cos(bands * w), jnp.sin(bands * w)], axis=-1)
    z = jnp.concatenate([z[:1], z[:0:-1], z], axis=0)
    z = jnp.pad(z, ((0, 0), (0, FEAT_PAD - z.shape[1])))
    w1p = jnp.pad(w1, ((0, FEAT_PAD - w1.shape[0]), (0, 0)))
    min_decay = math.log(HY_TARGET) / HY_SLOW_DECAY
    max_decay = math.log(HY_TARGET) / HY_FAST_DECAY
    deltas = jnp.abs(jnp.linspace(min_decay, max_decay, HY_W, dtype=F32))[None, :]
    full = lambda a: pl.BlockSpec(a.shape, lambda o: (0,) * a.ndim)
    ins = (z, w1p, b1[None, :], w2, b2[None, :], w3, freq, deltas)
    specs = [full(a) for a in ins]
    specs[5] = pl.BlockSpec((w3.shape[0], 2 * HY_W), lambda o: (0, o))
    return pl.pallas_call(
        functools.partial(_filter_kernel, lf=lf),
        grid=(2,),
        in_specs=specs,
        out_specs=pl.BlockSpec((1, HY_W, 2 * lf), lambda o: (o, 0, 0)),
        out_shape=jax.ShapeDtypeStruct((2, HY_W, 2 * lf), F32),
        compiler_params=_cparams(("arbitrary",)),
        name="hyena_filter",
    )(*ins)


def _hyena_kernel(cw_ref, cb_ref, db_ref, v_ref, x1_ref, x2_ref, g_ref, taps_ref, o_ref,
                  *, cw, n_blk, batch):
    rows = n_blk * batch
    c0 = pl.program_id(0) * cw
    lane = lax.broadcasted_iota(jnp.int32, (1, TOEP), 1)
    zero_blk = jnp.zeros((batch, TOEP), F32)

    def short_conv(p, ch):
        prev = pltpu.roll(p, 1, axis=1)
        nxt = pltpu.roll(p, TOEP - 1, axis=1)
        if n_blk > 1:
            prev_edge = jnp.concatenate([zero_blk, prev[:rows - batch]], axis=0)
            next_edge = jnp.concatenate([nxt[batch:], zero_blk], axis=0)
        else:
            prev_edge = jnp.zeros_like(p)
            next_edge = jnp.zeros_like(p)
        prev = jnp.where(lane == 0, prev_edge, prev)
        nxt = jnp.where(lane == TOEP - 1, next_edge, nxt)
        return cb_ref[ch] + cw_ref[0, ch] * prev + cw_ref[1, ch] * p + cw_ref[2, ch] * nxt

    def long_conv(z, order, ci):
        taps = taps_ref[order, ci]
        skew = pltpu.roll(jnp.broadcast_to(taps, (TOEP, taps.shape[1])), 0, axis=1,
                          stride=1, stride_axis=0).astype(BF16)
        zb = z.astype(BF16)
        acc = [jnp.zeros((batch, TOEP), F32) for _ in range(n_blk)]
        for d in range(-(n_blk - 1), n_blk):
            n_out = n_blk - abs(d)
            src = max(0, -d) * batch
            col = (n_blk + d) * TOEP
            part = jnp.dot(zb[src:src + n_out * batch], skew[:, col:col + TOEP],
                           preferred_element_type=F32)
            for k in range(n_out):
                acc[max(0, d) + k] = acc[max(0, d) + k] + part[k * batch:(k + 1) * batch]
        return jnp.concatenate(acc, axis=0) if n_blk > 1 else acc[0]

    def body(ci, carry):
        ch = c0 + ci
        z = short_conv(v_ref[ci].astype(F32), ch)
        xs = (short_conv(x1_ref[ci].astype(F32), HY_W + ch),
              short_conv(x2_ref[ci].astype(F32), 2 * HY_W + ch))
        for order in range(2):
            z = xs[order] * (long_conv(z, order, ci) + z * db_ref[order, ch])
        o_ref[ci] = (z * _silu(g_ref[ci].astype(F32))).astype(o_ref.dtype)
        return carry

    lax.fori_loop(0, cw, body, 0)


def _hyena(hy_t, taps, conv_w, conv_b, d_bias, n_blk, batch, cw=8):
    rows = n_blk * batch
    smem = pl.BlockSpec(memory_space=pltpu.SMEM)
    slab = lambda off: pl.BlockSpec((cw, rows, TOEP), lambda c: (off // cw + c, 0, 0))
    taps4 = taps.reshape(2, HY_W, 1, taps.shape[-1])
    return pl.pallas_call(
        functools.partial(_hyena_kernel, cw=cw, n_blk=n_blk, batch=batch),
        grid=(HY_W // cw,),
        in_specs=[smem, smem, smem, slab(0), slab(HY_W), slab(2 * HY_W), slab(3 * HY_W),
                  pl.BlockSpec((2, cw, 1, taps.shape[-1]), lambda c: (0, c, 0, 0))],
        out_specs=pl.BlockSpec((cw, rows, TOEP), lambda c: (c, 0, 0)),
        out_shape=jax.ShapeDtypeStruct((HY_W, rows, TOEP), BF16),
        compiler_params=_cparams(("arbitrary",)),
        name="hyena_mixer",
    )(conv_w, conv_b, d_bias, hy_t, hy_t, hy_t, hy_t, taps4)


def _out_kernel(x_ref, gate_ref, a_ref, b_ref, c_ref, w_ref, o_ref):
    acc = jnp.dot(a_ref[0], w_ref[0:Q_W], preferred_element_type=F32)
    acc = acc + jnp.dot(b_ref[0], w_ref[Q_W:Q_W + HY_W], preferred_element_type=F32)
    acc = acc + jnp.dot(c_ref[0], w_ref[Q_W + HY_W:], preferred_element_type=F32)
    o_ref[0] = x_ref[0] + gate_ref[0] * acc


def _out_projection(x, gate, a, bmix, c, w_out, tm):
    b, t, d = x.shape
    per_batch = gate.shape[0] > 1
    tok = lambda w: pl.BlockSpec((1, tm, w), lambda i, j: (i, j, 0))
    return pl.pallas_call(
        _out_kernel,
        grid=(b, t // tm),
        in_specs=[tok(d),
                  pl.BlockSpec((1, 1, d), (lambda i, j: (i, 0, 0)) if per_batch
                               else (lambda i, j: (0, 0, 0))),
                  tok(Q_W), tok(HY_W), tok(Q_W),
                  pl.BlockSpec(w_out.shape, lambda i, j: (0, 0))],
        out_specs=tok(d),
        out_shape=jax.ShapeDtypeStruct((b, t, d), F32),
        compiler_params=_cparams(("parallel", "arbitrary")),
        name="out_projection",
    )(x, gate, a, bmix, c, w_out)


def _rope_tables(t_len):
    pos = jnp.arange(t_len)
    n_freq = HEAD_DIM // 4
    inv_freq = ROPE_THETA ** (-jnp.arange(n_freq, dtype=F32) / n_freq)
    ang = jnp.stack([(pos // GRID_W).astype(F32)[:, None] * inv_freq,
                     (pos % GRID_W).astype(F32)[:, None] * inv_freq], axis=1)
    cos = jnp.cos(ang)[:, :, None, :]
    sin = jnp.sin(ang)[:, :, None, :]
    cos = jnp.broadcast_to(cos, (t_len, 2, 2, n_freq)).reshape(t_len, HEAD_DIM)
    sin = jnp.concatenate([-sin, sin], axis=2).reshape(t_len, HEAD_DIM)
    return jnp.tile(cos, (1, 2)), jnp.tile(sin, (1, 2))


def _to_channel_major(hy, n_blk):
    b, t, c = hy.shape
    return hy.reshape(b, n_blk, TOEP, c).transpose(3, 1, 0, 2).reshape(c, n_blk * b, TOEP)


def _to_token_major(y, n_blk, batch):
    c = y.shape[0]
    return y.reshape(c, n_blk, batch, TOEP).transpose(2, 1, 3, 0).reshape(batch, n_blk * TOEP, c)


def kernel(x, c, ctx, c_ctx, norm_g, w_mod, b_mod, w_in, w_out, qn_a, kn_a, qn_c, kn_c, sink_c,
           hy_conv_w, hy_conv_b, hy_w1, hy_b1, hy_w2, hy_b2, hy_w3, hy_freq, hy_bias):
    depth = w_in.shape[0]
    batch, seq, _ = x.shape
    n_ctx = ctx.shape[1]
    n_blk = seq // TOEP

    rows = -(-(batch + 1) // 8) * 8
    c_all = jnp.concatenate([c, c_ctx[None]], axis=0)
    c_all = jnp.pad(c_all, ((0, rows - batch - 1), (0, 0)))
    mod = _modulation(c_all, w_mod, b_mod)

    cos_l, sin_l = _rope_tables(seq)
    cos_c = jnp.ones((n_ctx, LANES), F32)
    sin_c = jnp.zeros((n_ctx, LANES), F32)
    blk = jnp.arange(TOEP) // HEAD_DIM
    bd = (blk[:, None] == blk[None, :]).astype(BF16)
    w_in_b = w_in.astype(BF16)
    w_out_b = w_out.astype(BF16)

    for l in range(depth):
        last = l == depth - 1
        mod_x = mod[l, :batch, None, :]
        mod_c = mod[l, batch:batch + 1, None, :]
        ng = norm_g[l][None, :]
        gq_a = jnp.concatenate([jnp.tile(qn_a[l], N_HEADS), jnp.tile(kn_a[l], N_KV)])[None, :]
        gq_c = jnp.concatenate([jnp.tile(qn_c[l], N_HEADS), jnp.tile(kn_c[l], N_KV)])[None, :]

        qa, ka, va, ga, hy, qc, kc, vc, gc = _projection(
            x, mod_x, ng, w_in_b[l], cos_l, sin_l, bd, gq_a, gq_c, tm=512)
        qa_c, ka_c, va_c, ga_c, hy_c, qc_c, kc_c, vc_c, gc_c = _projection(
            ctx, mod_c, ng, w_in_b[l], cos_c, sin_c, bd, gq_a, gq_c, tm=256)

        a_out = _attention(qa, ga, ka_c, va_c, ka, va, None, "full", 256, 2, "attn_global")
        c_out = _attention(qc, gc, kc_c, vc_c, kc, vc, sink_c[l], "window", 256, 2,
                           "attn_window")
        taps = _hyena_filters(seq, hy_w1[l], hy_b1[l], hy_w2[l], hy_b2[l], hy_w3[l], hy_freq[l])
        b_out = _hyena(_to_channel_major(hy, n_blk), taps, hy_conv_w[l], hy_conv_b[l],
                       hy_bias[l], n_blk, batch)
        b_out = _to_token_major(b_out, n_blk, batch)
        x_new = _out_projection(x, mod_x[:, :, 2 * D_MODEL:], a_out, b_out, c_out, w_out_b[l], 512)

        if not last:
            a_c = _attention(qa_c, ga_c, ka_c, va_c, None, None, None, None, 256, 1,
                             "attn_ctx_a")
            c_c = _attention(qc_c, gc_c, kc_c, vc_c, None, None, sink_c[l], None, 256, 1,
                             "attn_ctx_c")
            taps_c = _hyena_filters(n_ctx, hy_w1[l], hy_b1[l], hy_w2[l], hy_b2[l], hy_w3[l],
                                    hy_freq[l])
            b_c = _hyena(_to_channel_major(hy_c, 1), taps_c, hy_conv_w[l], hy_conv_b[l],
                         hy_bias[l], 1, batch)
            b_c = _to_token_major(b_c, 1, batch)
            ctx = _out_projection(ctx, mod_c[:, :, 2 * D_MODEL:], a_c, b_c, c_c, w_out_b[l], 256)
        x = x_new
    return x
```

```python
import functools
import math

import jax
import jax.numpy as jnp
from jax import lax
from jax.experimental import pallas as pl
from jax.experimental.pallas import tpu as pltpu

F32 = jnp.float32
BF16 = jnp.bfloat16

D_MODEL = 1024
HEAD_DIM = 64
N_HEADS = 6
N_KV = 2
Q_W = N_HEADS * HEAD_DIM
KV_W = N_KV * HEAD_DIM
HY_W = 256
GRID_W = 64
WINDOW = 128
ROPE_THETA = 10000.0
NORM_EPS = 1e-6
NEG_INF = -1e30
HY_BANDS = 16
HY_FAST_DECAY = 0.3
HY_SLOW_DECAY = 1.5
HY_TARGET = 1e-2
LANES = 128
TOEP = 256
FEAT_PAD = 128
PROJ_SUB = 256
HY_UNROLL = 4
VMEM_LIMIT = 56 * 1024 * 1024

_NT = (((1,), (1,)), ((), ()))
LOG2E = 1.4426950408889634
Q_SCALE = LOG2E / math.sqrt(HEAD_DIM)


def _cparams(sem):
    return pltpu.CompilerParams(dimension_semantics=sem, vmem_limit_bytes=VMEM_LIMIT)


def _silu(x):
    return x * (1.0 / (1.0 + jnp.exp(-x)))


def _mod_kernel(c_ref, w_ref, b_ref, o_ref):
    s = _silu(c_ref[...])
    o_ref[0] = jnp.dot(s, w_ref[0], precision=lax.Precision.HIGHEST,
                       preferred_element_type=F32) + b_ref[0]


def _modulation(c_all, w_mod, b_mod):
    depth, d, n = w_mod.shape
    rows = c_all.shape[0]
    nb = n // d
    return pl.pallas_call(
        _mod_kernel,
        grid=(depth, nb),
        in_specs=[pl.BlockSpec((rows, d), lambda l, j: (0, 0)),
                  pl.BlockSpec((1, d, d), lambda l, j: (l, 0, j)),
                  pl.BlockSpec((1, 1, d), lambda l, j: (l, 0, j))],
        out_specs=pl.BlockSpec((1, rows, d), lambda l, j: (l, 0, j)),
        out_shape=jax.ShapeDtypeStruct((depth, rows, n), F32),
        compiler_params=_cparams(("arbitrary", "arbitrary")),
        name="modulation",
    )(c_all, w_mod, b_mod.reshape(depth, 1, n))


def _head_norm_rope(qk, gain, cos, sin, bd):
    width = qk.shape[1]
    sq = (qk * qk).astype(BF16)
    ss = jnp.concatenate(
        [jnp.dot(sq[:, s:s + TOEP], bd, preferred_element_type=F32)
         for s in range(0, width, TOEP)], axis=1)
    y = qk * lax.rsqrt(ss * (1.0 / HEAD_DIM) + NORM_EPS) * gain
    lane = lax.broadcasted_iota(jnp.int32, (1, LANES), 1)
    first_half = (lane % 32) < 16
    out = []
    for s in range(0, width, LANES):
        ys = y[:, s:s + LANES]
        partner = jnp.where(first_half, pltpu.roll(ys, LANES - 16, axis=1),
                            pltpu.roll(ys, 16, axis=1))
        out.append(ys * cos + partner * sin)
    return jnp.concatenate(out, axis=1)


def _with_swapped(t):
    return jnp.concatenate([t, pltpu.roll(t, HEAD_DIM, axis=1)], axis=1).astype(BF16)


def _proj_kernel(x_ref, mod_ref, ng_ref, w_ref, cos_ref, sin_ref, bd_ref, gq_a_ref, gq_c_ref,
                 qa_ref, ka_ref, va_ref, ga_ref, hy_ref, qc_ref, kc_ref, vc_ref, gc_ref):
    shift = mod_ref[0, :, 0:D_MODEL]
    scale = mod_ref[0, :, D_MODEL:2 * D_MODEL]
    bd = bd_ref[...]
    tm = x_ref.shape[1]
    sub = min(tm, PROJ_SUB)
    for r0 in range(0, tm, sub):
        rows = slice(r0, r0 + sub)
        x = x_ref[0, rows]
        ms = jnp.mean(x * x, axis=-1, keepdims=True)
        y = x * lax.rsqrt(ms + NORM_EPS) * ng_ref[...]
        h = (y * (1.0 + scale) + shift).astype(BF16)
        cos = cos_ref[rows]
        sin = sin_ref[rows]

        def attn_branch(col0, gain_ref, q_ref, k_ref, v_ref, g_ref):
            acc = jnp.dot(h, w_ref[:, col0:col0 + D_MODEL], preferred_element_type=F32)
            qk = _head_norm_rope(acc[:, 0:Q_W + KV_W], gain_ref[...], cos, sin, bd)
            q_ref[0, rows] = (qk[:, 0:Q_W] * Q_SCALE).astype(BF16)
            k_ref[0, rows] = _with_swapped(qk[:, Q_W:Q_W + KV_W])
            v_ref[0, rows] = _with_swapped(acc[:, Q_W + KV_W:Q_W + 2 * KV_W])
            g_ref[0, rows] = _silu(acc[:, Q_W + 2 * KV_W:]).astype(BF16)

        attn_branch(0, gq_a_ref, qa_ref, ka_ref, va_ref, ga_ref)
        hy_ref[0, rows] = jnp.dot(h, w_ref[:, D_MODEL:2 * D_MODEL],
                                  preferred_element_type=F32).astype(BF16)
        attn_branch(2 * D_MODEL, gq_c_ref, qc_ref, kc_ref, vc_ref, gc_ref)


def _projection(x, mod, norm_g, w_in, cos, sin, bd, gq_a, gq_c, tm):
    b, t, d = x.shape
    per_batch_mod = mod.shape[0] > 1
    tok = lambda w: pl.BlockSpec((1, tm, w), lambda i, j: (i, j, 0))
    full = lambda a: pl.BlockSpec(a.shape, lambda i, j: (0,) * a.ndim)
    widths = (Q_W, 2 * LANES, 2 * LANES, Q_W, D_MODEL, Q_W, 2 * LANES, 2 * LANES, Q_W)
    return pl.pallas_call(
        _proj_kernel,
        grid=(b, t // tm),
        in_specs=[tok(d),
                  pl.BlockSpec((1, 1, mod.shape[2]),
                               (lambda i, j: (i, 0, 0)) if per_batch_mod else (lambda i, j: (0, 0, 0))),
                  full(norm_g), full(w_in),
                  pl.BlockSpec((tm, LANES), lambda i, j: (j, 0)),
                  pl.BlockSpec((tm, LANES), lambda i, j: (j, 0)),
                  full(bd), full(gq_a), full(gq_c)],
        out_specs=[tok(w) for w in widths],
        out_shape=[jax.ShapeDtypeStruct((b, t, w), BF16) for w in widths],
        compiler_params=_cparams(("parallel", "arbitrary")),
        name="in_projection",
    )(x, mod, norm_g, w_in, cos, sin, bd, gq_a, gq_c)


def _attn_kernel(*refs, tq, n_sub, lat_mode, has_sink, n_ctx, lat_len):
    refs = list(refs)
    sink_ref = refs.pop(0) if has_sink else None
    q_ref, g_ref, kc_ref, vc_ref = refs[:4]
    refs = refs[4:]
    if lat_mode is not None:
        kl_ref, vl_ref = refs[:2]
        refs = refs[2:]
    o_ref, kpad_s, vaug_s = refs
    i = pl.program_id(1)

    @pl.when(i == 0)
    def _stage_keys():
        lo = lax.broadcasted_iota(jnp.int32, (1, LANES), 1) < HEAD_DIM

        def fill(row0, n, k_ref, v_ref):
            rows = slice(row0, row0 + n)
            for dst, src, width in ((kpad_s, k_ref, LANES), (vaug_s, v_ref, 2 * LANES)):
                t = src[0, :, 0:LANES]
                sw = src[0, :, LANES:2 * LANES]
                zero = jnp.zeros_like(t)
                for blk, val in enumerate((jnp.where(lo, t, zero), jnp.where(lo, zero, sw),
                                           jnp.where(lo, sw, zero), jnp.where(lo, zero, t))):
                    dst[rows, blk * width:blk * width + LANES] = val
                    if width > LANES:
                        dst[rows, blk * width + LANES:(blk + 1) * width] = jnp.ones_like(t)

        fill(0, n_ctx, kc_ref, vc_ref)
        if lat_mode is not None:
            fill(n_ctx, lat_len, kl_ref, vl_ref)

    for sub, pair in [(s_, p_) for s_ in range(n_sub) for p_ in range(N_HEADS // 2)]:
        q_rows = slice(sub * tq, (sub + 1) * tq)
        if lat_mode == "window" and pair == 0:
            span = tq + 2 * WINDOW
            q0 = (i * n_sub + sub) * tq
            start = jnp.clip(q0 - WINDOW, 0, lat_len - span)
            qpos = q0 + lax.broadcasted_iota(jnp.int32, (tq, span), 0)
            kpos = start + lax.broadcasted_iota(jnp.int32, (tq, span), 1)
            valid = jnp.abs(qpos - kpos) <= WINDOW
            band = pl.ds(pl.multiple_of(n_ctx + start, WINDOW), span)
            ctx_rows = slice(0, n_ctx)
        lanes = slice(pair * LANES, (pair + 1) * LANES)
        qp = q_ref[0, q_rows, lanes]
        acc = jnp.zeros((tq, LANES), F32)
        for parity in range(2):
            head = 2 * pair + parity
            blk = 2 * (head // (N_HEADS // N_KV)) + parity
            kcols = slice(blk * LANES, (blk + 1) * LANES)
            vcols = slice(2 * blk * LANES, 2 * (blk + 1) * LANES)
            if lat_mode == "window":
                s_c = lax.dot_general(qp, kpad_s[ctx_rows, kcols], _NT,
                                      preferred_element_type=F32)
                s_l = lax.dot_general(qp, kpad_s[band, kcols], _NT, preferred_element_type=F32)
                s_l = jnp.where(valid, s_l, NEG_INF)
                m = jnp.maximum(jnp.max(s_c, axis=-1, keepdims=True),
                                jnp.max(s_l, axis=-1, keepdims=True))
            else:
                s = lax.dot_general(qp, kpad_s[:, kcols], _NT, preferred_element_type=F32)
                m = jnp.max(s, axis=-1, keepdims=True)
            if has_sink:
                sink = sink_ref[head] * LOG2E
                m = jnp.maximum(m, sink)
            if lat_mode == "window":
                o = jnp.dot(jnp.exp2(s_c - m).astype(BF16), vaug_s[ctx_rows, vcols],
                            preferred_element_type=F32)
                o = o + jnp.dot(jnp.exp2(s_l - m).astype(BF16), vaug_s[band, vcols],
                                preferred_element_type=F32)
            else:
                o = jnp.dot(jnp.exp2(s - m).astype(BF16), vaug_s[:, vcols],
                            preferred_element_type=F32)
            den = o[:, LANES:]
            if has_sink:
                den = den + jnp.exp2(sink - m)
            acc = acc + o[:, :LANES] / den
        o_ref[0, q_rows, lanes] = (acc * g_ref[0, q_rows, lanes].astype(F32)).astype(BF16)


def _attention(q, g, k_ctx, v_ctx, k_lat, v_lat, sink, lat_mode, tq, n_sub, name):
    b, t, _ = q.shape
    n_ctx = k_ctx.shape[1]
    has_sink = sink is not None
    tok = pl.BlockSpec((1, n_sub * tq, Q_W), lambda i, j: (i, j, 0))
    whole = lambda n: pl.BlockSpec((1, n, 2 * LANES), lambda i, j: (i, 0, 0))
    args, specs = [], []
    if has_sink:
        args.append(sink)
        specs.append(pl.BlockSpec(memory_space=pltpu.SMEM))
    args += [q, g, k_ctx, v_ctx]
    specs += [tok, tok, whole(n_ctx), whole(n_ctx)]
    lat_len = 0
    if lat_mode is not None:
        lat_len = k_lat.shape[1]
        args += [k_lat, v_lat]
        specs += [whole(lat_len), whole(lat_len)]
    n_keys = n_ctx + lat_len
    return pl.pallas_call(
        functools.partial(_attn_kernel, tq=tq, n_sub=n_sub, lat_mode=lat_mode,
                          has_sink=has_sink, n_ctx=n_ctx, lat_len=lat_len),
        grid=(b, t // (n_sub * tq)),
        in_specs=specs,
        out_specs=tok,
        out_shape=jax.ShapeDtypeStruct((b, t, Q_W), BF16),
        scratch_shapes=[pltpu.VMEM((n_keys, 2 * N_KV * LANES), BF16),
                        pltpu.VMEM((n_keys, 4 * N_KV * LANES), BF16)],
        compiler_params=_cparams(("arbitrary", "arbitrary")),
        name=name,
    )(*args)


def _filter_kernel(z_ref, w1_ref, b1_ref, w2_ref, b2_ref, w3_ref, fr_ref, dl_ref, o_ref, *, lf):
    hp = lax.Precision.HIGHEST
    z = z_ref[...]
    h = jnp.sin(fr_ref[0:1, :] * (jnp.dot(z, w1_ref[...], precision=hp,
                                          preferred_element_type=F32) + b1_ref[...]))
    h = jnp.sin(fr_ref[1:2, :] * (jnp.dot(h, w2_ref[...], precision=hp,
                                          preferred_element_type=F32) + b2_ref[...]))
    bwd = jnp.dot(h[:lf], w3_ref[:, HY_W:2 * HY_W], precision=hp, preferred_element_type=F32)
    fwd = jnp.dot(h[lf:], w3_ref[:, 0:HY_W], precision=hp, preferred_element_type=F32)
    taps = jnp.concatenate([bwd, fwd], axis=0) * jnp.exp(-z[:, 0:1] * dl_ref[...])
    row = lax.broadcasted_iota(jnp.int32, taps.shape, 0)
    taps = jnp.where(row == 0, 0.0, taps)
    taps = taps / jnp.sum(jnp.abs(taps), axis=0, keepdims=True)
    t = taps.T.astype(BF16).astype(F32)
    lo = pltpu.bitcast(t, jnp.uint32) >> 16
    hi = pltpu.bitcast(pltpu.roll(t, 1, axis=1), jnp.uint32) & jnp.uint32(0xFFFF0000)
    o_ref[0] = pltpu.bitcast(lo | hi, jnp.int32)


def _hyena_filters(lf, w1, b1, w2, b2, w3, freq):
    t = jnp.linspace(0.0, 1.0, lf, dtype=F32)[:, None]
    bands = jnp.linspace(1e-4, HY_BANDS - 1, HY_BANDS, dtype=F32)
    w = 2.0 * math.pi * jnp.arange(lf, dtype=F32)[:, None] / lf
    z = jnp.concatenate([t, jnp.cos(bands * w), jnp.sin(bands * w)], axis=-1)
    z = jnp.concatenate([z[:1], z[:0:-1], z], axis=0)
    z = jnp.pad(z, ((0, 0), (0, FEAT_PAD - z.shape[1])))
    w1p = jnp.pad(w1, ((0, FEAT_PAD - w1.shape[0]), (0, 0)))
    min_decay = math.log(HY_TARGET) / HY_SLOW_DECAY
    max_decay = math.log(HY_TARGET) / HY_FAST_DECAY
    deltas = jnp.abs(jnp.linspace(min_decay, max_decay, HY_W, dtype=F32))[None, :]
    full = lambda a: pl.BlockSpec(a.shape, lambda o: (0,) * a.ndim)
    ins = (z, w1p, b1[None, :], w2, b2[None, :], w3, freq, deltas)
    specs = [full(a) for a in ins]
    specs[5] = pl.BlockSpec((w3.shape[0], 2 * HY_W), lambda o: (0, o))
    return pl.pallas_call(
        functools.partial(_filter_kernel, lf=lf),
        grid=(2,),
        in_specs=specs,
        out_specs=pl.BlockSpec((1, HY_W, 2 * lf), lambda o: (o, 0, 0)),
        out_shape=jax.ShapeDtypeStruct((2, HY_W, 2 * lf), jnp.int32),
        compiler_params=_cparams(("arbitrary",)),
        name="hyena_filter",
    )(*ins)


def _hyena_kernel(cw_ref, cb_ref, db_ref, v_ref, x1_ref, x2_ref, g_ref, taps_ref, shift_ref,
                  o_ref, *, cw, n_blk, batch):
    rows = n_blk * batch
    c0 = pl.program_id(0) * cw
    lane = lax.broadcasted_iota(jnp.int32, (1, TOEP), 1)
    zero_blk = jnp.zeros((batch, TOEP), F32)

    def short_conv(p_b, ch):
        shifted = jnp.dot(p_b, shift_ref[...], preferred_element_type=F32)
        p = p_b.astype(F32)
        prev = shifted[:, :TOEP]
        nxt = shifted[:, TOEP:]
        if n_blk > 1:
            prev_edge = jnp.concatenate([zero_blk, prev[:rows - batch]], axis=0)
            next_edge = jnp.concatenate([nxt[batch:], zero_blk], axis=0)
        else:
            prev_edge = jnp.zeros_like(p)
            next_edge = jnp.zeros_like(p)
        prev = jnp.where(lane == 0, prev_edge, prev)
        nxt = jnp.where(lane == TOEP - 1, next_edge, nxt)
        return cb_ref[ch] + cw_ref[0, ch] * prev + cw_ref[1, ch] * p + cw_ref[2, ch] * nxt

    def long_conv(z, order, ci):
        words = taps_ref[order, ci]
        skew = pltpu.roll(jnp.broadcast_to(words, (TOEP // 2, words.shape[1])), 0, axis=1,
                          stride=2, stride_axis=0)
        skew = pltpu.bitcast(skew, BF16)
        zb = z.astype(BF16)
        acc = [jnp.zeros((batch, TOEP), F32) for _ in range(n_blk)]
        for d in range(-(n_blk - 1), n_blk):
            n_out = n_blk - abs(d)
            src = max(0, -d) * batch
            col = (n_blk + d) * TOEP
            part = jnp.dot(zb[src:src + n_out * batch], skew[:, col:col + TOEP],
                           preferred_element_type=F32)
            for k in range(n_out):
                acc[max(0, d) + k] = acc[max(0, d) + k] + part[k * batch:(k + 1) * batch]
        return jnp.concatenate(acc, axis=0) if n_blk > 1 else acc[0]

    def body(k, carry):
        cis = [k * HY_UNROLL + u for u in range(HY_UNROLL)]
        zs = [short_conv(v_ref[ci], c0 + ci) for ci in cis]
        for order, x_ref in enumerate((x1_ref, x2_ref)):
            xs = [short_conv(x_ref[ci], (order + 1) * HY_W + c0 + ci) for ci in cis]
            ys = [long_conv(z, order, ci) for z, ci in zip(zs, cis)]
            zs = [x * (y + z * db_ref[order, c0 + ci]) for x, y, z, ci in zip(xs, ys, zs, cis)]
        for z, ci in zip(zs, cis):
            o_ref[ci] = (z * _silu(g_ref[ci].astype(F32))).astype(o_ref.dtype)
        return carry

    lax.fori_loop(0, cw // HY_UNROLL, body, 0)


def _hyena(hy_t, taps, conv_w, conv_b, d_bias, n_blk, batch, cw=8):
    rows = n_blk * batch
    smem = pl.BlockSpec(memory_space=pltpu.SMEM)
    slab = lambda off: pl.BlockSpec((cw, rows, TOEP), lambda c: (off // cw + c, 0, 0))
    taps4 = taps.reshape(2, HY_W, 1, taps.shape[-1])
    pos = jnp.arange(TOEP)
    shift = jnp.concatenate([pos[:, None] == (pos[None, :] - 1) % TOEP,
                             pos[:, None] == (pos[None, :] + 1) % TOEP], axis=1).astype(BF16)
    return pl.pallas_call(
        functools.partial(_hyena_kernel, cw=cw, n_blk=n_blk, batch=batch),
        grid=(HY_W // cw,),
        in_specs=[smem, smem, smem, slab(0), slab(HY_W), slab(2 * HY_W), slab(3 * HY_W),
                  pl.BlockSpec((2, cw, 1, taps.shape[-1]), lambda c: (0, c, 0, 0)),
                  pl.BlockSpec((TOEP, 2 * TOEP), lambda c: (0, 0))],
        out_specs=pl.BlockSpec((cw, rows, TOEP), lambda c: (c, 0, 0)),
        out_shape=jax.ShapeDtypeStruct((HY_W, rows, TOEP), BF16),
        compiler_params=_cparams(("arbitrary",)),
        name="hyena_mixer",
    )(conv_w, conv_b, d_bias, hy_t, hy_t, hy_t, hy_t, taps4, shift)


def _out_kernel(x_ref, gate_ref, a_ref, b_ref, c_ref, w_ref, o_ref):
    acc = jnp.dot(a_ref[0], w_ref[0:Q_W], preferred_element_type=F32)
    acc = acc + jnp.dot(b_ref[0], w_ref[Q_W:Q_W + HY_W], preferred_element_type=F32)
    acc = acc + jnp.dot(c_ref[0], w_ref[Q_W + HY_W:], preferred_element_type=F32)
    o_ref[0] = x_ref[0] + gate_ref[0] * acc


def _out_projection(x, gate, a, bmix, c, w_out, tm):
    b, t, d = x.shape
    per_batch = gate.shape[0] > 1
    tok = lambda w: pl.BlockSpec((1, tm, w), lambda i, j: (i, j, 0))
    return pl.pallas_call(
        _out_kernel,
        grid=(b, t // tm),
        in_specs=[tok(d),
                  pl.BlockSpec((1, 1, d), (lambda i, j: (i, 0, 0)) if per_batch
                               else (lambda i, j: (0, 0, 0))),
                  tok(Q_W), tok(HY_W), tok(Q_W),
                  pl.BlockSpec(w_out.shape, lambda i, j: (0, 0))],
        out_specs=tok(d),
        out_shape=jax.ShapeDtypeStruct((b, t, d), F32),
        compiler_params=_cparams(("parallel", "arbitrary")),
        name="out_projection",
    )(x, gate, a, bmix, c, w_out)


def _rope_tables(t_len):
    pos = jnp.arange(t_len)
    n_freq = HEAD_DIM // 4
    inv_freq = ROPE_THETA ** (-jnp.arange(n_freq, dtype=F32) / n_freq)
    ang = jnp.stack([(pos // GRID_W).astype(F32)[:, None] * inv_freq,
                     (pos % GRID_W).astype(F32)[:, None] * inv_freq], axis=1)
    cos = jnp.cos(ang)[:, :, None, :]
    sin = jnp.sin(ang)[:, :, None, :]
    cos = jnp.broadcast_to(cos, (t_len, 2, 2, n_freq)).reshape(t_len, HEAD_DIM)
    sin = jnp.concatenate([-sin, sin], axis=2).reshape(t_len, HEAD_DIM)
    return jnp.tile(cos, (1, 2)), jnp.tile(sin, (1, 2))


def _to_channel_major(hy, n_blk):
    b, t, c = hy.shape
    return hy.reshape(b, n_blk, TOEP, c).transpose(3, 1, 0, 2).reshape(c, n_blk * b, TOEP)


def _to_token_major(y, n_blk, batch):
    c = y.shape[0]
    return y.reshape(c, n_blk, batch, TOEP).transpose(2, 1, 3, 0).reshape(batch, n_blk * TOEP, c)


def kernel(x, c, ctx, c_ctx, norm_g, w_mod, b_mod, w_in, w_out, qn_a, kn_a, qn_c, kn_c, sink_c,
           hy_conv_w, hy_conv_b, hy_w1, hy_b1, hy_w2, hy_b2, hy_w3, hy_freq, hy_bias):
    depth = w_in.shape[0]
    batch, seq, _ = x.shape
    n_ctx = ctx.shape[1]
    n_blk = seq // TOEP

    rows = -(-(batch + 1) // 8) * 8
    c_all = jnp.concatenate([c, c_ctx[None]], axis=0)
    c_all = jnp.pad(c_all, ((0, rows - batch - 1), (0, 0)))
    mod = _modulation(c_all, w_mod, b_mod)

    cos_l, sin_l = _rope_tables(seq)
    cos_c = jnp.ones((n_ctx, LANES), F32)
    sin_c = jnp.zeros((n_ctx, LANES), F32)
    blk = jnp.arange(TOEP) // HEAD_DIM
    bd = (blk[:, None] == blk[None, :]).astype(BF16)
    w_in_b = w_in.astype(BF16)
    w_out_b = w_out.astype(BF16)

    for l in range(depth):
        last = l == depth - 1
        mod_x = mod[l, :batch, None, :]
        mod_c = mod[l, batch:batch + 1, None, :]
        ng = norm_g[l][None, :]
        gq_a = jnp.concatenate([jnp.tile(qn_a[l], N_HEADS), jnp.tile(kn_a[l], N_KV)])[None, :]
        gq_c = jnp.concatenate([jnp.tile(qn_c[l], N_HEADS), jnp.tile(kn_c[l], N_KV)])[None, :]

        qa, ka, va, ga, hy, qc, kc, vc, gc = _projection(
            x, mod_x, ng, w_in_b[l], cos_l, sin_l, bd, gq_a, gq_c, tm=512)
        qa_c, ka_c, va_c, ga_c, hy_c, qc_c, kc_c, vc_c, gc_c = _projection(
            ctx, mod_c, ng, w_in_b[l], cos_c, sin_c, bd, gq_a, gq_c, tm=256)

        a_out = _attention(qa, ga, ka_c, va_c, ka, va, None, "full", 256, 4, "attn_global")
        c_out = _attention(qc, gc, kc_c, vc_c, kc, vc, sink_c[l], "window", 256, 4,
                           "attn_window")
        taps = _hyena_filters(seq, hy_w1[l], hy_b1[l], hy_w2[l], hy_b2[l], hy_w3[l], hy_freq[l])
        b_out = _hyena(_to_channel_major(hy, n_blk), taps, hy_conv_w[l], hy_conv_b[l],
                       hy_bias[l], n_blk, batch)
        b_out = _to_token_major(b_out, n_blk, batch)
        x_new = _out_projection(x, mod_x[:, :, 2 * D_MODEL:], a_out, b_out, c_out, w_out_b[l], 1024)

        if not last:
            a_c = _attention(qa_c, ga_c, ka_c, va_c, None, None, None, None, 256, 1,
                             "attn_ctx_a")
            c_c = _attention(qc_c, gc_c, kc_c, vc_c, None, None, sink_c[l], None, 256, 1,
                             "attn_ctx_c")
            taps_c = _hyena_filters(n_ctx, hy_w1[l], hy_b1[l], hy_w2[l], hy_b2[l], hy_w3[l],
                                    hy_freq[l])
            b_c = _hyena(_to_channel_major(hy_c, 1), taps_c, hy_conv_w[l], hy_conv_b[l],
                         hy_bias[l], 1, batch)
            b_c = _to_token_major(b_c, 1, batch)
            ctx = _out_projection(ctx, mod_c[:, :, 2 * D_MODEL:], a_c, b_c, c_c, w_out_b[l], 256)
        x = x_new
    return x
```

```python
import functools
import math

import jax
import jax.numpy as jnp
from jax import lax
from jax.experimental import pallas as pl
from jax.experimental.pallas import tpu as pltpu

F32 = jnp.float32
BF16 = jnp.bfloat16

D_MODEL = 1024
HEAD_DIM = 64
N_HEADS = 6
N_KV = 2
Q_W = N_HEADS * HEAD_DIM
KV_W = N_KV * HEAD_DIM
HY_W = 256
GRID_W = 64
WINDOW = 128
ROPE_THETA = 10000.0
NORM_EPS = 1e-6
NEG_INF = -1e30
HY_BANDS = 16
HY_FAST_DECAY = 0.3
HY_SLOW_DECAY = 1.5
HY_TARGET = 1e-2
LANES = 128
TOEP = 256
FEAT_PAD = 128
PROJ_SUB = 256
HY_UNROLL = 4
VMEM_LIMIT = 56 * 1024 * 1024

_NT = (((1,), (1,)), ((), ()))
LOG2E = 1.4426950408889634
Q_SCALE = LOG2E / math.sqrt(HEAD_DIM)


def _cparams(sem):
    return pltpu.CompilerParams(dimension_semantics=sem, vmem_limit_bytes=VMEM_LIMIT)


def _silu(x):
    return x * (1.0 / (1.0 + jnp.exp(-x)))


def _mod_kernel(c_ref, w_ref, b_ref, o_ref):
    s = _silu(c_ref[...])
    o_ref[0] = jnp.dot(s, w_ref[0], precision=lax.Precision.HIGHEST,
                       preferred_element_type=F32) + b_ref[0]


def _modulation(c_all, w_mod, b_mod):
    depth, d, n = w_mod.shape
    rows = c_all.shape[0]
    nb = n // d
    return pl.pallas_call(
        _mod_kernel,
        grid=(depth, nb),
        in_specs=[pl.BlockSpec((rows, d), lambda l, j: (0, 0)),
                  pl.BlockSpec((1, d, d), lambda l, j: (l, 0, j)),
                  pl.BlockSpec((1, 1, d), lambda l, j: (l, 0, j))],
        out_specs=pl.BlockSpec((1, rows, d), lambda l, j: (l, 0, j)),
        out_shape=jax.ShapeDtypeStruct((depth, rows, n), F32),
        compiler_params=_cparams(("arbitrary", "arbitrary")),
        name="modulation",
    )(c_all, w_mod, b_mod.reshape(depth, 1, n))


def _head_norm_rope(qk, gain, cos, sin, bd):
    width = qk.shape[1]
    sq = (qk * qk).astype(BF16)
    ss = jnp.concatenate(
        [jnp.dot(sq[:, s:s + TOEP], bd, preferred_element_type=F32)
         for s in range(0, width, TOEP)], axis=1)
    y = qk * lax.rsqrt(ss * (1.0 / HEAD_DIM) + NORM_EPS) * gain
    lane = lax.broadcasted_iota(jnp.int32, (1, LANES), 1)
    first_half = (lane % 32) < 16
    out = []
    for s in range(0, width, LANES):
        ys = y[:, s:s + LANES]
        partner = jnp.where(first_half, pltpu.roll(ys, LANES - 16, axis=1),
                            pltpu.roll(ys, 16, axis=1))
        out.append(ys * cos + partner * sin)
    return jnp.concatenate(out, axis=1)


def _with_swapped(t):
    return jnp.concatenate([t, pltpu.roll(t, HEAD_DIM, axis=1)], axis=1).astype(BF16)


def _proj_kernel(x_ref, mod_ref, ng_ref, w_ref, cos_ref, sin_ref, bd_ref, gq_a_ref, gq_c_ref,
                 qa_ref, ka_ref, va_ref, ga_ref, hy_ref, qc_ref, kc_ref, vc_ref, gc_ref):
    shift = mod_ref[0, :, 0:D_MODEL]
    scale = mod_ref[0, :, D_MODEL:2 * D_MODEL]
    bd = bd_ref[...]
    tm = x_ref.shape[1]
    sub = min(tm, PROJ_SUB)
    for r0 in range(0, tm, sub):
        rows = slice(r0, r0 + sub)
        x = x_ref[0, rows]
        ms = jnp.mean(x * x, axis=-1, keepdims=True)
        y = x * lax.rsqrt(ms + NORM_EPS) * ng_ref[...]
        h = (y * (1.0 + scale) + shift).astype(BF16)
        cos = cos_ref[rows]
        sin = sin_ref[rows]

        def attn_branch(col0, gain_ref, q_ref, k_ref, v_ref, g_ref):
            acc = jnp.dot(h, w_ref[:, col0:col0 + D_MODEL], preferred_element_type=F32)
            qk = _head_norm_rope(acc[:, 0:Q_W + KV_W], gain_ref[...], cos, sin, bd)
            q_ref[0, rows] = (qk[:, 0:Q_W] * Q_SCALE).astype(BF16)
            k_ref[0, rows] = _with_swapped(qk[:, Q_W:Q_W + KV_W])
            v_ref[0, rows] = _with_swapped(acc[:, Q_W + KV_W:Q_W + 2 * KV_W])
            g_ref[0, rows] = _silu(acc[:, Q_W + 2 * KV_W:]).astype(BF16)

        attn_branch(0, gq_a_ref, qa_ref, ka_ref, va_ref, ga_ref)
        hy_ref[0, rows] = jnp.dot(h, w_ref[:, D_MODEL:2 * D_MODEL],
                                  preferred_element_type=F32).astype(BF16)
        attn_branch(2 * D_MODEL, gq_c_ref, qc_ref, kc_ref, vc_ref, gc_ref)


def _projection(x, mod, norm_g, w_in, cos, sin, bd, gq_a, gq_c, tm):
    b, t, d = x.shape
    per_batch_mod = mod.shape[0] > 1
    tok = lambda w: pl.BlockSpec((1, tm, w), lambda i, j: (i, j, 0))
    full = lambda a: pl.BlockSpec(a.shape, lambda i, j: (0,) * a.ndim)
    widths = (Q_W, 2 * LANES, 2 * LANES, Q_W, D_MODEL, Q_W, 2 * LANES, 2 * LANES, Q_W)
    return pl.pallas_call(
        _proj_kernel,
        grid=(b, t // tm),
        in_specs=[tok(d),
                  pl.BlockSpec((1, 1, mod.shape[2]),
                               (lambda i, j: (i, 0, 0)) if per_batch_mod else (lambda i, j: (0, 0, 0))),
                  full(norm_g), full(w_in),
                  pl.BlockSpec((tm, LANES), lambda i, j: (j, 0)),
                  pl.BlockSpec((tm, LANES), lambda i, j: (j, 0)),
                  full(bd), full(gq_a), full(gq_c)],
        out_specs=[tok(w) for w in widths],
        out_shape=[jax.ShapeDtypeStruct((b, t, w), BF16) for w in widths],
        compiler_params=_cparams(("parallel", "arbitrary")),
        name="in_projection",
    )(x, mod, norm_g, w_in, cos, sin, bd, gq_a, gq_c)


def _proj_kv_kernel(x_ref, mod_ref, ng_ref, w_ref, bd_ref, gk_ref, ka_ref, va_ref, kc_ref, vc_ref):
    x = x_ref[0]
    ms = jnp.mean(x * x, axis=-1, keepdims=True)
    y = x * lax.rsqrt(ms + NORM_EPS) * ng_ref[...]
    h = (y * (1.0 + mod_ref[0, :, D_MODEL:2 * D_MODEL]) + mod_ref[0, :, 0:D_MODEL]).astype(BF16)
    acc = jnp.dot(h, w_ref[...], preferred_element_type=F32)
    k = jnp.concatenate([acc[:, 0:KV_W], acc[:, 2 * KV_W:3 * KV_W]], axis=1)
    ss = jnp.dot((k * k).astype(BF16), bd_ref[...], preferred_element_type=F32)
    k = k * lax.rsqrt(ss * (1.0 / HEAD_DIM) + NORM_EPS) * gk_ref[...]
    ka_ref[0] = _with_swapped(k[:, 0:KV_W])
    kc_ref[0] = _with_swapped(k[:, KV_W:])
    va_ref[0] = _with_swapped(acc[:, KV_W:2 * KV_W])
    vc_ref[0] = _with_swapped(acc[:, 3 * KV_W:])


def _kv_projection(x, mod, norm_g, w_kv, bd, gk, tm):
    b, t, d = x.shape
    tok = lambda w: pl.BlockSpec((1, tm, w), lambda i, j: (i, j, 0))
    full = lambda a: pl.BlockSpec(a.shape, lambda i, j: (0,) * a.ndim)
    return pl.pallas_call(
        _proj_kv_kernel,
        grid=(b, t // tm),
        in_specs=[tok(d), full(mod), full(norm_g), full(w_kv), full(bd), full(gk)],
        out_specs=[tok(2 * LANES)] * 4,
        out_shape=[jax.ShapeDtypeStruct((b, t, 2 * LANES), BF16)] * 4,
        compiler_params=_cparams(("parallel", "arbitrary")),
        name="kv_projection",
    )(x, mod, norm_g, w_kv, bd, gk)


def _attn_kernel(*refs, tq, n_sub, lat_mode, has_sink, n_ctx, lat_len):
    refs = list(refs)
    sink_ref = refs.pop(0) if has_sink else None
    q_ref, g_ref, kc_ref, vc_ref = refs[:4]
    refs = refs[4:]
    if lat_mode is not None:
        kl_ref, vl_ref = refs[:2]
        refs = refs[2:]
    o_ref, kpad_s, vaug_s = refs
    i = pl.program_id(1)

    @pl.when(i == 0)
    def _stage_keys():
        lo = lax.broadcasted_iota(jnp.int32, (1, LANES), 1) < HEAD_DIM

        def fill(row0, n, k_ref, v_ref):
            rows = slice(row0, row0 + n)
            for dst, src, width in ((kpad_s, k_ref, LANES), (vaug_s, v_ref, 2 * LANES)):
                t = src[0, :, 0:LANES]
                sw = src[0, :, LANES:2 * LANES]
                zero = jnp.zeros_like(t)
                for blk, val in enumerate((jnp.where(lo, t, zero), jnp.where(lo, zero, sw),
                                           jnp.where(lo, sw, zero), jnp.where(lo, zero, t))):
                    dst[rows, blk * width:blk * width + LANES] = val

        fill(0, n_ctx, kc_ref, vc_ref)
        if lat_mode is not None:
            fill(n_ctx, lat_len, kl_ref, vl_ref)

    @pl.when((i == 0) & (pl.program_id(0) == 0))
    def _stage_ones():
        for blk in range(2 * N_KV):
            vaug_s[:, (2 * blk + 1) * LANES:(2 * blk + 2) * LANES] = jnp.ones(
                (vaug_s.shape[0], LANES), BF16)

    for sub, pair in [(s_, p_) for s_ in range(n_sub) for p_ in range(N_HEADS // 2)]:
        q_rows = slice(sub * tq, (sub + 1) * tq)
        if lat_mode == "window" and pair == 0:
            span = tq + 2 * WINDOW
            q0 = (i * n_sub + sub) * tq
            start = jnp.clip(q0 - WINDOW, 0, lat_len - span)
            qpos = q0 + lax.broadcasted_iota(jnp.int32, (tq, span), 0)
            kpos = start + lax.broadcasted_iota(jnp.int32, (tq, span), 1)
            valid = jnp.abs(qpos - kpos) <= WINDOW
            band = pl.ds(pl.multiple_of(n_ctx + start, WINDOW), span)
            ctx_rows = slice(0, n_ctx)
        lanes = slice(pair * LANES, (pair + 1) * LANES)
        qp = q_ref[0, q_rows, lanes]
        acc = jnp.zeros((tq, LANES), F32)
        for parity in range(2):
            head = 2 * pair + parity
            blk = 2 * (head // (N_HEADS // N_KV)) + parity
            kcols = slice(blk * LANES, (blk + 1) * LANES)
            vcols = slice(2 * blk * LANES, 2 * (blk + 1) * LANES)
            if lat_mode == "window":
                s_c = lax.dot_general(qp, kpad_s[ctx_rows, kcols], _NT,
                                      preferred_element_type=F32)
                s_l = lax.dot_general(qp, kpad_s[band, kcols], _NT, preferred_element_type=F32)
                s_l = jnp.where(valid, s_l, NEG_INF)
                m = jnp.maximum(jnp.max(s_c, axis=-1, keepdims=True),
                                jnp.max(s_l, axis=-1, keepdims=True))
            else:
                s = lax.dot_general(qp, kpad_s[:, kcols], _NT, preferred_element_type=F32)
                m = jnp.max(s, axis=-1, keepdims=True)
            if has_sink:
                sink = sink_ref[head] * LOG2E
                m = jnp.maximum(m, sink)
            if lat_mode == "window":
                o = jnp.dot(jnp.exp2(s_c - m).astype(BF16), vaug_s[ctx_rows, vcols],
                            preferred_element_type=F32)
                o = o + jnp.dot(jnp.exp2(s_l - m).astype(BF16), vaug_s[band, vcols],
                                preferred_element_type=F32)
            else:
                o = jnp.dot(jnp.exp2(s - m).astype(BF16), vaug_s[:, vcols],
                            preferred_element_type=F32)
            den = o[:, LANES:]
            if has_sink:
                den = den + jnp.exp2(sink - m)
            acc = acc + o[:, :LANES] / den
        o_ref[0, q_rows, lanes] = (acc * g_ref[0, q_rows, lanes].astype(F32)).astype(BF16)


def _attention(q, g, k_ctx, v_ctx, k_lat, v_lat, sink, lat_mode, tq, n_sub, name):
    b, t, _ = q.shape
    n_ctx = k_ctx.shape[1]
    has_sink = sink is not None
    tok = pl.BlockSpec((1, n_sub * tq, Q_W), lambda i, j: (i, j, 0))
    whole = lambda n: pl.BlockSpec((1, n, 2 * LANES), lambda i, j: (i, 0, 0))
    args, specs = [], []
    if has_sink:
        args.append(sink)
        specs.append(pl.BlockSpec(memory_space=pltpu.SMEM))
    args += [q, g, k_ctx, v_ctx]
    specs += [tok, tok, whole(n_ctx), whole(n_ctx)]
    lat_len = 0
    if lat_mode is not None:
        lat_len = k_lat.shape[1]
        args += [k_lat, v_lat]
        specs += [whole(lat_len), whole(lat_len)]
    n_keys = n_ctx + lat_len
    return pl.pallas_call(
        functools.partial(_attn_kernel, tq=tq, n_sub=n_sub, lat_mode=lat_mode,
                          has_sink=has_sink, n_ctx=n_ctx, lat_len=lat_len),
        grid=(b, t // (n_sub * tq)),
        in_specs=specs,
        out_specs=tok,
        out_shape=jax.ShapeDtypeStruct((b, t, Q_W), BF16),
        scratch_shapes=[pltpu.VMEM((n_keys, 2 * N_KV * LANES), BF16),
                        pltpu.VMEM((n_keys, 4 * N_KV * LANES), BF16)],
        compiler_params=_cparams(("arbitrary", "arbitrary")),
        name=name,
    )(*args)


def _filter_kernel(z_ref, w1_ref, b1_ref, w2_ref, b2_ref, w3_ref, fr_ref, dl_ref, o_ref, *, lf):
    hp = lax.Precision.HIGHEST
    z = z_ref[...]
    h = jnp.sin(fr_ref[0:1, :] * (jnp.dot(z, w1_ref[...], precision=hp,
                                          preferred_element_type=F32) + b1_ref[...]))
    h = jnp.sin(fr_ref[1:2, :] * (jnp.dot(h, w2_ref[...], precision=hp,
                                          preferred_element_type=F32) + b2_ref[...]))
    decay = jnp.exp(-z[:, 0:1] * dl_ref[...])
    first_row = lax.broadcasted_iota(jnp.int32, decay.shape, 0) == 0
    for order in range(2):
        c0 = 2 * order * HY_W
        bwd = jnp.dot(h[:lf], w3_ref[:, c0 + HY_W:c0 + 2 * HY_W], precision=hp,
                      preferred_element_type=F32)
        fwd = jnp.dot(h[lf:], w3_ref[:, c0:c0 + HY_W], precision=hp,
                      preferred_element_type=F32)
        taps = jnp.where(first_row, 0.0, jnp.concatenate([bwd, fwd], axis=0) * decay)
        taps = taps / jnp.sum(jnp.abs(taps), axis=0, keepdims=True)
        t = taps.T.astype(BF16).astype(F32)
        lo = pltpu.bitcast(t, jnp.uint32) >> 16
        hi = pltpu.bitcast(pltpu.roll(t, 1, axis=1), jnp.uint32) & jnp.uint32(0xFFFF0000)
        o_ref[order] = pltpu.bitcast(lo | hi, jnp.int32)


def _hyena_filters(lf, w1, b1, w2, b2, w3, freq):
    t = jnp.linspace(0.0, 1.0, lf, dtype=F32)[:, None]
    bands = jnp.linspace(1e-4, HY_BANDS - 1, HY_BANDS, dtype=F32)
    w = 2.0 * math.pi * jnp.arange(lf, dtype=F32)[:, None] / lf
    z = jnp.concatenate([t, jnp.cos(bands * w), jnp.sin(bands * w)], axis=-1)
    z = jnp.concatenate([z[:1], z[:0:-1], z], axis=0)
    z = jnp.pad(z, ((0, 0), (0, FEAT_PAD - z.shape[1])))
    w1p = jnp.pad(w1, ((0, FEAT_PAD - w1.shape[0]), (0, 0)))
    min_decay = math.log(HY_TARGET) / HY_SLOW_DECAY
    max_decay = math.log(HY_TARGET) / HY_FAST_DECAY
    deltas = jnp.abs(jnp.linspace(min_decay, max_decay, HY_W, dtype=F32))[None, :]
    full = lambda a: pl.BlockSpec(a.shape, lambda o: (0,) * a.ndim)
    ins = (z, w1p, b1[None, :], w2, b2[None, :], w3, freq, deltas)
    return pl.pallas_call(
        functools.partial(_filter_kernel, lf=lf),
        grid=(1,),
        in_specs=[full(a) for a in ins],
        out_specs=pl.BlockSpec((2, HY_W, 2 * lf), lambda o: (0, 0, 0)),
        out_shape=jax.ShapeDtypeStruct((2, HY_W, 2 * lf), jnp.int32),
        compiler_params=_cparams(("arbitrary",)),
        name="hyena_filter",
    )(*ins)


def _hyena_kernel(cw_ref, cb_ref, db_ref, v_ref, x1_ref, x2_ref, g_ref, taps_ref, shift_ref,
                  o_ref, *, cw, n_blk, batch):
    rows = n_blk * batch
    c0 = pl.program_id(0) * cw
    lane = lax.broadcasted_iota(jnp.int32, (1, TOEP), 1)
    zero_blk = jnp.zeros((batch, TOEP), F32)

    def short_conv(p_b, ch):
        shifted = jnp.dot(p_b, shift_ref[...], preferred_element_type=F32)
        p = p_b.astype(F32)
        prev = shifted[:, :TOEP]
        nxt = shifted[:, TOEP:]
        if n_blk > 1:
            prev_edge = jnp.concatenate([zero_blk, prev[:rows - batch]], axis=0)
            next_edge = jnp.concatenate([nxt[batch:], zero_blk], axis=0)
        else:
            prev_edge = jnp.zeros_like(p)
            next_edge = jnp.zeros_like(p)
        prev = jnp.where(lane == 0, prev_edge, prev)
        nxt = jnp.where(lane == TOEP - 1, next_edge, nxt)
        return cb_ref[ch] + cw_ref[0, ch] * prev + cw_ref[1, ch] * p + cw_ref[2, ch] * nxt

    def long_conv(z, order, ci):
        words = taps_ref[order, ci]
        skew = pltpu.roll(jnp.broadcast_to(words, (TOEP // 2, words.shape[1])), 0, axis=1,
                          stride=2, stride_axis=0)
        skew = pltpu.bitcast(skew, BF16)
        zb = z.astype(BF16)
        acc = [jnp.zeros((batch, TOEP), F32) for _ in range(n_blk)]
        for d in range(-(n_blk - 1), n_blk):
            n_out = n_blk - abs(d)
            src = max(0, -d) * batch
            col = (n_blk + d) * TOEP
            part = jnp.dot(zb[src:src + n_out * batch], skew[:, col:col + TOEP],
                           preferred_element_type=F32)
            for k in range(n_out):
                acc[max(0, d) + k] = acc[max(0, d) + k] + part[k * batch:(k + 1) * batch]
        return jnp.concatenate(acc, axis=0) if n_blk > 1 else acc[0]

    def body(k, carry):
        cis = [k * HY_UNROLL + u for u in range(HY_UNROLL)]
        zs = [short_conv(v_ref[ci], c0 + ci) for ci in cis]
        for order, x_ref in enumerate((x1_ref, x2_ref)):
            xs = [short_conv(x_ref[ci], (order + 1) * HY_W + c0 + ci) for ci in cis]
            ys = [long_conv(z, order, ci) for z, ci in zip(zs, cis)]
            zs = [x * (y + z * db_ref[order, c0 + ci]) for x, y, z, ci in zip(xs, ys, zs, cis)]
        for z, ci in zip(zs, cis):
            o_ref[ci] = (z * _silu(g_ref[ci].astype(F32))).astype(o_ref.dtype)
        return carry

    lax.fori_loop(0, cw // HY_UNROLL, body, 0)


def _hyena(hy_t, taps, conv_w, conv_b, d_bias, n_blk, batch, cw=8):
    rows = n_blk * batch
    smem = pl.BlockSpec(memory_space=pltpu.SMEM)
    slab = lambda off: pl.BlockSpec((cw, rows, TOEP), lambda c: (off // cw + c, 0, 0))
    taps4 = taps.reshape(2, HY_W, 1, taps.shape[-1])
    pos = jnp.arange(TOEP)
    shift = jnp.concatenate([pos[:, None] == (pos[None, :] - 1) % TOEP,
                             pos[:, None] == (pos[None, :] + 1) % TOEP], axis=1).astype(BF16)
    return pl.pallas_call(
        functools.partial(_hyena_kernel, cw=cw, n_blk=n_blk, batch=batch),
        grid=(HY_W // cw,),
        in_specs=[smem, smem, smem, slab(0), slab(HY_W), slab(2 * HY_W), slab(3 * HY_W),
                  pl.BlockSpec((2, cw, 1, taps.shape[-1]), lambda c: (0, c, 0, 0)),
                  pl.BlockSpec((TOEP, 2 * TOEP), lambda c: (0, 0))],
        out_specs=pl.BlockSpec((cw, rows, TOEP), lambda c: (c, 0, 0)),
        out_shape=jax.ShapeDtypeStruct((HY_W, rows, TOEP), BF16),
        compiler_params=_cparams(("arbitrary",)),
        name="hyena_mixer",
    )(conv_w, conv_b, d_bias, hy_t, hy_t, hy_t, hy_t, taps4, shift)


def _out_kernel(x_ref, gate_ref, a_ref, b_ref, c_ref, w_ref, o_ref):
    acc = jnp.dot(a_ref[0], w_ref[0:Q_W], preferred_element_type=F32)
    acc = acc + jnp.dot(b_ref[0], w_ref[Q_W:Q_W + HY_W], preferred_element_type=F32)
    acc = acc + jnp.dot(c_ref[0], w_ref[Q_W + HY_W:], preferred_element_type=F32)
    o_ref[0] = x_ref[0] + gate_ref[0] * acc


def _out_projection(x, gate, a, bmix, c, w_out, tm):
    b, t, d = x.shape
    per_batch = gate.shape[0] > 1
    tok = lambda w: pl.BlockSpec((1, tm, w), lambda i, j: (i, j, 0))
    return pl.pallas_call(
        _out_kernel,
        grid=(b, t // tm),
        in_specs=[tok(d),
                  pl.BlockSpec((1, 1, d), (lambda i, j: (i, 0, 0)) if per_batch
                               else (lambda i, j: (0, 0, 0))),
                  tok(Q_W), tok(HY_W), tok(Q_W),
                  pl.BlockSpec(w_out.shape, lambda i, j: (0, 0))],
        out_specs=tok(d),
        out_shape=jax.ShapeDtypeStruct((b, t, d), F32),
        compiler_params=_cparams(("parallel", "arbitrary")),
        name="out_projection",
    )(x, gate, a, bmix, c, w_out)


def _rope_tables(t_len):
    pos = jnp.arange(t_len)
    n_freq = HEAD_DIM // 4
    inv_freq = ROPE_THETA ** (-jnp.arange(n_freq, dtype=F32) / n_freq)
    ang = jnp.stack([(pos // GRID_W).astype(F32)[:, None] * inv_freq,
                     (pos % GRID_W).astype(F32)[:, None] * inv_freq], axis=1)
    cos = jnp.cos(ang)[:, :, None, :]
    sin = jnp.sin(ang)[:, :, None, :]
    cos = jnp.broadcast_to(cos, (t_len, 2, 2, n_freq)).reshape(t_len, HEAD_DIM)
    sin = jnp.concatenate([-sin, sin], axis=2).reshape(t_len, HEAD_DIM)
    return jnp.tile(cos, (1, 2)), jnp.tile(sin, (1, 2))


def _to_channel_major(hy, n_blk):
    b, t, c = hy.shape
    return hy.reshape(b, n_blk, TOEP, c).transpose(3, 1, 0, 2).reshape(c, n_blk * b, TOEP)


def _to_token_major(y, n_blk, batch):
    c = y.shape[0]
    return y.reshape(c, n_blk, batch, TOEP).transpose(2, 1, 3, 0).reshape(batch, n_blk * TOEP, c)


def kernel(x, c, ctx, c_ctx, norm_g, w_mod, b_mod, w_in, w_out, qn_a, kn_a, qn_c, kn_c, sink_c,
           hy_conv_w, hy_conv_b, hy_w1, hy_b1, hy_w2, hy_b2, hy_w3, hy_freq, hy_bias):
    depth = w_in.shape[0]
    batch, seq, _ = x.shape
    n_ctx = ctx.shape[1]
    n_blk = seq // TOEP

    rows = -(-(batch + 1) // 8) * 8
    c_all = jnp.concatenate([c, c_ctx[None]], axis=0)
    c_all = jnp.pad(c_all, ((0, rows - batch - 1), (0, 0)))
    mod = _modulation(c_all, w_mod, b_mod)

    cos_l, sin_l = _rope_tables(seq)
    cos_c = jnp.ones((n_ctx, LANES), F32)
    sin_c = jnp.zeros((n_ctx, LANES), F32)
    blk = jnp.arange(TOEP) // HEAD_DIM
    bd = (blk[:, None] == blk[None, :]).astype(BF16)
    w_in_b = w_in.astype(BF16)
    w_out_b = w_out.astype(BF16)

    for l in range(depth):
        last = l == depth - 1
        mod_x = mod[l, :batch, None, :]
        mod_c = mod[l, batch:batch + 1, None, :]
        ng = norm_g[l][None, :]
        gq_a = jnp.concatenate([jnp.tile(qn_a[l], N_HEADS), jnp.tile(kn_a[l], N_KV)])[None, :]
        gq_c = jnp.concatenate([jnp.tile(qn_c[l], N_HEADS), jnp.tile(kn_c[l], N_KV)])[None, :]

        qa, ka, va, ga, hy, qc, kc, vc, gc = _projection(
            x, mod_x, ng, w_in_b[l], cos_l, sin_l, bd, gq_a, gq_c, tm=1024)
        if last:
            c_off = 2 * D_MODEL + Q_W
            w_kv = jnp.concatenate([w_in_b[l][:, Q_W:Q_W + 2 * KV_W],
                                    w_in_b[l][:, c_off:c_off + 2 * KV_W]], axis=1)
            gk = jnp.concatenate([jnp.tile(kn_a[l], N_KV), jnp.tile(kn_c[l], N_KV)])[None, :]
            ka_c, va_c, kc_c, vc_c = _kv_projection(ctx, mod_c, ng, w_kv, bd, gk, tm=256)
        else:
            qa_c, ka_c, va_c, ga_c, hy_c, qc_c, kc_c, vc_c, gc_c = _projection(
                ctx, mod_c, ng, w_in_b[l], cos_c, sin_c, bd, gq_a, gq_c, tm=256)

        a_out = _attention(qa, ga, ka_c, va_c, ka, va, None, "full", 256, 4, "attn_global")
        c_out = _attention(qc, gc, kc_c, vc_c, kc, vc, sink_c[l], "window", 256, 4,
                           "attn_window")
        taps = _hyena_filters(seq, hy_w1[l], hy_b1[l], hy_w2[l], hy_b2[l], hy_w3[l], hy_freq[l])
        b_out = _hyena(_to_channel_major(hy, n_blk), taps, hy_conv_w[l], hy_conv_b[l],
                       hy_bias[l], n_blk, batch)
        b_out = _to_token_major(b_out, n_blk, batch)
        x_new = _out_projection(x, mod_x[:, :, 2 * D_MODEL:], a_out, b_out, c_out, w_out_b[l], 1024)

        if not last:
            a_c = _attention(qa_c, ga_c, ka_c, va_c, None, None, None, None, 256, 1,
                             "attn_ctx_a")
            c_c = _attention(qc_c, gc_c, kc_c, vc_c, None, None, sink_c[l], None, 256, 1,
                             "attn_ctx_c")
            taps_c = _hyena_filters(n_ctx, hy_w1[l], hy_b1[l], hy_w2[l], hy_b2[l], hy_w3[l],
                                    hy_freq[l])
            b_c = _hyena(_to_channel_major(hy_c, 1), taps_c, hy_conv_w[l], hy_conv_b[l],
                         hy_bias[l], 1, batch)
            b_c = _to_token_major(b_c, 1, batch)
            ctx = _out_projection(ctx, mod_c[:, :, 2 * D_MODEL:], a_c, b_c, c_c, w_out_b[l], 256)
        x = x_new
    return x
```

```python
import functools
import math

import jax
import jax.numpy as jnp
from jax import lax
from jax.experimental import pallas as pl
from jax.experimental.pallas import tpu as pltpu

F32 = jnp.float32
BF16 = jnp.bfloat16

D_MODEL = 1024
HEAD_DIM = 64
N_HEADS = 6
N_KV = 2
Q_W = N_HEADS * HEAD_DIM
KV_W = N_KV * HEAD_DIM
HY_W = 256
GRID_W = 64
WINDOW = 128
ROPE_THETA = 10000.0
NORM_EPS = 1e-6
NEG_INF = -1e30
HY_BANDS = 16
HY_FAST_DECAY = 0.3
HY_SLOW_DECAY = 1.5
HY_TARGET = 1e-2
LANES = 128
TOEP = 256
FEAT_PAD = 128
ROPE_FREQS = HEAD_DIM // 4
ROPE_PAIR = 2 * ROPE_FREQS

PROJ_TM = 1024
PROJ_SUB = 256
ATTN_TQ = 256
ATTN_TILES = 4
HY_CW = 8
HY_UNROLL = 4
VMEM_LIMIT = 56 * 1024 * 1024

_NT = (((1,), (1,)), ((), ()))
LOG2E = 1.4426950408889634
Q_SCALE = LOG2E / math.sqrt(HEAD_DIM)


def _cparams(sem):
    return pltpu.CompilerParams(dimension_semantics=sem, vmem_limit_bytes=VMEM_LIMIT)


def _silu(x):
    return x * (1.0 / (1.0 + jnp.exp(-x)))


def _mod_kernel(c_ref, w_ref, b_ref, o_ref):
    s = _silu(c_ref[...])
    o_ref[0] = jnp.dot(s, w_ref[0], precision=lax.Precision.HIGHEST,
                       preferred_element_type=F32) + b_ref[0]


def _modulation(c_all, w_mod, b_mod):
    depth, d, n = w_mod.shape
    rows = c_all.shape[0]
    nb = n // d
    return pl.pallas_call(
        _mod_kernel,
        grid=(depth, nb),
        in_specs=[pl.BlockSpec((rows, d), lambda l, j: (0, 0)),
                  pl.BlockSpec((1, d, d), lambda l, j: (l, 0, j)),
                  pl.BlockSpec((1, 1, d), lambda l, j: (l, 0, j))],
        out_specs=pl.BlockSpec((1, rows, d), lambda l, j: (l, 0, j)),
        out_shape=jax.ShapeDtypeStruct((depth, rows, n), F32),
        compiler_params=_cparams(("arbitrary", "arbitrary")),
        name="modulation",
    )(c_all, w_mod, b_mod.reshape(depth, 1, n))


def _head_norm_rope(qk, gain, cos, sin, bd):
    width = qk.shape[1]
    sq = (qk * qk).astype(BF16)
    ss = jnp.concatenate(
        [jnp.dot(sq[:, s:s + TOEP], bd, preferred_element_type=F32)
         for s in range(0, width, TOEP)], axis=1)
    y = qk * lax.rsqrt(ss * (1.0 / HEAD_DIM) + NORM_EPS) * gain
    lane = lax.broadcasted_iota(jnp.int32, (1, LANES), 1)
    first_half = (lane % ROPE_PAIR) < ROPE_FREQS
    out = []
    for s in range(0, width, LANES):
        ys = y[:, s:s + LANES]
        partner = jnp.where(first_half, pltpu.roll(ys, LANES - ROPE_FREQS, axis=1),
                            pltpu.roll(ys, ROPE_FREQS, axis=1))
        out.append(ys * cos + partner * sin)
    return jnp.concatenate(out, axis=1)


def _with_swapped(t):
    return jnp.concatenate([t, pltpu.roll(t, HEAD_DIM, axis=1)], axis=1).astype(BF16)


def _mix_out(a, b, c, w_ref):
    acc = jnp.dot(a, w_ref[0:Q_W], preferred_element_type=F32)
    acc = acc + jnp.dot(b, w_ref[Q_W:Q_W + HY_W], preferred_element_type=F32)
    return acc + jnp.dot(c, w_ref[Q_W + HY_W:], preferred_element_type=F32)


def _proj_kernel(*refs, prev_out):
    if prev_out:
        gate_ref, a_ref, b_ref, c_ref, wo_ref = refs[:5]
        refs = refs[5:]
    (x_ref, mod_ref, ng_ref, w_ref, cos_ref, sin_ref, bd_ref, gq_a_ref, gq_c_ref) = refs[:9]
    outs = refs[9:]
    if prev_out:
        xo_ref, outs = outs[0], outs[1:]
    qa_ref, ka_ref, va_ref, ga_ref, hy_ref, qc_ref, kc_ref, vc_ref, gc_ref = outs
    shift = mod_ref[0, :, 0:D_MODEL]
    scale = mod_ref[0, :, D_MODEL:2 * D_MODEL]
    bd = bd_ref[...]
    tm = x_ref.shape[1]
    sub = min(tm, PROJ_SUB)
    for r0 in range(0, tm, sub):
        rows = slice(r0, r0 + sub)
        x = x_ref[0, rows]
        if prev_out:
            x = x + gate_ref[0] * _mix_out(a_ref[0, rows], b_ref[0, rows], c_ref[0, rows], wo_ref)
            xo_ref[0, rows] = x
        ms = jnp.mean(x * x, axis=-1, keepdims=True)
        y = x * lax.rsqrt(ms + NORM_EPS) * ng_ref[...]
        h = (y * (1.0 + scale) + shift).astype(BF16)
        cos = cos_ref[rows]
        sin = sin_ref[rows]

        def attn_branch(col0, gain_ref, q_ref, k_ref, v_ref, g_ref):
            acc = jnp.dot(h, w_ref[:, col0:col0 + D_MODEL], preferred_element_type=F32)
            qk = _head_norm_rope(acc[:, 0:Q_W + KV_W], gain_ref[...], cos, sin, bd)
            q_ref[0, rows] = (qk[:, 0:Q_W] * Q_SCALE).astype(BF16)
            k_ref[0, rows] = _with_swapped(qk[:, Q_W:Q_W + KV_W])
            v_ref[0, rows] = _with_swapped(acc[:, Q_W + KV_W:Q_W + 2 * KV_W])
            g_ref[0, rows] = _silu(acc[:, Q_W + 2 * KV_W:]).astype(BF16)

        attn_branch(0, gq_a_ref, qa_ref, ka_ref, va_ref, ga_ref)
        hy_ref[0, rows] = jnp.dot(h, w_ref[:, D_MODEL:2 * D_MODEL],
                                  preferred_element_type=F32).astype(BF16)
        attn_branch(2 * D_MODEL, gq_c_ref, qc_ref, kc_ref, vc_ref, gc_ref)


def _projection(x, mod, norm_g, w_in, cos, sin, bd, gq_a, gq_c, tm, prev=None):
    b, t, d = x.shape
    tok = lambda w: pl.BlockSpec((1, tm, w), lambda i, j: (i, j, 0))
    once = lambda a: pl.BlockSpec(a.shape, lambda i, j: (0,) * a.ndim,
                                  pipeline_mode=pl.Buffered(1))
    per_batch = lambda a: pl.BlockSpec((1, 1, a.shape[2]), (lambda i, j: (i, 0, 0))
                                       if a.shape[0] > 1 else (lambda i, j: (0, 0, 0)))
    widths = (Q_W, 2 * LANES, 2 * LANES, Q_W, D_MODEL, Q_W, 2 * LANES, 2 * LANES, Q_W)
    args = [x, mod, norm_g, w_in, cos, sin, bd, gq_a, gq_c]
    in_specs = [tok(d), per_batch(mod), once(norm_g), once(w_in),
                pl.BlockSpec((tm, LANES), lambda i, j: (j, 0)),
                pl.BlockSpec((tm, LANES), lambda i, j: (j, 0)),
                once(bd), once(gq_a), once(gq_c)]
    out_specs = [tok(w) for w in widths]
    out_shape = [jax.ShapeDtypeStruct((b, t, w), BF16) for w in widths]
    if prev is not None:
        gate, a, bmix, c, w_out = prev
        args = [gate, a, bmix, c, w_out] + args
        in_specs = [per_batch(gate), tok(Q_W), tok(HY_W), tok(Q_W), once(w_out)] + in_specs
        out_specs = [tok(d)] + out_specs
        out_shape = [jax.ShapeDtypeStruct((b, t, d), F32)] + out_shape
    return pl.pallas_call(
        functools.partial(_proj_kernel, prev_out=prev is not None),
        grid=(b, t // tm),
        in_specs=in_specs,
        out_specs=out_specs,
        out_shape=out_shape,
        compiler_params=_cparams(("parallel", "arbitrary")),
        name="in_projection",
    )(*args)


def _proj_kv_kernel(x_ref, mod_ref, ng_ref, w_ref, bd_ref, gk_ref, ka_ref, va_ref, kc_ref, vc_ref):
    x = x_ref[0]
    ms = jnp.mean(x * x, axis=-1, keepdims=True)
    y = x * lax.rsqrt(ms + NORM_EPS) * ng_ref[...]
    h = (y * (1.0 + mod_ref[0, :, D_MODEL:2 * D_MODEL]) + mod_ref[0, :, 0:D_MODEL]).astype(BF16)
    acc = jnp.dot(h, w_ref[...], preferred_element_type=F32)
    k = jnp.concatenate([acc[:, 0:KV_W], acc[:, 2 * KV_W:3 * KV_W]], axis=1)
    ss = jnp.dot((k * k).astype(BF16), bd_ref[...], preferred_element_type=F32)
    k = k * lax.rsqrt(ss * (1.0 / HEAD_DIM) + NORM_EPS) * gk_ref[...]
    ka_ref[0] = _with_swapped(k[:, 0:KV_W])
    kc_ref[0] = _with_swapped(k[:, KV_W:])
    va_ref[0] = _with_swapped(acc[:, KV_W:2 * KV_W])
    vc_ref[0] = _with_swapped(acc[:, 3 * KV_W:])


def _kv_projection(x, mod, norm_g, w_kv, bd, gk, tm):
    b, t, d = x.shape
    tok = lambda w: pl.BlockSpec((1, tm, w), lambda i, j: (i, j, 0))
    full = lambda a: pl.BlockSpec(a.shape, lambda i, j: (0,) * a.ndim)
    return pl.pallas_call(
        _proj_kv_kernel,
        grid=(b, t // tm),
        in_specs=[tok(d), full(mod), full(norm_g), full(w_kv), full(bd), full(gk)],
        out_specs=[tok(2 * LANES)] * 4,
        out_shape=[jax.ShapeDtypeStruct((b, t, 2 * LANES), BF16)] * 4,
        compiler_params=_cparams(("parallel", "arbitrary")),
        name="kv_projection",
    )(x, mod, norm_g, w_kv, bd, gk)


def _attn_kernel(*refs, tq, n_sub, lat_mode, has_sink, n_ctx, lat_len):
    refs = list(refs)
    sink_ref = refs.pop(0) if has_sink else None
    q_ref, g_ref, kc_ref, vc_ref = refs[:4]
    refs = refs[4:]
    if lat_mode is not None:
        kl_ref, vl_ref = refs[:2]
        refs = refs[2:]
    o_ref, kpad_s, vaug_s = refs
    i = pl.program_id(1)

    @pl.when(i == 0)
    def _stage_keys():
        lo = lax.broadcasted_iota(jnp.int32, (1, LANES), 1) < HEAD_DIM

        def fill(row0, n, k_ref, v_ref):
            rows = slice(row0, row0 + n)
            for dst, src, width in ((kpad_s, k_ref, LANES), (vaug_s, v_ref, 2 * LANES)):
                t = src[0, :, 0:LANES]
                sw = src[0, :, LANES:2 * LANES]
                zero = jnp.zeros_like(t)
                for blk, val in enumerate((jnp.where(lo, t, zero), jnp.where(lo, zero, sw),
                                           jnp.where(lo, sw, zero), jnp.where(lo, zero, t))):
                    dst[rows, blk * width:blk * width + LANES] = val

        fill(0, n_ctx, kc_ref, vc_ref)
        if lat_mode is not None:
            fill(n_ctx, lat_len, kl_ref, vl_ref)

    @pl.when((i == 0) & (pl.program_id(0) == 0))
    def _stage_ones():
        for blk in range(2 * N_KV):
            vaug_s[:, (2 * blk + 1) * LANES:(2 * blk + 2) * LANES] = jnp.ones(
                (vaug_s.shape[0], LANES), BF16)

    for sub, pair in [(s_, p_) for s_ in range(n_sub) for p_ in range(N_HEADS // 2)]:
        q_rows = slice(sub * tq, (sub + 1) * tq)
        if lat_mode == "window" and pair == 0:
            span = tq + 2 * WINDOW
            q0 = (i * n_sub + sub) * tq
            start = jnp.clip(q0 - WINDOW, 0, lat_len - span)
            qpos = q0 + lax.broadcasted_iota(jnp.int32, (tq, span), 0)
            kpos = start + lax.broadcasted_iota(jnp.int32, (tq, span), 1)
            valid = jnp.abs(qpos - kpos) <= WINDOW
            band = pl.ds(pl.multiple_of(n_ctx + start, WINDOW), span)
            ctx_rows = slice(0, n_ctx)
        lanes = slice(pair * LANES, (pair + 1) * LANES)
        qp = q_ref[0, q_rows, lanes]
        acc = jnp.zeros((tq, LANES), F32)
        for parity in range(2):
            head = 2 * pair + parity
            blk = 2 * (head // (N_HEADS // N_KV)) + parity
            kcols = slice(blk * LANES, (blk + 1) * LANES)
            vcols = slice(2 * blk * LANES, 2 * (blk + 1) * LANES)
            if lat_mode == "window":
                s_c = lax.dot_general(qp, kpad_s[ctx_rows, kcols], _NT,
                                      preferred_element_type=F32)
                s_l = lax.dot_general(qp, kpad_s[band, kcols], _NT, preferred_element_type=F32)
                s_l = jnp.where(valid, s_l, NEG_INF)
                m = jnp.maximum(jnp.max(s_c, axis=-1, keepdims=True),
                                jnp.max(s_l, axis=-1, keepdims=True))
            else:
                s = lax.dot_general(qp, kpad_s[:, kcols], _NT, preferred_element_type=F32)
                m = jnp.max(s, axis=-1, keepdims=True)
            if has_sink:
                sink = sink_ref[head] * LOG2E
                m = jnp.maximum(m, sink)
            if lat_mode == "window":
                o = jnp.dot(jnp.exp2(s_c - m).astype(BF16), vaug_s[ctx_rows, vcols],
                            preferred_element_type=F32)
                o = o + jnp.dot(jnp.exp2(s_l - m).astype(BF16), vaug_s[band, vcols],
                                preferred_element_type=F32)
            else:
                o = jnp.dot(jnp.exp2(s - m).astype(BF16), vaug_s[:, vcols],
                            preferred_element_type=F32)
            den = o[:, LANES:]
            if has_sink:
                den = den + jnp.exp2(sink - m)
            acc = acc + o[:, :LANES] / den
        o_ref[0, q_rows, lanes] = (acc * g_ref[0, q_rows, lanes].astype(F32)).astype(BF16)


def _attention(q, g, k_ctx, v_ctx, k_lat, v_lat, sink, lat_mode, tq, n_sub, name):
    b, t, _ = q.shape
    n_ctx = k_ctx.shape[1]
    has_sink = sink is not None
    tok = pl.BlockSpec((1, n_sub * tq, Q_W), lambda i, j: (i, j, 0))
    whole = lambda n: pl.BlockSpec((1, n, 2 * LANES), lambda i, j: (i, 0, 0))
    args, specs = [], []
    if has_sink:
        args.append(sink)
        specs.append(pl.BlockSpec(memory_space=pltpu.SMEM))
    args += [q, g, k_ctx, v_ctx]
    specs += [tok, tok, whole(n_ctx), whole(n_ctx)]
    lat_len = 0
    if lat_mode is not None:
        lat_len = k_lat.shape[1]
        args += [k_lat, v_lat]
        specs += [whole(lat_len), whole(lat_len)]
    n_keys = n_ctx + lat_len
    return pl.pallas_call(
        functools.partial(_attn_kernel, tq=tq, n_sub=n_sub, lat_mode=lat_mode,
                          has_sink=has_sink, n_ctx=n_ctx, lat_len=lat_len),
        grid=(b, t // (n_sub * tq)),
        in_specs=specs,
        out_specs=tok,
        out_shape=jax.ShapeDtypeStruct((b, t, Q_W), BF16),
        scratch_shapes=[pltpu.VMEM((n_keys, 2 * N_KV * LANES), BF16),
                        pltpu.VMEM((n_keys, 4 * N_KV * LANES), BF16)],
        compiler_params=_cparams(("arbitrary", "arbitrary")),
        name=name,
    )(*args)


def _filter_kernel(z_ref, w1_ref, b1_ref, w2_ref, b2_ref, w3_ref, fr_ref, dl_ref, o_ref, *, lf):
    hp = lax.Precision.HIGHEST
    z = z_ref[...]
    h = jnp.sin(fr_ref[0:1, :] * (jnp.dot(z, w1_ref[...], precision=hp,
                                          preferred_element_type=F32) + b1_ref[...]))
    h = jnp.sin(fr_ref[1:2, :] * (jnp.dot(h, w2_ref[...], precision=hp,
                                          preferred_element_type=F32) + b2_ref[...]))
    decay = jnp.exp(-z[:, 0:1] * dl_ref[...])
    first_row = lax.broadcasted_iota(jnp.int32, decay.shape, 0) == 0
    for order in range(2):
        c0 = 2 * order * HY_W
        bwd = jnp.dot(h[:lf], w3_ref[:, c0 + HY_W:c0 + 2 * HY_W], precision=hp,
                      preferred_element_type=F32)
        fwd = jnp.dot(h[lf:], w3_ref[:, c0:c0 + HY_W], precision=hp,
                      preferred_element_type=F32)
        taps = jnp.where(first_row, 0.0, jnp.concatenate([bwd, fwd], axis=0) * decay)
        taps = taps / jnp.sum(jnp.abs(taps), axis=0, keepdims=True)
        t = taps.T.astype(BF16).astype(F32)
        lo = pltpu.bitcast(t, jnp.uint32) >> 16
        hi = pltpu.bitcast(pltpu.roll(t, 1, axis=1), jnp.uint32) & jnp.uint32(0xFFFF0000)
        o_ref[order] = pltpu.bitcast(lo | hi, jnp.int32)


def _hyena_filters(lf, w1, b1, w2, b2, w3, freq):
    t = jnp.linspace(0.0, 1.0, lf, dtype=F32)[:, None]
    bands = jnp.linspace(1e-4, HY_BANDS - 1, HY_BANDS, dtype=F32)
    w = 2.0 * math.pi * jnp.arange(lf, dtype=F32)[:, None] / lf
    z = jnp.concatenate([t, jnp.cos(bands * w), jnp.sin(bands * w)], axis=-1)
    z = jnp.concatenate([z[:1], z[:0:-1], z], axis=0)
    z = jnp.pad(z, ((0, 0), (0, FEAT_PAD - z.shape[1])))
    w1p = jnp.pad(w1, ((0, FEAT_PAD - w1.shape[0]), (0, 0)))
    min_decay = math.log(HY_TARGET) / HY_SLOW_DECAY
    max_decay = math.log(HY_TARGET) / HY_FAST_DECAY
    deltas = jnp.abs(jnp.linspace(min_decay, max_decay, HY_W, dtype=F32))[None, :]
    full = lambda a: pl.BlockSpec(a.shape, lambda o: (0,) * a.ndim)
    ins = (z, w1p, b1[None, :], w2, b2[None, :], w3, freq, deltas)
    return pl.pallas_call(
        functools.partial(_filter_kernel, lf=lf),
        grid=(1,),
        in_specs=[full(a) for a in ins],
        out_specs=pl.BlockSpec((2, HY_W, 2 * lf), lambda o: (0, 0, 0)),
        out_shape=jax.ShapeDtypeStruct((2, HY_W, 2 * lf), jnp.int32),
        compiler_params=_cparams(("arbitrary",)),
        name="hyena_filter",
    )(*ins)


def _hyena_kernel(cw_ref, cb_ref, db_ref, v_ref, x1_ref, x2_ref, g_ref, taps_ref, shift_ref,
                  o_ref, *, cw, n_blk, batch):
    rows = n_blk * batch
    c0 = pl.program_id(0) * cw
    lane = lax.broadcasted_iota(jnp.int32, (1, TOEP), 1)
    zero_blk = jnp.zeros((batch, TOEP), F32)

    def short_conv(p_b, ch):
        shifted = jnp.dot(p_b, shift_ref[...], preferred_element_type=F32)
        p = p_b.astype(F32)
        prev = shifted[:, :TOEP]
        nxt = shifted[:, TOEP:]
        if n_blk > 1:
            prev_edge = jnp.concatenate([zero_blk, prev[:rows - batch]], axis=0)
            next_edge = jnp.concatenate([nxt[batch:], zero_blk], axis=0)
        else:
            prev_edge = jnp.zeros_like(p)
            next_edge = jnp.zeros_like(p)
        prev = jnp.where(lane == 0, prev_edge, prev)
        nxt = jnp.where(lane == TOEP - 1, next_edge, nxt)
        return cb_ref[ch] + cw_ref[0, ch] * prev + cw_ref[1, ch] * p + cw_ref[2, ch] * nxt

    def long_conv(z, order, ci):
        words = taps_ref[order, ci]
        skew = pltpu.roll(jnp.broadcast_to(words, (TOEP // 2, words.shape[1])), 0, axis=1,
                          stride=2, stride_axis=0)
        skew = pltpu.bitcast(skew, BF16)
        zb = z.astype(BF16)
        acc = [jnp.zeros((batch, TOEP), F32) for _ in range(n_blk)]
        for d in range(-(n_blk - 1), n_blk):
            n_out = n_blk - abs(d)
            src = max(0, -d) * batch
            col = (n_blk + d) * TOEP
            part = jnp.dot(zb[src:src + n_out * batch], skew[:, col:col + TOEP],
                           preferred_element_type=F32)
            for k in range(n_out):
                acc[max(0, d) + k] = acc[max(0, d) + k] + part[k * batch:(k + 1) * batch]
        return jnp.concatenate(acc, axis=0) if n_blk > 1 else acc[0]

    def body(k, carry):
        cis = [k * HY_UNROLL + u for u in range(HY_UNROLL)]
        zs = [short_conv(v_ref[ci], c0 + ci) for ci in cis]
        for order, x_ref in enumerate((x1_ref, x2_ref)):
            xs = [short_conv(x_ref[ci], (order + 1) * HY_W + c0 + ci) for ci in cis]
            ys = [long_conv(z, order, ci) for z, ci in zip(zs, cis)]
            zs = [x * (y + z * db_ref[order, c0 + ci]) for x, y, z, ci in zip(xs, ys, zs, cis)]
        for z, ci in zip(zs, cis):
            o_ref[ci] = (z * _silu(g_ref[ci].astype(F32))).astype(o_ref.dtype)
        return carry

    lax.fori_loop(0, cw // HY_UNROLL, body, 0)


def _hyena(hy_t, taps, conv_w, conv_b, d_bias, n_blk, batch, cw=HY_CW):
    rows = n_blk * batch
    smem = pl.BlockSpec(memory_space=pltpu.SMEM)
    slab = lambda off: pl.BlockSpec((cw, rows, TOEP), lambda c: (off // cw + c, 0, 0))
    taps4 = taps.reshape(2, HY_W, 1, taps.shape[-1])
    pos = jnp.arange(TOEP)
    shift = jnp.concatenate([pos[:, None] == (pos[None, :] - 1) % TOEP,
                             pos[:, None] == (pos[None, :] + 1) % TOEP], axis=1).astype(BF16)
    return pl.pallas_call(
        functools.partial(_hyena_kernel, cw=cw, n_blk=n_blk, batch=batch),
        grid=(HY_W // cw,),
        in_specs=[smem, smem, smem, slab(0), slab(HY_W), slab(2 * HY_W), slab(3 * HY_W),
                  pl.BlockSpec((2, cw, 1, taps.shape[-1]), lambda c: (0, c, 0, 0)),
                  pl.BlockSpec((TOEP, 2 * TOEP), lambda c: (0, 0))],
        out_specs=pl.BlockSpec((cw, rows, TOEP), lambda c: (c, 0, 0)),
        out_shape=jax.ShapeDtypeStruct((HY_W, rows, TOEP), BF16),
        compiler_params=_cparams(("arbitrary",)),
        name="hyena_mixer",
    )(conv_w, conv_b, d_bias, hy_t, hy_t, hy_t, hy_t, taps4, shift)


def _out_kernel(x_ref, gate_ref, a_ref, b_ref, c_ref, w_ref, o_ref):
    o_ref[0] = x_ref[0] + gate_ref[0] * _mix_out(a_ref[0], b_ref[0], c_ref[0], w_ref)


def _out_projection(x, gate, a, bmix, c, w_out, tm):
    b, t, d = x.shape
    per_batch = gate.shape[0] > 1
    tok = lambda w: pl.BlockSpec((1, tm, w), lambda i, j: (i, j, 0))
    return pl.pallas_call(
        _out_kernel,
        grid=(b, t // tm),
        in_specs=[tok(d),
                  pl.BlockSpec((1, 1, d), (lambda i, j: (i, 0, 0)) if per_batch
                               else (lambda i, j: (0, 0, 0))),
                  tok(Q_W), tok(HY_W), tok(Q_W),
                  pl.BlockSpec(w_out.shape, lambda i, j: (0, 0))],
        out_specs=tok(d),
        out_shape=jax.ShapeDtypeStruct((b, t, d), F32),
        compiler_params=_cparams(("parallel", "arbitrary")),
        name="out_projection",
    )(x, gate, a, bmix, c, w_out)


def _rope_tables(t_len):
    pos = jnp.arange(t_len)
    n_freq = ROPE_FREQS
    inv_freq = ROPE_THETA ** (-jnp.arange(n_freq, dtype=F32) / n_freq)
    ang = jnp.stack([(pos // GRID_W).astype(F32)[:, None] * inv_freq,
                     (pos % GRID_W).astype(F32)[:, None] * inv_freq], axis=1)
    cos = jnp.cos(ang)[:, :, None, :]
    sin = jnp.sin(ang)[:, :, None, :]
    cos = jnp.broadcast_to(cos, (t_len, 2, 2, n_freq)).reshape(t_len, HEAD_DIM)
    sin = jnp.concatenate([-sin, sin], axis=2).reshape(t_len, HEAD_DIM)
    return jnp.tile(cos, (1, 2)), jnp.tile(sin, (1, 2))


def _to_channel_major(hy, n_blk):
    b, t, c = hy.shape
    return hy.reshape(b, n_blk, TOEP, c).transpose(3, 1, 0, 2).reshape(c, n_blk * b, TOEP)


def _to_token_major(y, n_blk, batch):
    c = y.shape[0]
    return y.reshape(c, n_blk, batch, TOEP).transpose(2, 1, 3, 0).reshape(batch, n_blk * TOEP, c)


def kernel(x, c, ctx, c_ctx, norm_g, w_mod, b_mod, w_in, w_out, qn_a, kn_a, qn_c, kn_c, sink_c,
           hy_conv_w, hy_conv_b, hy_w1, hy_b1, hy_w2, hy_b2, hy_w3, hy_freq, hy_bias):
    depth = w_in.shape[0]
    batch, seq, _ = x.shape
    n_ctx = ctx.shape[1]
    n_blk = seq // TOEP
    assert x.shape[2] == D_MODEL and seq % (ATTN_TQ * ATTN_TILES) == 0 and seq % GRID_W == 0
    assert seq % TOEP == 0 and n_ctx == TOEP and seq >= ATTN_TQ + 2 * WINDOW
    assert batch % 16 == 0 and HY_W % HY_CW == 0 and HY_CW % HY_UNROLL == 0

    rows = -(-(batch + 1) // 8) * 8
    c_all = jnp.concatenate([c, c_ctx[None]], axis=0)
    c_all = jnp.pad(c_all, ((0, rows - batch - 1), (0, 0)))
    mod = _modulation(c_all, w_mod, b_mod)

    cos_l, sin_l = _rope_tables(seq)
    cos_c = jnp.ones((n_ctx, LANES), F32)
    sin_c = jnp.zeros((n_ctx, LANES), F32)
    blk = jnp.arange(TOEP) // HEAD_DIM
    bd = (blk[:, None] == blk[None, :]).astype(BF16)
    w_in_b = w_in.astype(BF16)
    w_out_b = w_out.astype(BF16)

    pending = None
    for l in range(depth):
        last = l == depth - 1
        mod_x = mod[l, :batch, None, :]
        mod_c = mod[l, batch:batch + 1, None, :]
        ng = norm_g[l][None, :]
        gq_a = jnp.concatenate([jnp.tile(qn_a[l], N_HEADS), jnp.tile(kn_a[l], N_KV)])[None, :]
        gq_c = jnp.concatenate([jnp.tile(qn_c[l], N_HEADS), jnp.tile(kn_c[l], N_KV)])[None, :]

        outs = _projection(x, mod_x, ng, w_in_b[l], cos_l, sin_l, bd, gq_a, gq_c,
                           tm=min(seq, PROJ_TM), prev=pending)
        if pending is not None:
            x, outs = outs[0], outs[1:]
        qa, ka, va, ga, hy, qc, kc, vc, gc = outs
        if last:
            c_off = 2 * D_MODEL + Q_W
            w_kv = jnp.concatenate([w_in_b[l][:, Q_W:Q_W + 2 * KV_W],
                                    w_in_b[l][:, c_off:c_off + 2 * KV_W]], axis=1)
            gk = jnp.concatenate([jnp.tile(kn_a[l], N_KV), jnp.tile(kn_c[l], N_KV)])[None, :]
            ka_c, va_c, kc_c, vc_c = _kv_projection(ctx, mod_c, ng, w_kv, bd, gk, tm=n_ctx)
        else:
            qa_c, ka_c, va_c, ga_c, hy_c, qc_c, kc_c, vc_c, gc_c = _projection(
                ctx, mod_c, ng, w_in_b[l], cos_c, sin_c, bd, gq_a, gq_c, tm=n_ctx)

        a_out = _attention(qa, ga, ka_c, va_c, ka, va, None, "full", ATTN_TQ, ATTN_TILES,
                           "attn_global")
        c_out = _attention(qc, gc, kc_c, vc_c, kc, vc, sink_c[l], "window", ATTN_TQ, ATTN_TILES,
                           "attn_window")
        taps = _hyena_filters(seq, hy_w1[l], hy_b1[l], hy_w2[l], hy_b2[l], hy_w3[l], hy_freq[l])
        b_out = _hyena(_to_channel_major(hy, n_blk), taps, hy_conv_w[l], hy_conv_b[l],
                       hy_bias[l], n_blk, batch)
        b_out = _to_token_major(b_out, n_blk, batch)
        pending = (mod_x[:, :, 2 * D_MODEL:], a_out, b_out, c_out, w_out_b[l])

        if not last:
            a_c = _attention(qa_c, ga_c, ka_c, va_c, None, None, None, None, n_ctx, 1,
                             "attn_ctx_a")
            c_c = _attention(qc_c, gc_c, kc_c, vc_c, None, None, sink_c[l], None, n_ctx, 1,
                             "attn_ctx_c")
            taps_c = _hyena_filters(n_ctx, hy_w1[l], hy_b1[l], hy_w2[l], hy_b2[l], hy_w3[l],
                                    hy_freq[l])
            b_c = _hyena(_to_channel_major(hy_c, 1), taps_c, hy_conv_w[l], hy_conv_b[l],
                         hy_bias[l], 1, batch)
            b_c = _to_token_major(b_c, 1, batch)
            ctx = _out_projection(ctx, mod_c[:, :, 2 * D_MODEL:], a_c, b_c, c_c, w_out_b[l], n_ctx)
    gate, a_out, b_out, c_out, w_o = pending
    return _out_projection(x, gate, a_out, b_out, c_out, w_o, min(seq, PROJ_TM))
```

```python
import functools
import math

import jax
import jax.numpy as jnp
from jax import lax
from jax.experimental import pallas as pl
from jax.experimental.pallas import tpu as pltpu

F32 = jnp.float32
BF16 = jnp.bfloat16

D_MODEL = 1024
HEAD_DIM = 64
N_HEADS = 6
N_KV = 2
Q_W = N_HEADS * HEAD_DIM
KV_W = N_KV * HEAD_DIM
HY_W = 256
GRID_W = 64
WINDOW = 128
ROPE_THETA = 10000.0
NORM_EPS = 1e-6
NEG_INF = -1e30
HY_BANDS = 16
HY_FAST_DECAY = 0.3
HY_SLOW_DECAY = 1.5
HY_TARGET = 1e-2
LANES = 128
TOEP = 256
FEAT_PAD = 128
ROPE_FREQS = HEAD_DIM // 4
ROPE_PAIR = 2 * ROPE_FREQS

PROJ_TM = 1024
PROJ_SUB = 256
ATTN_TQ = 256
ATTN_TILES = 8
HY_CW = 8
HY_UNROLL = 4
VMEM_LIMIT = 56 * 1024 * 1024

_NT = (((1,), (1,)), ((), ()))
LOG2E = 1.4426950408889634
Q_SCALE = LOG2E / math.sqrt(HEAD_DIM)


def _cparams(sem):
    return pltpu.CompilerParams(dimension_semantics=sem, vmem_limit_bytes=VMEM_LIMIT)


def _silu(x):
    return x * (1.0 / (1.0 + jnp.exp(-x)))


def _mod_kernel(c_ref, w_ref, b_ref, o_ref):
    s = _silu(c_ref[...])
    o_ref[0] = jnp.dot(s, w_ref[0], precision=lax.Precision.HIGHEST,
                       preferred_element_type=F32) + b_ref[0]


def _modulation(c_all, w_mod, b_mod):
    depth, d, n = w_mod.shape
    rows = c_all.shape[0]
    nb = n // d
    return pl.pallas_call(
        _mod_kernel,
        grid=(depth, nb),
        in_specs=[pl.BlockSpec((rows, d), lambda l, j: (0, 0)),
                  pl.BlockSpec((1, d, d), lambda l, j: (l, 0, j)),
                  pl.BlockSpec((1, 1, d), lambda l, j: (l, 0, j))],
        out_specs=pl.BlockSpec((1, rows, d), lambda l, j: (l, 0, j)),
        out_shape=jax.ShapeDtypeStruct((depth, rows, n), F32),
        compiler_params=_cparams(("arbitrary", "arbitrary")),
        name="modulation",
    )(c_all, w_mod, b_mod.reshape(depth, 1, n))


def _head_norm_rope(qk, gain, cos, sin, bd):
    width = qk.shape[1]
    sq = (qk * qk).astype(BF16)
    ss = jnp.concatenate(
        [jnp.dot(sq[:, s:s + TOEP], bd, preferred_element_type=F32)
         for s in range(0, width, TOEP)], axis=1)
    y = qk * lax.rsqrt(ss * (1.0 / HEAD_DIM) + NORM_EPS) * gain
    lane = lax.broadcasted_iota(jnp.int32, (1, LANES), 1)
    first_half = (lane % ROPE_PAIR) < ROPE_FREQS
    out = []
    for s in range(0, width, LANES):
        ys = y[:, s:s + LANES]
        partner = jnp.where(first_half, pltpu.roll(ys, LANES - ROPE_FREQS, axis=1),
                            pltpu.roll(ys, ROPE_FREQS, axis=1))
        out.append(ys * cos + partner * sin)
    return jnp.concatenate(out, axis=1)


def _with_swapped(t):
    return jnp.concatenate([t, pltpu.roll(t, HEAD_DIM, axis=1)], axis=1).astype(BF16)


def _mix_out(a, b, c, w_ref):
    acc = jnp.dot(a, w_ref[0:Q_W], preferred_element_type=F32)
    acc = acc + jnp.dot(b, w_ref[Q_W:Q_W + HY_W], preferred_element_type=F32)
    return acc + jnp.dot(c, w_ref[Q_W + HY_W:], preferred_element_type=F32)


def _proj_kernel(*refs, prev_out):
    if prev_out:
        gate_ref, a_ref, b_ref, c_ref, wo_ref = refs[:5]
        refs = refs[5:]
    (x_ref, mod_ref, ng_ref, w_ref, cos_ref, sin_ref, bd_ref, gq_a_ref, gq_c_ref) = refs[:9]
    outs = refs[9:]
    if prev_out:
        xo_ref, outs = outs[0], outs[1:]
    qa_ref, ka_ref, va_ref, ga_ref, hy_ref, qc_ref, kc_ref, vc_ref, gc_ref = outs
    shift = mod_ref[0, :, 0:D_MODEL]
    scale = mod_ref[0, :, D_MODEL:2 * D_MODEL]
    bd = bd_ref[...]
    tm = x_ref.shape[1]
    sub = min(tm, PROJ_SUB)
    for r0 in range(0, tm, sub):
        rows = slice(r0, r0 + sub)
        x = x_ref[0, rows]
        if prev_out:
            x = x + gate_ref[0] * _mix_out(a_ref[0, rows], b_ref[0, rows], c_ref[0, rows], wo_ref)
            xo_ref[0, rows] = x
        ms = jnp.mean(x * x, axis=-1, keepdims=True)
        y = x * lax.rsqrt(ms + NORM_EPS) * ng_ref[...]
        h = (y * (1.0 + scale) + shift).astype(BF16)
        cos = cos_ref[rows]
        sin = sin_ref[rows]

        def attn_branch(col0, gain_ref, q_ref, k_ref, v_ref, g_ref):
            acc = jnp.dot(h, w_ref[:, col0:col0 + D_MODEL], preferred_element_type=F32)
            qk = _head_norm_rope(acc[:, 0:Q_W + KV_W], gain_ref[...], cos, sin, bd)
            q_ref[0, rows] = (qk[:, 0:Q_W] * Q_SCALE).astype(BF16)
            k_ref[0, rows] = _with_swapped(qk[:, Q_W:Q_W + KV_W])
            v_ref[0, rows] = _with_swapped(acc[:, Q_W + KV_W:Q_W + 2 * KV_W])
            g_ref[0, rows] = _silu(acc[:, Q_W + 2 * KV_W:]).astype(BF16)

        attn_branch(0, gq_a_ref, qa_ref, ka_ref, va_ref, ga_ref)
        hy_ref[0, rows] = jnp.dot(h, w_ref[:, D_MODEL:2 * D_MODEL],
                                  preferred_element_type=F32).astype(BF16)
        attn_branch(2 * D_MODEL, gq_c_ref, qc_ref, kc_ref, vc_ref, gc_ref)


def _projection(x, mod, norm_g, w_in, cos, sin, bd, gq_a, gq_c, tm, prev=None):
    b, t, d = x.shape
    tok = lambda w: pl.BlockSpec((1, tm, w), lambda i, j: (i, j, 0))
    once = lambda a: pl.BlockSpec(a.shape, lambda i, j: (0,) * a.ndim,
                                  pipeline_mode=pl.Buffered(1))
    per_batch = lambda a: pl.BlockSpec((1, 1, a.shape[2]), (lambda i, j: (i, 0, 0))
                                       if a.shape[0] > 1 else (lambda i, j: (0, 0, 0)))
    widths = (Q_W, 2 * LANES, 2 * LANES, Q_W, D_MODEL, Q_W, 2 * LANES, 2 * LANES, Q_W)
    args = [x, mod, norm_g, w_in, cos, sin, bd, gq_a, gq_c]
    in_specs = [tok(d), per_batch(mod), once(norm_g), once(w_in),
                pl.BlockSpec((tm, LANES), lambda i, j: (j, 0)),
                pl.BlockSpec((tm, LANES), lambda i, j: (j, 0)),
                once(bd), once(gq_a), once(gq_c)]
    out_specs = [tok(w) for w in widths]
    out_shape = [jax.ShapeDtypeStruct((b, t, w), BF16) for w in widths]
    if prev is not None:
        gate, a, bmix, c, w_out = prev
        args = [gate, a, bmix, c, w_out] + args
        in_specs = [per_batch(gate), tok(Q_W), tok(HY_W), tok(Q_W), once(w_out)] + in_specs
        out_specs = [tok(d)] + out_specs
        out_shape = [jax.ShapeDtypeStruct((b, t, d), F32)] + out_shape
    return pl.pallas_call(
        functools.partial(_proj_kernel, prev_out=prev is not None),
        grid=(b, t // tm),
        in_specs=in_specs,
        out_specs=out_specs,
        out_shape=out_shape,
        compiler_params=_cparams(("parallel", "arbitrary")),
        name="in_projection",
    )(*args)


def _proj_kv_kernel(x_ref, mod_ref, ng_ref, w_ref, bd_ref, gk_ref, ka_ref, va_ref, kc_ref, vc_ref):
    x = x_ref[0]
    ms = jnp.mean(x * x, axis=-1, keepdims=True)
    y = x * lax.rsqrt(ms + NORM_EPS) * ng_ref[...]
    h = (y * (1.0 + mod_ref[0, :, D_MODEL:2 * D_MODEL]) + mod_ref[0, :, 0:D_MODEL]).astype(BF16)
    acc = jnp.dot(h, w_ref[...], preferred_element_type=F32)
    k = jnp.concatenate([acc[:, 0:KV_W], acc[:, 2 * KV_W:3 * KV_W]], axis=1)
    ss = jnp.dot((k * k).astype(BF16), bd_ref[...], preferred_element_type=F32)
    k = k * lax.rsqrt(ss * (1.0 / HEAD_DIM) + NORM_EPS) * gk_ref[...]
    ka_ref[0] = _with_swapped(k[:, 0:KV_W])
    kc_ref[0] = _with_swapped(k[:, KV_W:])
    va_ref[0] = _with_swapped(acc[:, KV_W:2 * KV_W])
    vc_ref[0] = _with_swapped(acc[:, 3 * KV_W:])


def _kv_projection(x, mod, norm_g, w_kv, bd, gk, tm):
    b, t, d = x.shape
    tok = lambda w: pl.BlockSpec((1, tm, w), lambda i, j: (i, j, 0))
    full = lambda a: pl.BlockSpec(a.shape, lambda i, j: (0,) * a.ndim)
    return pl.pallas_call(
        _proj_kv_kernel,
        grid=(b, t // tm),
        in_specs=[tok(d), full(mod), full(norm_g), full(w_kv), full(bd), full(gk)],
        out_specs=[tok(2 * LANES)] * 4,
        out_shape=[jax.ShapeDtypeStruct((b, t, 2 * LANES), BF16)] * 4,
        compiler_params=_cparams(("parallel", "arbitrary")),
        name="kv_projection",
    )(x, mod, norm_g, w_kv, bd, gk)


def _attn_kernel(*refs, tq, n_sub, lat_mode, has_sink, n_ctx, lat_len):
    refs = list(refs)
    sink_ref = refs.pop(0) if has_sink else None
    q_ref, g_ref, kc_ref, vc_ref = refs[:4]
    refs = refs[4:]
    if lat_mode is not None:
        kl_ref, vl_ref = refs[:2]
        refs = refs[2:]
    o_ref, kpad_s, vaug_s = refs
    i = pl.program_id(1)

    @pl.when(i == 0)
    def _stage_keys():
        lo = lax.broadcasted_iota(jnp.int32, (1, LANES), 1) < HEAD_DIM

        def fill(row0, n, k_ref, v_ref):
            rows = slice(row0, row0 + n)
            for dst, src, width in ((kpad_s, k_ref, LANES), (vaug_s, v_ref, 2 * LANES)):
                t = src[0, :, 0:LANES]
                sw = src[0, :, LANES:2 * LANES]
                zero = jnp.zeros_like(t)
                for blk, val in enumerate((jnp.where(lo, t, zero), jnp.where(lo, zero, sw),
                                           jnp.where(lo, sw, zero), jnp.where(lo, zero, t))):
                    dst[rows, blk * width:blk * width + LANES] = val

        fill(0, n_ctx, kc_ref, vc_ref)
        if lat_mode is not None:
            fill(n_ctx, lat_len, kl_ref, vl_ref)

    @pl.when((i == 0) & (pl.program_id(0) == 0))
    def _stage_ones():
        for blk in range(2 * N_KV):
            vaug_s[:, (2 * blk + 1) * LANES:(2 * blk + 2) * LANES] = jnp.ones(
                (vaug_s.shape[0], LANES), BF16)

    for sub, pair in [(s_, p_) for s_ in range(n_sub) for p_ in range(N_HEADS // 2)]:
        q_rows = slice(sub * tq, (sub + 1) * tq)
        if lat_mode == "window" and pair == 0:
            span = tq + 2 * WINDOW
            q0 = (i * n_sub + sub) * tq
            start = jnp.clip(q0 - WINDOW, 0, lat_len - span)
            qpos = q0 + lax.broadcasted_iota(jnp.int32, (tq, span), 0)
            kpos = start + lax.broadcasted_iota(jnp.int32, (tq, span), 1)
            valid = jnp.abs(qpos - kpos) <= WINDOW
            band = pl.ds(pl.multiple_of(n_ctx + start, WINDOW), span)
            ctx_rows = slice(0, n_ctx)
        lanes = slice(pair * LANES, (pair + 1) * LANES)
        qp = q_ref[0, q_rows, lanes]
        acc = jnp.zeros((tq, LANES), F32)
        for parity in range(2):
            head = 2 * pair + parity
            blk = 2 * (head // (N_HEADS // N_KV)) + parity
            kcols = slice(blk * LANES, (blk + 1) * LANES)
            vcols = slice(2 * blk * LANES, 2 * (blk + 1) * LANES)
            if lat_mode == "window":
                s_c = lax.dot_general(qp, kpad_s[ctx_rows, kcols], _NT,
                                      preferred_element_type=F32)
                s_l = lax.dot_general(qp, kpad_s[band, kcols], _NT, preferred_element_type=F32)
                s_l = jnp.where(valid, s_l, NEG_INF)
                m = jnp.maximum(jnp.max(s_c, axis=-1, keepdims=True),
                                jnp.max(s_l, axis=-1, keepdims=True))
            else:
                s = lax.dot_general(qp, kpad_s[:, kcols], _NT, preferred_element_type=F32)
                m = jnp.max(s, axis=-1, keepdims=True)
            if has_sink:
                sink = sink_ref[head] * LOG2E
                m = jnp.maximum(m, sink)
            if lat_mode == "window":
                o = jnp.dot(jnp.exp2(s_c - m).astype(BF16), vaug_s[ctx_rows, vcols],
                            preferred_element_type=F32)
                o = o + jnp.dot(jnp.exp2(s_l - m).astype(BF16), vaug_s[band, vcols],
                                preferred_element_type=F32)
            else:
                o = jnp.dot(jnp.exp2(s - m).astype(BF16), vaug_s[:, vcols],
                            preferred_element_type=F32)
            den = o[:, LANES:]
            if has_sink:
                den = den + jnp.exp2(sink - m)
            acc = acc + o[:, :LANES] / den
        o_ref[0, q_rows, lanes] = (acc * g_ref[0, q_rows, lanes].astype(F32)).astype(BF16)


def _attention(q, g, k_ctx, v_ctx, k_lat, v_lat, sink, lat_mode, tq, n_sub, name):
    b, t, _ = q.shape
    n_ctx = k_ctx.shape[1]
    has_sink = sink is not None
    tok = pl.BlockSpec((1, n_sub * tq, Q_W), lambda i, j: (i, j, 0))
    whole = lambda n: pl.BlockSpec((1, n, 2 * LANES), lambda i, j: (i, 0, 0))
    args, specs = [], []
    if has_sink:
        args.append(sink)
        specs.append(pl.BlockSpec(memory_space=pltpu.SMEM))
    args += [q, g, k_ctx, v_ctx]
    specs += [tok, tok, whole(n_ctx), whole(n_ctx)]
    lat_len = 0
    if lat_mode is not None:
        lat_len = k_lat.shape[1]
        args += [k_lat, v_lat]
        specs += [whole(lat_len), whole(lat_len)]
    n_keys = n_ctx + lat_len
    return pl.pallas_call(
        functools.partial(_attn_kernel, tq=tq, n_sub=n_sub, lat_mode=lat_mode,
                          has_sink=has_sink, n_ctx=n_ctx, lat_len=lat_len),
        grid=(b, t // (n_sub * tq)),
        in_specs=specs,
        out_specs=tok,
        out_shape=jax.ShapeDtypeStruct((b, t, Q_W), BF16),
        scratch_shapes=[pltpu.VMEM((n_keys, 2 * N_KV * LANES), BF16),
                        pltpu.VMEM((n_keys, 4 * N_KV * LANES), BF16)],
        compiler_params=_cparams(("arbitrary", "arbitrary")),
        name=name,
    )(*args)


def _filter_kernel(z_ref, w1_ref, b1_ref, w2_ref, b2_ref, w3_ref, fr_ref, dl_ref, o_ref, *, lf):
    hp = lax.Precision.HIGHEST
    z = z_ref[...]
    h = jnp.sin(fr_ref[0:1, :] * (jnp.dot(z, w1_ref[...], precision=hp,
                                          preferred_element_type=F32) + b1_ref[...]))
    h = jnp.sin(fr_ref[1:2, :] * (jnp.dot(h, w2_ref[...], precision=hp,
                                          preferred_element_type=F32) + b2_ref[...]))
    decay = jnp.exp(-z[:, 0:1] * dl_ref[...])
    first_row = lax.broadcasted_iota(jnp.int32, decay.shape, 0) == 0
    for order in range(2):
        c0 = 2 * order * HY_W
        bwd = jnp.dot(h[:lf], w3_ref[:, c0 + HY_W:c0 + 2 * HY_W], precision=hp,
                      preferred_element_type=F32)
        fwd = jnp.dot(h[lf:], w3_ref[:, c0:c0 + HY_W], precision=hp,
                      preferred_element_type=F32)
        taps = jnp.where(first_row, 0.0, jnp.concatenate([bwd, fwd], axis=0) * decay)
        taps = taps / jnp.sum(jnp.abs(taps), axis=0, keepdims=True)
        t = taps.T.astype(BF16).astype(F32)
        lo = pltpu.bitcast(t, jnp.uint32) >> 16
        hi = pltpu.bitcast(pltpu.roll(t, 1, axis=1), jnp.uint32) & jnp.uint32(0xFFFF0000)
        o_ref[order] = pltpu.bitcast(lo | hi, jnp.int32)


def _hyena_filters(lf, w1, b1, w2, b2, w3, freq):
    t = jnp.linspace(0.0, 1.0, lf, dtype=F32)[:, None]
    bands = jnp.linspace(1e-4, HY_BANDS - 1, HY_BANDS, dtype=F32)
    w = 2.0 * math.pi * jnp.arange(lf, dtype=F32)[:, None] / lf
    z = jnp.concatenate([t, jnp.cos(bands * w), jnp.sin(bands * w)], axis=-1)
    z = jnp.concatenate([z[:1], z[:0:-1], z], axis=0)
    z = jnp.pad(z, ((0, 0), (0, FEAT_PAD - z.shape[1])))
    w1p = jnp.pad(w1, ((0, FEAT_PAD - w1.shape[0]), (0, 0)))
    min_decay = math.log(HY_TARGET) / HY_SLOW_DECAY
    max_decay = math.log(HY_TARGET) / HY_FAST_DECAY
    deltas = jnp.abs(jnp.linspace(min_decay, max_decay, HY_W, dtype=F32))[None, :]
    full = lambda a: pl.BlockSpec(a.shape, lambda o: (0,) * a.ndim)
    ins = (z, w1p, b1[None, :], w2, b2[None, :], w3, freq, deltas)
    return pl.pallas_call(
        functools.partial(_filter_kernel, lf=lf),
        grid=(1,),
        in_specs=[full(a) for a in ins],
        out_specs=pl.BlockSpec((2, HY_W, 2 * lf), lambda o: (0, 0, 0)),
        out_shape=jax.ShapeDtypeStruct((2, HY_W, 2 * lf), jnp.int32),
        compiler_params=_cparams(("arbitrary",)),
        name="hyena_filter",
    )(*ins)


def _hyena_kernel(cw_ref, cb_ref, db_ref, v_ref, x1_ref, x2_ref, g_ref, taps_ref, shift_ref,
                  o_ref, *, cw, n_blk, batch):
    rows = n_blk * batch
    c0 = pl.program_id(0) * cw
    lane = lax.broadcasted_iota(jnp.int32, (1, TOEP), 1)
    zero_blk = jnp.zeros((batch, TOEP), F32)

    def short_conv(p_b, ch):
        shifted = jnp.dot(p_b, shift_ref[...], preferred_element_type=F32)
        p = p_b.astype(F32)
        prev = shifted[:, :TOEP]
        nxt = shifted[:, TOEP:]
        if n_blk > 1:
            prev_edge = jnp.concatenate([zero_blk, prev[:rows - batch]], axis=0)
            next_edge = jnp.concatenate([nxt[batch:], zero_blk], axis=0)
        else:
            prev_edge = jnp.zeros_like(p)
            next_edge = jnp.zeros_like(p)
        prev = jnp.where(lane == 0, prev_edge, prev)
        nxt = jnp.where(lane == TOEP - 1, next_edge, nxt)
        return cb_ref[ch] + cw_ref[0, ch] * prev + cw_ref[1, ch] * p + cw_ref[2, ch] * nxt

    def long_conv(z, order, ci):
        words = taps_ref[order, ci]
        skew = pltpu.roll(jnp.broadcast_to(words, (TOEP // 2, words.shape[1])), 0, axis=1,
                          stride=2, stride_axis=0)
        skew = pltpu.bitcast(skew, BF16)
        zb = z.astype(BF16)
        acc = [jnp.zeros((batch, TOEP), F32) for _ in range(n_blk)]
        for d in range(-(n_blk - 1), n_blk):
            n_out = n_blk - abs(d)
            src = max(0, -d) * batch
            col = (n_blk + d) * TOEP
            part = jnp.dot(zb[src:src + n_out * batch], skew[:, col:col + TOEP],
                           preferred_element_type=F32)
            for k in range(n_out):
                acc[max(0, d) + k] = acc[max(0, d) + k] + part[k * batch:(k + 1) * batch]
        return jnp.concatenate(acc, axis=0) if n_blk > 1 else acc[0]

    def body(k, carry):
        cis = [k * HY_UNROLL + u for u in range(HY_UNROLL)]
        zs = [short_conv(v_ref[ci], c0 + ci) for ci in cis]
        for order, x_ref in enumerate((x1_ref, x2_ref)):
            xs = [short_conv(x_ref[ci], (order + 1) * HY_W + c0 + ci) for ci in cis]
            ys = [long_conv(z, order, ci) for z, ci in zip(zs, cis)]
            zs = [x * (y + z * db_ref[order, c0 + ci]) for x, y, z, ci in zip(xs, ys, zs, cis)]
        for z, ci in zip(zs, cis):
            o_ref[ci] = (z * _silu(g_ref[ci].astype(F32))).astype(o_ref.dtype)
        return carry

    lax.fori_loop(0, cw // HY_UNROLL, body, 0)


def _hyena(hy_t, taps, conv_w, conv_b, d_bias, n_blk, batch, cw=HY_CW):
    rows = n_blk * batch
    smem = pl.BlockSpec(memory_space=pltpu.SMEM)
    slab = lambda off: pl.BlockSpec((cw, rows, TOEP), lambda c: (off // cw + c, 0, 0))
    taps4 = taps.reshape(2, HY_W, 1, taps.shape[-1])
    pos = jnp.arange(TOEP)
    shift = jnp.concatenate([pos[:, None] == (pos[None, :] - 1) % TOEP,
                             pos[:, None] == (pos[None, :] + 1) % TOEP], axis=1).astype(BF16)
    return pl.pallas_call(
        functools.partial(_hyena_kernel, cw=cw, n_blk=n_blk, batch=batch),
        grid=(HY_W // cw,),
        in_specs=[smem, smem, smem, slab(0), slab(HY_W), slab(2 * HY_W), slab(3 * HY_W),
                  pl.BlockSpec((2, cw, 1, taps.shape[-1]), lambda c: (0, c, 0, 0)),
                  pl.BlockSpec((TOEP, 2 * TOEP), lambda c: (0, 0))],
        out_specs=pl.BlockSpec((cw, rows, TOEP), lambda c: (c, 0, 0)),
        out_shape=jax.ShapeDtypeStruct((HY_W, rows, TOEP), BF16),
        compiler_params=_cparams(("arbitrary",)),
        name="hyena_mixer",
    )(conv_w, conv_b, d_bias, hy_t, hy_t, hy_t, hy_t, taps4, shift)


def _out_kernel(x_ref, gate_ref, a_ref, b_ref, c_ref, w_ref, o_ref):
    o_ref[0] = x_ref[0] + gate_ref[0] * _mix_out(a_ref[0], b_ref[0], c_ref[0], w_ref)


def _out_projection(x, gate, a, bmix, c, w_out, tm):
    b, t, d = x.shape
    per_batch = gate.shape[0] > 1
    tok = lambda w: pl.BlockSpec((1, tm, w), lambda i, j: (i, j, 0))
    return pl.pallas_call(
        _out_kernel,
        grid=(b, t // tm),
        in_specs=[tok(d),
                  pl.BlockSpec((1, 1, d), (lambda i, j: (i, 0, 0)) if per_batch
                               else (lambda i, j: (0, 0, 0))),
                  tok(Q_W), tok(HY_W), tok(Q_W),
                  pl.BlockSpec(w_out.shape, lambda i, j: (0, 0))],
        out_specs=tok(d),
        out_shape=jax.ShapeDtypeStruct((b, t, d), F32),
        compiler_params=_cparams(("parallel", "arbitrary")),
        name="out_projection",
    )(x, gate, a, bmix, c, w_out)


def _rope_tables(t_len):
    pos = jnp.arange(t_len)
    n_freq = ROPE_FREQS
    inv_freq = ROPE_THETA ** (-jnp.arange(n_freq, dtype=F32) / n_freq)
    ang = jnp.stack([(pos // GRID_W).astype(F32)[:, None] * inv_freq,
                     (pos % GRID_W).astype(F32)[:, None] * inv_freq], axis=1)
    cos = jnp.cos(ang)[:, :, None, :]
    sin = jnp.sin(ang)[:, :, None, :]
    cos = jnp.broadcast_to(cos, (t_len, 2, 2, n_freq)).reshape(t_len, HEAD_DIM)
    sin = jnp.concatenate([-sin, sin], axis=2).reshape(t_len, HEAD_DIM)
    return jnp.tile(cos, (1, 2)), jnp.tile(sin, (1, 2))


def _to_channel_major(hy, n_blk):
    b, t, c = hy.shape
    return hy.reshape(b, n_blk, TOEP, c).transpose(3, 1, 0, 2).reshape(c, n_blk * b, TOEP)


def _to_token_major(y, n_blk, batch):
    c = y.shape[0]
    return y.reshape(c, n_blk, batch, TOEP).transpose(2, 1, 3, 0).reshape(batch, n_blk * TOEP, c)


def kernel(x, c, ctx, c_ctx, norm_g, w_mod, b_mod, w_in, w_out, qn_a, kn_a, qn_c, kn_c, sink_c,
           hy_conv_w, hy_conv_b, hy_w1, hy_b1, hy_w2, hy_b2, hy_w3, hy_freq, hy_bias):
    depth = w_in.shape[0]
    batch, seq, _ = x.shape
    n_ctx = ctx.shape[1]
    n_blk = seq // TOEP
    assert x.shape[2] == D_MODEL and seq % (ATTN_TQ * ATTN_TILES) == 0 and seq % GRID_W == 0
    assert seq % TOEP == 0 and n_ctx == TOEP and seq >= ATTN_TQ + 2 * WINDOW
    assert batch % 16 == 0 and HY_W % HY_CW == 0 and HY_CW % HY_UNROLL == 0

    rows = -(-(batch + 1) // 8) * 8
    c_all = jnp.concatenate([c, c_ctx[None]], axis=0)
    c_all = jnp.pad(c_all, ((0, rows - batch - 1), (0, 0)))
    mod = _modulation(c_all, w_mod, b_mod)

    cos_l, sin_l = _rope_tables(seq)
    cos_c = jnp.ones((n_ctx, LANES), F32)
    sin_c = jnp.zeros((n_ctx, LANES), F32)
    blk = jnp.arange(TOEP) // HEAD_DIM
    bd = (blk[:, None] == blk[None, :]).astype(BF16)
    w_in_b = w_in.astype(BF16)
    w_out_b = w_out.astype(BF16)

    pending = None
    for l in range(depth):
        last = l == depth - 1
        mod_x = mod[l, :batch, None, :]
        mod_c = mod[l, batch:batch + 1, None, :]
        ng = norm_g[l][None, :]
        gq_a = jnp.concatenate([jnp.tile(qn_a[l], N_HEADS), jnp.tile(kn_a[l], N_KV)])[None, :]
        gq_c = jnp.concatenate([jnp.tile(qn_c[l], N_HEADS), jnp.tile(kn_c[l], N_KV)])[None, :]

        outs = _projection(x, mod_x, ng, w_in_b[l], cos_l, sin_l, bd, gq_a, gq_c,
                           tm=min(seq, PROJ_TM), prev=pending)
        if pending is not None:
            x, outs = outs[0], outs[1:]
        qa, ka, va, ga, hy, qc, kc, vc, gc = outs
        if last:
            c_off = 2 * D_MODEL + Q_W
            w_kv = jnp.concatenate([w_in_b[l][:, Q_W:Q_W + 2 * KV_W],
                                    w_in_b[l][:, c_off:c_off + 2 * KV_W]], axis=1)
            gk = jnp.concatenate([jnp.tile(kn_a[l], N_KV), jnp.tile(kn_c[l], N_KV)])[None, :]
            ka_c, va_c, kc_c, vc_c = _kv_projection(ctx, mod_c, ng, w_kv, bd, gk, tm=n_ctx)
        else:
            qa_c, ka_c, va_c, ga_c, hy_c, qc_c, kc_c, vc_c, gc_c = _projection(
                ctx, mod_c, ng, w_in_b[l], cos_c, sin_c, bd, gq_a, gq_c, tm=n_ctx)

        a_out = _attention(qa, ga, ka_c, va_c, ka, va, None, "full", ATTN_TQ, ATTN_TILES,
                           "attn_global")
        c_out = _attention(qc, gc, kc_c, vc_c, kc, vc, sink_c[l], "window", ATTN_TQ, ATTN_TILES,
                           "attn_window")
        taps = _hyena_filters(seq, hy_w1[l], hy_b1[l], hy_w2[l], hy_b2[l], hy_w3[l], hy_freq[l])
        b_out = _hyena(_to_channel_major(hy, n_blk), taps, hy_conv_w[l], hy_conv_b[l],
                       hy_bias[l], n_blk, batch)
        b_out = _to_token_major(b_out, n_blk, batch)
        pending = (mod_x[:, :, 2 * D_MODEL:], a_out, b_out, c_out, w_out_b[l])

        if not last:
            a_c = _attention(qa_c, ga_c, ka_c, va_c, None, None, None, None, n_ctx, 1,
                             "attn_ctx_a")
            c_c = _attention(qc_c, gc_c, kc_c, vc_c, None, None, sink_c[l], None, n_ctx, 1,
                             "attn_ctx_c")
            taps_c = _hyena_filters(n_ctx, hy_w1[l], hy_b1[l], hy_w2[l], hy_b2[l], hy_w3[l],
                                    hy_freq[l])
            b_c = _hyena(_to_channel_major(hy_c, 1), taps_c, hy_conv_w[l], hy_conv_b[l],
                         hy_bias[l], 1, batch)
            b_c = _to_token_major(b_c, 1, batch)
            ctx = _out_projection(ctx, mod_c[:, :, 2 * D_MODEL:], a_c, b_c, c_c, w_out_b[l], n_ctx)
    gate, a_out, b_out, c_out, w_o = pending
    return _out_projection(x, gate, a_out, b_out, c_out, w_o, min(seq, PROJ_TM))
```

```python
import functools
import math

import jax
import jax.numpy as jnp
from jax import lax
from jax.experimental import pallas as pl
from jax.experimental.pallas import tpu as pltpu

F32 = jnp.float32
BF16 = jnp.bfloat16

D_MODEL = 1024
HEAD_DIM = 64
N_HEADS = 6
N_KV = 2
Q_W = N_HEADS * HEAD_DIM
KV_W = N_KV * HEAD_DIM
HY_W = 256
GRID_W = 64
WINDOW = 128
ROPE_THETA = 10000.0
NORM_EPS = 1e-6
NEG_INF = -1e30
HY_BANDS = 16
HY_FAST_DECAY = 0.3
HY_SLOW_DECAY = 1.5
HY_TARGET = 1e-2
LANES = 128
TOEP = 256
FEAT_PAD = 128
ROPE_FREQS = HEAD_DIM // 4
ROPE_PAIR = 2 * ROPE_FREQS

PROJ_TM = 1024
PROJ_SUB = 256
ATTN_TQ = 256
ATTN_TILES = 8
KEY_CHUNK = 768
HY_CW = 8
HY_UNROLL = 4
VMEM_LIMIT = 56 * 1024 * 1024

_NT = (((1,), (1,)), ((), ()))
LOG2E = 1.4426950408889634
Q_SCALE = LOG2E / math.sqrt(HEAD_DIM)


def _cparams(sem):
    return pltpu.CompilerParams(dimension_semantics=sem, vmem_limit_bytes=VMEM_LIMIT)


def _silu(x):
    return x * (1.0 / (1.0 + jnp.exp(-x)))


def _mod_kernel(c_ref, w_ref, b_ref, o_ref):
    s = _silu(c_ref[...])
    o_ref[0] = jnp.dot(s, w_ref[0], precision=lax.Precision.HIGHEST,
                       preferred_element_type=F32) + b_ref[0]


def _modulation(c_all, w_mod, b_mod):
    depth, d, n = w_mod.shape
    rows = c_all.shape[0]
    nb = n // d
    return pl.pallas_call(
        _mod_kernel,
        grid=(depth, nb),
        in_specs=[pl.BlockSpec((rows, d), lambda l, j: (0, 0)),
                  pl.BlockSpec((1, d, d), lambda l, j: (l, 0, j)),
                  pl.BlockSpec((1, 1, d), lambda l, j: (l, 0, j))],
        out_specs=pl.BlockSpec((1, rows, d), lambda l, j: (l, 0, j)),
        out_shape=jax.ShapeDtypeStruct((depth, rows, n), F32),
        compiler_params=_cparams(("arbitrary", "arbitrary")),
        name="modulation",
    )(c_all, w_mod, b_mod.reshape(depth, 1, n))


def _head_norm_rope(qk, gain, cos, sin, bd):
    width = qk.shape[1]
    sq = (qk * qk).astype(BF16)
    ss = jnp.concatenate(
        [jnp.dot(sq[:, s:s + TOEP], bd, preferred_element_type=F32)
         for s in range(0, width, TOEP)], axis=1)
    y = qk * lax.rsqrt(ss * (1.0 / HEAD_DIM) + NORM_EPS) * gain
    lane = lax.broadcasted_iota(jnp.int32, (1, LANES), 1)
    first_half = (lane % ROPE_PAIR) < ROPE_FREQS
    out = []
    for s in range(0, width, LANES):
        ys = y[:, s:s + LANES]
        partner = jnp.where(first_half, pltpu.roll(ys, LANES - ROPE_FREQS, axis=1),
                            pltpu.roll(ys, ROPE_FREQS, axis=1))
        out.append(ys * cos + partner * sin)
    return jnp.concatenate(out, axis=1)


def _with_swapped(t):
    return jnp.concatenate([t, pltpu.roll(t, HEAD_DIM, axis=1)], axis=1).astype(BF16)


def _mix_out(a, b, c, w_ref):
    acc = jnp.dot(a, w_ref[0:Q_W], preferred_element_type=F32)
    acc = acc + jnp.dot(b, w_ref[Q_W:Q_W + HY_W], preferred_element_type=F32)
    return acc + jnp.dot(c, w_ref[Q_W + HY_W:], preferred_element_type=F32)


def _proj_kernel(*refs, prev_out):
    if prev_out:
        gate_ref, a_ref, b_ref, c_ref, wo_ref = refs[:5]
        refs = refs[5:]
    (x_ref, mod_ref, ng_ref, w_ref, cos_ref, sin_ref, bd_ref, gq_a_ref, gq_c_ref) = refs[:9]
    outs = refs[9:]
    if prev_out:
        xo_ref, outs = outs[0], outs[1:]
    qa_ref, ka_ref, va_ref, ga_ref, hy_ref, qc_ref, kc_ref, vc_ref, gc_ref = outs
    shift = mod_ref[0, :, 0:D_MODEL]
    scale = mod_ref[0, :, D_MODEL:2 * D_MODEL]
    bd = bd_ref[...]
    tm = x_ref.shape[1]
    sub = min(tm, PROJ_SUB)
    for r0 in range(0, tm, sub):
        rows = slice(r0, r0 + sub)
        x = x_ref[0, rows]
        if prev_out:
            x = x + gate_ref[0] * _mix_out(a_ref[0, rows], b_ref[0, rows], c_ref[0, rows], wo_ref)
            xo_ref[0, rows] = x
        ms = jnp.mean(x * x, axis=-1, keepdims=True)
        y = x * lax.rsqrt(ms + NORM_EPS) * ng_ref[...]
        h = (y * (1.0 + scale) + shift).astype(BF16)
        cos = cos_ref[rows]
        sin = sin_ref[rows]

        def attn_branch(col0, gain_ref, q_ref, k_ref, v_ref, g_ref):
            acc = jnp.dot(h, w_ref[:, col0:col0 + D_MODEL], preferred_element_type=F32)
            qk = _head_norm_rope(acc[:, 0:Q_W + KV_W], gain_ref[...], cos, sin, bd)
            q_ref[0, rows] = (qk[:, 0:Q_W] * Q_SCALE).astype(BF16)
            k_ref[0, rows] = _with_swapped(qk[:, Q_W:Q_W + KV_W])
            v_ref[0, rows] = _with_swapped(acc[:, Q_W + KV_W:Q_W + 2 * KV_W])
            g_ref[0, rows] = _silu(acc[:, Q_W + 2 * KV_W:]).astype(BF16)

        attn_branch(0, gq_a_ref, qa_ref, ka_ref, va_ref, ga_ref)
        hy_ref[0, rows] = jnp.dot(h, w_ref[:, D_MODEL:2 * D_MODEL],
                                  preferred_element_type=F32).astype(BF16)
        attn_branch(2 * D_MODEL, gq_c_ref, qc_ref, kc_ref, vc_ref, gc_ref)


def _projection(x, mod, norm_g, w_in, cos, sin, bd, gq_a, gq_c, tm, prev=None):
    b, t, d = x.shape
    tok = lambda w: pl.BlockSpec((1, tm, w), lambda i, j: (i, j, 0))
    once = lambda a: pl.BlockSpec(a.shape, lambda i, j: (0,) * a.ndim,
                                  pipeline_mode=pl.Buffered(1))
    per_batch = lambda a: pl.BlockSpec((1, 1, a.shape[2]), (lambda i, j: (i, 0, 0))
                                       if a.shape[0] > 1 else (lambda i, j: (0, 0, 0)))
    widths = (Q_W, 2 * LANES, 2 * LANES, Q_W, D_MODEL, Q_W, 2 * LANES, 2 * LANES, Q_W)
    args = [x, mod, norm_g, w_in, cos, sin, bd, gq_a, gq_c]
    in_specs = [tok(d), per_batch(mod), once(norm_g), once(w_in),
                pl.BlockSpec((tm, LANES), lambda i, j: (j, 0)),
                pl.BlockSpec((tm, LANES), lambda i, j: (j, 0)),
                once(bd), once(gq_a), once(gq_c)]
    out_specs = [tok(w) for w in widths]
    out_shape = [jax.ShapeDtypeStruct((b, t, w), BF16) for w in widths]
    if prev is not None:
        gate, a, bmix, c, w_out = prev
        args = [gate, a, bmix, c, w_out] + args
        in_specs = [per_batch(gate), tok(Q_W), tok(HY_W), tok(Q_W), once(w_out)] + in_specs
        out_specs = [tok(d)] + out_specs
        out_shape = [jax.ShapeDtypeStruct((b, t, d), F32)] + out_shape
    return pl.pallas_call(
        functools.partial(_proj_kernel, prev_out=prev is not None),
        grid=(b, t // tm),
        in_specs=in_specs,
        out_specs=out_specs,
        out_shape=out_shape,
        compiler_params=_cparams(("parallel", "arbitrary")),
        name="in_projection",
    )(*args)


def _proj_kv_kernel(x_ref, mod_ref, ng_ref, w_ref, bd_ref, gk_ref, ka_ref, va_ref, kc_ref, vc_ref):
    x = x_ref[0]
    ms = jnp.mean(x * x, axis=-1, keepdims=True)
    y = x * lax.rsqrt(ms + NORM_EPS) * ng_ref[...]
    h = (y * (1.0 + mod_ref[0, :, D_MODEL:2 * D_MODEL]) + mod_ref[0, :, 0:D_MODEL]).astype(BF16)
    acc = jnp.dot(h, w_ref[...], preferred_element_type=F32)
    k = jnp.concatenate([acc[:, 0:KV_W], acc[:, 2 * KV_W:3 * KV_W]], axis=1)
    ss = jnp.dot((k * k).astype(BF16), bd_ref[...], preferred_element_type=F32)
    k = k * lax.rsqrt(ss * (1.0 / HEAD_DIM) + NORM_EPS) * gk_ref[...]
    ka_ref[0] = _with_swapped(k[:, 0:KV_W])
    kc_ref[0] = _with_swapped(k[:, KV_W:])
    va_ref[0] = _with_swapped(acc[:, KV_W:2 * KV_W])
    vc_ref[0] = _with_swapped(acc[:, 3 * KV_W:])


def _kv_projection(x, mod, norm_g, w_kv, bd, gk, tm):
    b, t, d = x.shape
    tok = lambda w: pl.BlockSpec((1, tm, w), lambda i, j: (i, j, 0))
    full = lambda a: pl.BlockSpec(a.shape, lambda i, j: (0,) * a.ndim)
    return pl.pallas_call(
        _proj_kv_kernel,
        grid=(b, t // tm),
        in_specs=[tok(d), full(mod), full(norm_g), full(w_kv), full(bd), full(gk)],
        out_specs=[tok(2 * LANES)] * 4,
        out_shape=[jax.ShapeDtypeStruct((b, t, 2 * LANES), BF16)] * 4,
        compiler_params=_cparams(("parallel", "arbitrary")),
        name="kv_projection",
    )(x, mod, norm_g, w_kv, bd, gk)


def _attn_kernel(*refs, tq, n_sub, lat_mode, has_sink, n_ctx, lat_len):
    refs = list(refs)
    sink_ref = refs.pop(0) if has_sink else None
    q_ref, g_ref, kc_ref, vc_ref = refs[:4]
    refs = refs[4:]
    if lat_mode is not None:
        kl_ref, vl_ref = refs[:2]
        refs = refs[2:]
    o_ref, kpad_s, vaug_s = refs
    i = pl.program_id(1)

    @pl.when(i == 0)
    def _stage_keys():
        lo = lax.broadcasted_iota(jnp.int32, (1, LANES), 1) < HEAD_DIM

        def fill(row0, n, k_ref, v_ref):
            rows = slice(row0, row0 + n)
            for dst, src, width in ((kpad_s, k_ref, LANES), (vaug_s, v_ref, 2 * LANES)):
                t = src[0, :, 0:LANES]
                sw = src[0, :, LANES:2 * LANES]
                zero = jnp.zeros_like(t)
                for blk, val in enumerate((jnp.where(lo, t, zero), jnp.where(lo, zero, sw),
                                           jnp.where(lo, sw, zero), jnp.where(lo, zero, t))):
                    dst[rows, blk * width:blk * width + LANES] = val

        fill(0, n_ctx, kc_ref, vc_ref)
        if lat_mode is not None:
            fill(n_ctx, lat_len, kl_ref, vl_ref)

    @pl.when((i == 0) & (pl.program_id(0) == 0))
    def _stage_ones():
        for blk in range(2 * N_KV):
            vaug_s[:, (2 * blk + 1) * LANES:(2 * blk + 2) * LANES] = jnp.ones(
                (vaug_s.shape[0], LANES), BF16)

    n_keys = n_ctx + lat_len
    for sub, pair in [(s_, p_) for s_ in range(n_sub) for p_ in range(N_HEADS // 2)]:
        q_rows = slice(sub * tq, (sub + 1) * tq)
        if lat_mode == "window":
            if pair == 0:
                span = tq + 2 * WINDOW
                q0 = (i * n_sub + sub) * tq
                start = jnp.clip(q0 - WINDOW, 0, lat_len - span)
                qpos = q0 + lax.broadcasted_iota(jnp.int32, (tq, span), 0)
                kpos = start + lax.broadcasted_iota(jnp.int32, (tq, span), 1)
                valid = jnp.abs(qpos - kpos) <= WINDOW
                band = pl.ds(pl.multiple_of(n_ctx + start, WINDOW), span)
            key_rows = [(slice(0, n_ctx), None), (band, valid)]
        else:
            key_rows = [(slice(r, min(r + KEY_CHUNK, n_keys)), None)
                        for r in range(0, n_keys, KEY_CHUNK)]
        lanes = slice(pair * LANES, (pair + 1) * LANES)
        qp = q_ref[0, q_rows, lanes]
        acc = jnp.zeros((tq, LANES), F32)
        for parity in range(2):
            head = 2 * pair + parity
            blk = 2 * (head // (N_HEADS // N_KV)) + parity
            kcols = slice(blk * LANES, (blk + 1) * LANES)
            vcols = slice(2 * blk * LANES, 2 * (blk + 1) * LANES)
            scores = []
            for rows, mask in key_rows:
                s = lax.dot_general(qp, kpad_s[rows, kcols], _NT, preferred_element_type=F32)
                scores.append(s if mask is None else jnp.where(mask, s, NEG_INF))
            m = functools.reduce(jnp.maximum,
                                 [jnp.max(s, axis=-1, keepdims=True) for s in scores])
            if has_sink:
                sink = sink_ref[head] * LOG2E
                m = jnp.maximum(m, sink)
            o = functools.reduce(jnp.add, [
                jnp.dot(jnp.exp2(s - m).astype(BF16), vaug_s[rows, vcols],
                        preferred_element_type=F32)
                for s, (rows, _) in zip(scores, key_rows)])
            den = o[:, LANES:]
            if has_sink:
                den = den + jnp.exp2(sink - m)
            acc = acc + o[:, :LANES] / den
        o_ref[0, q_rows, lanes] = (acc * g_ref[0, q_rows, lanes].astype(F32)).astype(BF16)


def _attention(q, g, k_ctx, v_ctx, k_lat, v_lat, sink, lat_mode, tq, n_sub, name):
    b, t, _ = q.shape
    n_ctx = k_ctx.shape[1]
    has_sink = sink is not None
    tok = pl.BlockSpec((1, n_sub * tq, Q_W), lambda i, j: (i, j, 0))
    whole = lambda n: pl.BlockSpec((1, n, 2 * LANES), lambda i, j: (i, 0, 0))
    args, specs = [], []
    if has_sink:
        args.append(sink)
        specs.append(pl.BlockSpec(memory_space=pltpu.SMEM))
    args += [q, g, k_ctx, v_ctx]
    specs += [tok, tok, whole(n_ctx), whole(n_ctx)]
    lat_len = 0
    if lat_mode is not None:
        lat_len = k_lat.shape[1]
        args += [k_lat, v_lat]
        specs += [whole(lat_len), whole(lat_len)]
    n_keys = n_ctx + lat_len
    return pl.pallas_call(
        functools.partial(_attn_kernel, tq=tq, n_sub=n_sub, lat_mode=lat_mode,
                          has_sink=has_sink, n_ctx=n_ctx, lat_len=lat_len),
        grid=(b, t // (n_sub * tq)),
        in_specs=specs,
        out_specs=tok,
        out_shape=jax.ShapeDtypeStruct((b, t, Q_W), BF16),
        scratch_shapes=[pltpu.VMEM((n_keys, 2 * N_KV * LANES), BF16),
                        pltpu.VMEM((n_keys, 4 * N_KV * LANES), BF16)],
        compiler_params=_cparams(("arbitrary", "arbitrary")),
        name=name,
    )(*args)


def _filter_kernel(z_ref, w1_ref, b1_ref, w2_ref, b2_ref, w3_ref, fr_ref, dl_ref, o_ref, *, lf):
    hp = lax.Precision.HIGHEST
    z = z_ref[...]
    h = jnp.sin(fr_ref[0:1, :] * (jnp.dot(z, w1_ref[...], precision=hp,
                                          preferred_element_type=F32) + b1_ref[...]))
    h = jnp.sin(fr_ref[1:2, :] * (jnp.dot(h, w2_ref[...], precision=hp,
                                          preferred_element_type=F32) + b2_ref[...]))
    decay = jnp.exp(-z[:, 0:1] * dl_ref[...])
    first_row = lax.broadcasted_iota(jnp.int32, decay.shape, 0) == 0
    for order in range(2):
        c0 = 2 * order * HY_W
        bwd = jnp.dot(h[:lf], w3_ref[:, c0 + HY_W:c0 + 2 * HY_W], precision=hp,
                      preferred_element_type=F32)
        fwd = jnp.dot(h[lf:], w3_ref[:, c0:c0 + HY_W], precision=hp,
                      preferred_element_type=F32)
        taps = jnp.where(first_row, 0.0, jnp.concatenate([bwd, fwd], axis=0) * decay)
        taps = taps / jnp.sum(jnp.abs(taps), axis=0, keepdims=True)
        t = taps.T.astype(BF16).astype(F32)
        lo = pltpu.bitcast(t, jnp.uint32) >> 16
        hi = pltpu.bitcast(pltpu.roll(t, 1, axis=1), jnp.uint32) & jnp.uint32(0xFFFF0000)
        o_ref[order] = pltpu.bitcast(lo | hi, jnp.int32)


def _hyena_filters(lf, w1, b1, w2, b2, w3, freq):
    t = jnp.linspace(0.0, 1.0, lf, dtype=F32)[:, None]
    bands = jnp.linspace(1e-4, HY_BANDS - 1, HY_BANDS, dtype=F32)
    w = 2.0 * math.pi * jnp.arange(lf, dtype=F32)[:, None] / lf
    z = jnp.concatenate([t, jnp.cos(bands * w), jnp.sin(bands * w)], axis=-1)
    z = jnp.concatenate([z[:1], z[:0:-1], z], axis=0)
    z = jnp.pad(z, ((0, 0), (0, FEAT_PAD - z.shape[1])))
    w1p = jnp.pad(w1, ((0, FEAT_PAD - w1.shape[0]), (0, 0)))
    min_decay = math.log(HY_TARGET) / HY_SLOW_DECAY
    max_decay = math.log(HY_TARGET) / HY_FAST_DECAY
    deltas = jnp.abs(jnp.linspace(min_decay, max_decay, HY_W, dtype=F32))[None, :]
    full = lambda a: pl.BlockSpec(a.shape, lambda o: (0,) * a.ndim)
    ins = (z, w1p, b1[None, :], w2, b2[None, :], w3, freq, deltas)
    return pl.pallas_call(
        functools.partial(_filter_kernel, lf=lf),
        grid=(1,),
        in_specs=[full(a) for a in ins],
        out_specs=pl.BlockSpec((2, HY_W, 2 * lf), lambda o: (0, 0, 0)),
        out_shape=jax.ShapeDtypeStruct((2, HY_W, 2 * lf), jnp.int32),
        compiler_params=_cparams(("arbitrary",)),
        name="hyena_filter",
    )(*ins)


def _hyena_kernel(cw_ref, cb_ref, db_ref, v_ref, x1_ref, x2_ref, g_ref, taps_ref, shift_ref,
                  o_ref, *, cw, n_blk, batch):
    rows = n_blk * batch
    c0 = pl.program_id(0) * cw
    lane = lax.broadcasted_iota(jnp.int32, (1, TOEP), 1)
    zero_blk = jnp.zeros((batch, TOEP), F32)

    def short_conv(p_b, ch):
        shifted = jnp.dot(p_b, shift_ref[...], preferred_element_type=F32)
        p = p_b.astype(F32)
        prev = shifted[:, :TOEP]
        nxt = shifted[:, TOEP:]
        if n_blk > 1:
            prev_edge = jnp.concatenate([zero_blk, prev[:rows - batch]], axis=0)
            next_edge = jnp.concatenate([nxt[batch:], zero_blk], axis=0)
        else:
            prev_edge = jnp.zeros_like(p)
            next_edge = jnp.zeros_like(p)
        prev = jnp.where(lane == 0, prev_edge, prev)
        nxt = jnp.where(lane == TOEP - 1, next_edge, nxt)
        return cb_ref[ch] + cw_ref[0, ch] * prev + cw_ref[1, ch] * p + cw_ref[2, ch] * nxt

    def long_conv(z, order, ci):
        words = taps_ref[order, ci]
        skew = pltpu.roll(jnp.broadcast_to(words, (TOEP // 2, words.shape[1])), 0, axis=1,
                          stride=2, stride_axis=0)
        skew = pltpu.bitcast(skew, BF16)
        zb = z.astype(BF16)
        acc = [jnp.zeros((batch, TOEP), F32) for _ in range(n_blk)]
        for d in range(-(n_blk - 1), n_blk):
            n_out = n_blk - abs(d)
            src = max(0, -d) * batch
            col = (n_blk + d) * TOEP
            part = jnp.dot(zb[src:src + n_out * batch], skew[:, col:col + TOEP],
                           preferred_element_type=F32)
            for k in range(n_out):
                acc[max(0, d) + k] = acc[max(0, d) + k] + part[k * batch:(k + 1) * batch]
        return jnp.concatenate(acc, axis=0) if n_blk > 1 else acc[0]

    def body(k, carry):
        cis = [k * HY_UNROLL + u for u in range(HY_UNROLL)]
        zs = [short_conv(v_ref[ci], c0 + ci) for ci in cis]
        for order, x_ref in enumerate((x1_ref, x2_ref)):
            xs = [short_conv(x_ref[ci], (order + 1) * HY_W + c0 + ci) for ci in cis]
            ys = [long_conv(z, order, ci) for z, ci in zip(zs, cis)]
            zs = [x * (y + z * db_ref[order, c0 + ci]) for x, y, z, ci in zip(xs, ys, zs, cis)]
        for z, ci in zip(zs, cis):
            o_ref[ci] = (z * _silu(g_ref[ci].astype(F32))).astype(o_ref.dtype)
        return carry

    lax.fori_loop(0, cw // HY_UNROLL, body, 0)


def _hyena(hy_t, taps, conv_w, conv_b, d_bias, n_blk, batch, cw=HY_CW):
    rows = n_blk * batch
    smem = pl.BlockSpec(memory_space=pltpu.SMEM)
    slab = lambda off: pl.BlockSpec((cw, rows, TOEP), lambda c: (off // cw + c, 0, 0))
    taps4 = taps.reshape(2, HY_W, 1, taps.shape[-1])
    pos = jnp.arange(TOEP)
    shift = jnp.concatenate([pos[:, None] == (pos[None, :] - 1) % TOEP,
                             pos[:, None] == (pos[None, :] + 1) % TOEP], axis=1).astype(BF16)
    return pl.pallas_call(
        functools.partial(_hyena_kernel, cw=cw, n_blk=n_blk, batch=batch),
        grid=(HY_W // cw,),
        in_specs=[smem, smem, smem, slab(0), slab(HY_W), slab(2 * HY_W), slab(3 * HY_W),
                  pl.BlockSpec((2, cw, 1, taps.shape[-1]), lambda c: (0, c, 0, 0)),
                  pl.BlockSpec((TOEP, 2 * TOEP), lambda c: (0, 0))],
        out_specs=pl.BlockSpec((cw, rows, TOEP), lambda c: (c, 0, 0)),
        out_shape=jax.ShapeDtypeStruct((HY_W, rows, TOEP), BF16),
        compiler_params=_cparams(("arbitrary",)),
        name="hyena_mixer",
    )(conv_w, conv_b, d_bias, hy_t, hy_t, hy_t, hy_t, taps4, shift)


def _out_kernel(x_ref, gate_ref, a_ref, b_ref, c_ref, w_ref, o_ref):
    o_ref[0] = x_ref[0] + gate_ref[0] * _mix_out(a_ref[0], b_ref[0], c_ref[0], w_ref)


def _out_projection(x, gate, a, bmix, c, w_out, tm):
    b, t, d = x.shape
    per_batch = gate.shape[0] > 1
    tok = lambda w: pl.BlockSpec((1, tm, w), lambda i, j: (i, j, 0))
    return pl.pallas_call(
        _out_kernel,
        grid=(b, t // tm),
        in_specs=[tok(d),
                  pl.BlockSpec((1, 1, d), (lambda i, j: (i, 0, 0)) if per_batch
                               else (lambda i, j: (0, 0, 0))),
                  tok(Q_W), tok(HY_W), tok(Q_W),
                  pl.BlockSpec(w_out.shape, lambda i, j: (0, 0))],
        out_specs=tok(d),
        out_shape=jax.ShapeDtypeStruct((b, t, d), F32),
        compiler_params=_cparams(("parallel", "arbitrary")),
        name="out_projection",
    )(x, gate, a, bmix, c, w_out)


def _rope_tables(t_len):
    pos = jnp.arange(t_len)
    n_freq = ROPE_FREQS
    inv_freq = ROPE_THETA ** (-jnp.arange(n_freq, dtype=F32) / n_freq)
    ang = jnp.stack([(pos // GRID_W).astype(F32)[:, None] * inv_freq,
                     (pos % GRID_W).astype(F32)[:, None] * inv_freq], axis=1)
    cos = jnp.cos(ang)[:, :, None, :]
    sin = jnp.sin(ang)[:, :, None, :]
    cos = jnp.broadcast_to(cos, (t_len, 2, 2, n_freq)).reshape(t_len, HEAD_DIM)
    sin = jnp.concatenate([-sin, sin], axis=2).reshape(t_len, HEAD_DIM)
    return jnp.tile(cos, (1, 2)), jnp.tile(sin, (1, 2))


def _to_channel_major(hy, n_blk):
    b, t, c = hy.shape
    return hy.reshape(b, n_blk, TOEP, c).transpose(3, 1, 0, 2).reshape(c, n_blk * b, TOEP)


def _to_token_major(y, n_blk, batch):
    c = y.shape[0]
    return y.reshape(c, n_blk, batch, TOEP).transpose(2, 1, 3, 0).reshape(batch, n_blk * TOEP, c)


def kernel(x, c, ctx, c_ctx, norm_g, w_mod, b_mod, w_in, w_out, qn_a, kn_a, qn_c, kn_c, sink_c,
           hy_conv_w, hy_conv_b, hy_w1, hy_b1, hy_w2, hy_b2, hy_w3, hy_freq, hy_bias):
    depth = w_in.shape[0]
    batch, seq, _ = x.shape
    n_ctx = ctx.shape[1]
    n_blk = seq // TOEP
    assert x.shape[2] == D_MODEL and seq % (ATTN_TQ * ATTN_TILES) == 0 and seq % GRID_W == 0
    assert seq % TOEP == 0 and n_ctx == TOEP and seq >= ATTN_TQ + 2 * WINDOW
    assert batch % 16 == 0 and HY_W % HY_CW == 0 and HY_CW % HY_UNROLL == 0

    rows = -(-(batch + 1) // 8) * 8
    c_all = jnp.concatenate([c, c_ctx[None]], axis=0)
    c_all = jnp.pad(c_all, ((0, rows - batch - 1), (0, 0)))
    mod = _modulation(c_all, w_mod, b_mod)

    cos_l, sin_l = _rope_tables(seq)
    cos_c = jnp.ones((n_ctx, LANES), F32)
    sin_c = jnp.zeros((n_ctx, LANES), F32)
    blk = jnp.arange(TOEP) // HEAD_DIM
    bd = (blk[:, None] == blk[None, :]).astype(BF16)
    w_in_b = w_in.astype(BF16)
    w_out_b = w_out.astype(BF16)

    pending = None
    for l in range(depth):
        last = l == depth - 1
        mod_x = mod[l, :batch, None, :]
        mod_c = mod[l, batch:batch + 1, None, :]
        ng = norm_g[l][None, :]
        gq_a = jnp.concatenate([jnp.tile(qn_a[l], N_HEADS), jnp.tile(kn_a[l], N_KV)])[None, :]
        gq_c = jnp.concatenate([jnp.tile(qn_c[l], N_HEADS), jnp.tile(kn_c[l], N_KV)])[None, :]

        outs = _projection(x, mod_x, ng, w_in_b[l], cos_l, sin_l, bd, gq_a, gq_c,
                           tm=min(seq, PROJ_TM), prev=pending)
        if pending is not None:
            x, outs = outs[0], outs[1:]
        qa, ka, va, ga, hy, qc, kc, vc, gc = outs
        if last:
            c_off = 2 * D_MODEL + Q_W
            w_kv = jnp.concatenate([w_in_b[l][:, Q_W:Q_W + 2 * KV_W],
                                    w_in_b[l][:, c_off:c_off + 2 * KV_W]], axis=1)
            gk = jnp.concatenate([jnp.tile(kn_a[l], N_KV), jnp.tile(kn_c[l], N_KV)])[None, :]
            ka_c, va_c, kc_c, vc_c = _kv_projection(ctx, mod_c, ng, w_kv, bd, gk, tm=n_ctx)
        else:
            qa_c, ka_c, va_c, ga_c, hy_c, qc_c, kc_c, vc_c, gc_c = _projection(
                ctx, mod_c, ng, w_in_b[l], cos_c, sin_c, bd, gq_a, gq_c, tm=n_ctx)

        a_out = _attention(qa, ga, ka_c, va_c, ka, va, None, "full", ATTN_TQ, ATTN_TILES,
                           "attn_global")
        c_out = _attention(qc, gc, kc_c, vc_c, kc, vc, sink_c[l], "window", ATTN_TQ, ATTN_TILES,
                           "attn_window")
        taps = _hyena_filters(seq, hy_w1[l], hy_b1[l], hy_w2[l], hy_b2[l], hy_w3[l], hy_freq[l])
        b_out = _hyena(_to_channel_major(hy, n_blk), taps, hy_conv_w[l], hy_conv_b[l],
                       hy_bias[l], n_blk, batch)
        b_out = _to_token_major(b_out, n_blk, batch)
        pending = (mod_x[:, :, 2 * D_MODEL:], a_out, b_out, c_out, w_out_b[l])

        if not last:
            a_c = _attention(qa_c, ga_c, ka_c, va_c, None, None, None, None, n_ctx, 1,
                             "attn_ctx_a")
            c_c = _attention(qc_c, gc_c, kc_c, vc_c, None, None, sink_c[l], None, n_ctx, 1,
                             "attn_ctx_c")
            taps_c = _hyena_filters(n_ctx, hy_w1[l], hy_b1[l], hy_w2[l], hy_b2[l], hy_w3[l],
                                    hy_freq[l])
            b_c = _hyena(_to_channel_major(hy_c, 1), taps_c, hy_conv_w[l], hy_conv_b[l],
                         hy_bias[l], 1, batch)
            b_c = _to_token_major(b_c, 1, batch)
            ctx = _out_projection(ctx, mod_c[:, :, 2 * D_MODEL:], a_c, b_c, c_c, w_out_b[l], n_ctx)
    gate, a_out, b_out, c_out, w_o = pending
    return _out_projection(x, gate, a_out, b_out, c_out, w_o, min(seq, PROJ_TM))
```

```python
import functools
import math

import jax
import jax.numpy as jnp
from jax import lax
from jax.experimental import pallas as pl
from jax.experimental.pallas import tpu as pltpu

F32 = jnp.float32
BF16 = jnp.bfloat16

D_MODEL = 1024
HEAD_DIM = 64
N_HEADS = 6
N_KV = 2
Q_W = N_HEADS * HEAD_DIM
KV_W = N_KV * HEAD_DIM
HY_W = 256
GRID_W = 64
WINDOW = 128
ROPE_THETA = 10000.0
NORM_EPS = 1e-6
NEG_INF = -1e30
HY_BANDS = 16
HY_FAST_DECAY = 0.3
HY_SLOW_DECAY = 1.5
HY_TARGET = 1e-2
LANES = 128
TOEP = 256
FEAT_PAD = 128
ROPE_FREQS = HEAD_DIM // 4
ROPE_PAIR = 2 * ROPE_FREQS

PROJ_TM = 1024
PROJ_SUB = 256
ATTN_TQ = 256
ATTN_TILES = 8
KEY_CHUNK = 768
HY_CW = 8
HY_UNROLL = 4
VMEM_LIMIT = 56 * 1024 * 1024

_NT = (((1,), (1,)), ((), ()))
LOG2E = 1.4426950408889634
Q_SCALE = LOG2E / math.sqrt(HEAD_DIM)


def _cparams(sem):
    return pltpu.CompilerParams(dimension_semantics=sem, vmem_limit_bytes=VMEM_LIMIT)


def _silu(x):
    return x * (1.0 / (1.0 + jnp.exp(-x)))


def _mod_kernel(c_ref, w_ref, b_ref, o_ref):
    s = _silu(c_ref[...])
    o_ref[0] = jnp.dot(s, w_ref[0], precision=lax.Precision.HIGHEST,
                       preferred_element_type=F32) + b_ref[0]


def _modulation(c_all, w_mod, b_mod):
    depth, d, n = w_mod.shape
    rows = c_all.shape[0]
    nb = n // d
    return pl.pallas_call(
        _mod_kernel,
        grid=(depth, nb),
        in_specs=[pl.BlockSpec((rows, d), lambda l, j: (0, 0)),
                  pl.BlockSpec((1, d, d), lambda l, j: (l, 0, j)),
                  pl.BlockSpec((1, 1, d), lambda l, j: (l, 0, j))],
        out_specs=pl.BlockSpec((1, rows, d), lambda l, j: (l, 0, j)),
        out_shape=jax.ShapeDtypeStruct((depth, rows, n), F32),
        compiler_params=_cparams(("arbitrary", "arbitrary")),
        name="modulation",
    )(c_all, w_mod, b_mod.reshape(depth, 1, n))


def _head_norm_rope(qk, gain, cos, sin, bd):
    width = qk.shape[1]
    sq = (qk * qk).astype(BF16)
    ss = jnp.concatenate(
        [jnp.dot(sq[:, s:s + TOEP], bd, preferred_element_type=F32)
         for s in range(0, width, TOEP)], axis=1)
    y = qk * lax.rsqrt(ss * (1.0 / HEAD_DIM) + NORM_EPS) * gain
    lane = lax.broadcasted_iota(jnp.int32, (1, LANES), 1)
    first_half = (lane % ROPE_PAIR) < ROPE_FREQS
    out = []
    for s in range(0, width, LANES):
        ys = y[:, s:s + LANES]
        partner = jnp.where(first_half, pltpu.roll(ys, LANES - ROPE_FREQS, axis=1),
                            pltpu.roll(ys, ROPE_FREQS, axis=1))
        out.append(ys * cos + partner * sin)
    return jnp.concatenate(out, axis=1)


def _with_swapped(t):
    return jnp.concatenate([t, pltpu.roll(t, HEAD_DIM, axis=1)], axis=1).astype(BF16)


def _mix_out(a, b, c, w_ref):
    acc = jnp.dot(a, w_ref[0:Q_W], preferred_element_type=F32)
    acc = acc + jnp.dot(b, w_ref[Q_W:Q_W + HY_W], preferred_element_type=F32)
    return acc + jnp.dot(c, w_ref[Q_W + HY_W:], preferred_element_type=F32)


def _proj_kernel(*refs, prev_out):
    if prev_out:
        gate_ref, a_ref, b_ref, c_ref, wo_ref = refs[:5]
        refs = refs[5:]
    (x_ref, mod_ref, ng_ref, w_ref, cos_ref, sin_ref, bd_ref, gq_a_ref, gq_c_ref) = refs[:9]
    outs = refs[9:]
    if prev_out:
        xo_ref, outs = outs[0], outs[1:]
    qa_ref, ka_ref, va_ref, ga_ref, hy_ref, qc_ref, kc_ref, vc_ref, gc_ref = outs
    shift = mod_ref[0, :, 0:D_MODEL]
    scale = mod_ref[0, :, D_MODEL:2 * D_MODEL]
    bd = bd_ref[...]
    tm = x_ref.shape[1]
    sub = min(tm, PROJ_SUB)
    for r0 in range(0, tm, sub):
        rows = slice(r0, r0 + sub)
        x = x_ref[0, rows]
        if prev_out:
            x = x + gate_ref[0] * _mix_out(a_ref[0, rows], b_ref[0, rows], c_ref[0, rows], wo_ref)
            xo_ref[0, rows] = x
        ms = jnp.mean(x * x, axis=-1, keepdims=True)
        y = x * lax.rsqrt(ms + NORM_EPS) * ng_ref[...]
        h = (y * (1.0 + scale) + shift).astype(BF16)
        cos = cos_ref[rows]
        sin = sin_ref[rows]

        def attn_branch(col0, gain_ref, q_ref, k_ref, v_ref, g_ref):
            acc = jnp.dot(h, w_ref[:, col0:col0 + D_MODEL], preferred_element_type=F32)
            qk = _head_norm_rope(acc[:, 0:Q_W + KV_W], gain_ref[...], cos, sin, bd)
            q_ref[0, rows] = (qk[:, 0:Q_W] * Q_SCALE).astype(BF16)
            k_ref[0, rows] = _with_swapped(qk[:, Q_W:Q_W + KV_W])
            v_ref[0, rows] = _with_swapped(acc[:, Q_W + KV_W:Q_W + 2 * KV_W])
            g_ref[0, rows] = _silu(acc[:, Q_W + 2 * KV_W:]).astype(BF16)

        attn_branch(0, gq_a_ref, qa_ref, ka_ref, va_ref, ga_ref)
        hy_ref[0, rows] = jnp.dot(h, w_ref[:, D_MODEL:2 * D_MODEL],
                                  preferred_element_type=F32).astype(BF16)
        attn_branch(2 * D_MODEL, gq_c_ref, qc_ref, kc_ref, vc_ref, gc_ref)


def _projection(x, mod, norm_g, w_in, cos, sin, bd, gq_a, gq_c, tm, prev=None):
    b, t, d = x.shape
    tok = lambda w: pl.BlockSpec((1, tm, w), lambda i, j: (i, j, 0))
    once = lambda a: pl.BlockSpec(a.shape, lambda i, j: (0,) * a.ndim,
                                  pipeline_mode=pl.Buffered(1))
    per_batch = lambda a: pl.BlockSpec((1, 1, a.shape[2]), (lambda i, j: (i, 0, 0))
                                       if a.shape[0] > 1 else (lambda i, j: (0, 0, 0)))
    widths = (Q_W, 2 * LANES, 2 * LANES, Q_W, D_MODEL, Q_W, 2 * LANES, 2 * LANES, Q_W)
    args = [x, mod, norm_g, w_in, cos, sin, bd, gq_a, gq_c]
    in_specs = [tok(d), per_batch(mod), once(norm_g), once(w_in),
                pl.BlockSpec((tm, LANES), lambda i, j: (j, 0)),
                pl.BlockSpec((tm, LANES), lambda i, j: (j, 0)),
                once(bd), once(gq_a), once(gq_c)]
    out_specs = [tok(w) for w in widths]
    out_shape = [jax.ShapeDtypeStruct((b, t, w), BF16) for w in widths]
    if prev is not None:
        gate, a, bmix, c, w_out = prev
        args = [gate, a, bmix, c, w_out] + args
        in_specs = [per_batch(gate), tok(Q_W), tok(HY_W), tok(Q_W), once(w_out)] + in_specs
        out_specs = [tok(d)] + out_specs
        out_shape = [jax.ShapeDtypeStruct((b, t, d), F32)] + out_shape
    return pl.pallas_call(
        functools.partial(_proj_kernel, prev_out=prev is not None),
        grid=(b, t // tm),
        in_specs=in_specs,
        out_specs=out_specs,
        out_shape=out_shape,
        compiler_params=_cparams(("parallel", "arbitrary")),
        name="in_projection",
    )(*args)


def _proj_kv_kernel(x_ref, mod_ref, ng_ref, w_ref, bd_ref, gk_ref, ka_ref, va_ref, kc_ref, vc_ref):
    x = x_ref[0]
    ms = jnp.mean(x * x, axis=-1, keepdims=True)
    y = x * lax.rsqrt(ms + NORM_EPS) * ng_ref[...]
    h = (y * (1.0 + mod_ref[0, :, D_MODEL:2 * D_MODEL]) + mod_ref[0, :, 0:D_MODEL]).astype(BF16)
    acc = jnp.dot(h, w_ref[...], preferred_element_type=F32)
    k = jnp.concatenate([acc[:, 0:KV_W], acc[:, 2 * KV_W:3 * KV_W]], axis=1)
    ss = jnp.dot((k * k).astype(BF16), bd_ref[...], preferred_element_type=F32)
    k = k * lax.rsqrt(ss * (1.0 / HEAD_DIM) + NORM_EPS) * gk_ref[...]
    ka_ref[0] = _with_swapped(k[:, 0:KV_W])
    kc_ref[0] = _with_swapped(k[:, KV_W:])
    va_ref[0] = _with_swapped(acc[:, KV_W:2 * KV_W])
    vc_ref[0] = _with_swapped(acc[:, 3 * KV_W:])


def _kv_projection(x, mod, norm_g, w_kv, bd, gk, tm):
    b, t, d = x.shape
    tok = lambda w: pl.BlockSpec((1, tm, w), lambda i, j: (i, j, 0))
    full = lambda a: pl.BlockSpec(a.shape, lambda i, j: (0,) * a.ndim)
    return pl.pallas_call(
        _proj_kv_kernel,
        grid=(b, t // tm),
        in_specs=[tok(d), full(mod), full(norm_g), full(w_kv), full(bd), full(gk)],
        out_specs=[tok(2 * LANES)] * 4,
        out_shape=[jax.ShapeDtypeStruct((b, t, 2 * LANES), BF16)] * 4,
        compiler_params=_cparams(("parallel", "arbitrary")),
        name="kv_projection",
    )(x, mod, norm_g, w_kv, bd, gk)


def _attn_kernel(*refs, tq, n_sub, lat_mode, has_sink, n_ctx, lat_len):
    refs = list(refs)
    sink_ref = refs.pop(0) if has_sink else None
    q_ref, g_ref, kc_ref, vc_ref = refs[:4]
    refs = refs[4:]
    if lat_mode is not None:
        kl_ref, vl_ref = refs[:2]
        refs = refs[2:]
    o_ref, kpad_s, vaug_s = refs
    i = pl.program_id(1)

    @pl.when(i == 0)
    def _stage_keys():
        lo = lax.broadcasted_iota(jnp.int32, (1, LANES), 1) < HEAD_DIM

        def fill(row0, n, k_ref, v_ref):
            rows = slice(row0, row0 + n)
            for dst, src, width in ((kpad_s, k_ref, LANES), (vaug_s, v_ref, 2 * LANES)):
                t = src[0, :, 0:LANES]
                sw = src[0, :, LANES:2 * LANES]
                zero = jnp.zeros_like(t)
                for blk, val in enumerate((jnp.where(lo, t, zero), jnp.where(lo, zero, sw),
                                           jnp.where(lo, sw, zero), jnp.where(lo, zero, t))):
                    dst[rows, blk * width:blk * width + LANES] = val

        fill(0, n_ctx, kc_ref, vc_ref)
        if lat_mode is not None:
            fill(n_ctx, lat_len, kl_ref, vl_ref)

    @pl.when((i == 0) & (pl.program_id(0) == 0))
    def _stage_ones():
        for blk in range(2 * N_KV):
            vaug_s[:, (2 * blk + 1) * LANES:(2 * blk + 2) * LANES] = jnp.ones(
                (vaug_s.shape[0], LANES), BF16)

    n_keys = n_ctx + lat_len
    for sub, pair in [(s_, p_) for s_ in range(n_sub) for p_ in range(N_HEADS // 2)]:
        q_rows = slice(sub * tq, (sub + 1) * tq)
        if lat_mode == "window":
            if pair == 0:
                span = tq + 2 * WINDOW
                q0 = (i * n_sub + sub) * tq
                start = jnp.clip(q0 - WINDOW, 0, lat_len - span)
                qpos = q0 + lax.broadcasted_iota(jnp.int32, (tq, span), 0)
                kpos = start + lax.broadcasted_iota(jnp.int32, (tq, span), 1)
                valid = jnp.abs(qpos - kpos) <= WINDOW
                band = pl.ds(pl.multiple_of(n_ctx + start, WINDOW), span)
            key_rows = [(slice(0, n_ctx), None), (band, valid)]
        else:
            key_rows = [(slice(r, min(r + KEY_CHUNK, n_keys)), None)
                        for r in range(0, n_keys, KEY_CHUNK)]
        lanes = slice(pair * LANES, (pair + 1) * LANES)
        qp = q_ref[0, q_rows, lanes]
        acc = jnp.zeros((tq, LANES), F32)
        for parity in range(2):
            head = 2 * pair + parity
            blk = 2 * (head // (N_HEADS // N_KV)) + parity
            kcols = slice(blk * LANES, (blk + 1) * LANES)
            vcols = slice(2 * blk * LANES, 2 * (blk + 1) * LANES)
            scores = []
            for rows, mask in key_rows:
                s = lax.dot_general(qp, kpad_s[rows, kcols], _NT, preferred_element_type=F32)
                scores.append(s if mask is None else jnp.where(mask, s, NEG_INF))
            m = functools.reduce(jnp.maximum,
                                 [jnp.max(s, axis=-1, keepdims=True) for s in scores])
            if has_sink:
                sink = sink_ref[head] * LOG2E
                m = jnp.maximum(m, sink)
            o = functools.reduce(jnp.add, [
                jnp.dot(jnp.exp2(s - m).astype(BF16), vaug_s[rows, vcols],
                        preferred_element_type=F32)
                for s, (rows, _) in zip(scores, key_rows)])
            den = o[:, LANES:]
            if has_sink:
                den = den + jnp.exp2(sink - m)
            acc = acc + o[:, :LANES] / den
        o_ref[0, q_rows, lanes] = (acc * g_ref[0, q_rows, lanes].astype(F32)).astype(BF16)


def _attention(q, g, k_ctx, v_ctx, k_lat, v_lat, sink, lat_mode, tq, n_sub, name):
    b, t, _ = q.shape
    n_ctx = k_ctx.shape[1]
    has_sink = sink is not None
    tok = pl.BlockSpec((1, n_sub * tq, Q_W), lambda i, j: (i, j, 0))
    whole = lambda n: pl.BlockSpec((1, n, 2 * LANES), lambda i, j: (i, 0, 0))
    args, specs = [], []
    if has_sink:
        args.append(sink)
        specs.append(pl.BlockSpec(memory_space=pltpu.SMEM))
    args += [q, g, k_ctx, v_ctx]
    specs += [tok, tok, whole(n_ctx), whole(n_ctx)]
    lat_len = 0
    if lat_mode is not None:
        lat_len = k_lat.shape[1]
        args += [k_lat, v_lat]
        specs += [whole(lat_len), whole(lat_len)]
    n_keys = n_ctx + lat_len
    return pl.pallas_call(
        functools.partial(_attn_kernel, tq=tq, n_sub=n_sub, lat_mode=lat_mode,
                          has_sink=has_sink, n_ctx=n_ctx, lat_len=lat_len),
        grid=(b, t // (n_sub * tq)),
        in_specs=specs,
        out_specs=tok,
        out_shape=jax.ShapeDtypeStruct((b, t, Q_W), BF16),
        scratch_shapes=[pltpu.VMEM((n_keys, 2 * N_KV * LANES), BF16),
                        pltpu.VMEM((n_keys, 4 * N_KV * LANES), BF16)],
        compiler_params=_cparams(("arbitrary", "arbitrary")),
        name=name,
    )(*args)


def _filter_kernel(z_ref, w1_ref, b1_ref, w2_ref, b2_ref, w3_ref, fr_ref, dl_ref, o_ref, *, lf):
    hp = lax.Precision.HIGHEST
    z = z_ref[...]
    h = jnp.sin(fr_ref[0:1, :] * (jnp.dot(z, w1_ref[...], precision=hp,
                                          preferred_element_type=F32) + b1_ref[...]))
    h = jnp.sin(fr_ref[1:2, :] * (jnp.dot(h, w2_ref[...], precision=hp,
                                          preferred_element_type=F32) + b2_ref[...]))
    decay = jnp.exp(-z[:, 0:1] * dl_ref[...])
    first_row = lax.broadcasted_iota(jnp.int32, decay.shape, 0) == 0
    for order in range(2):
        c0 = 2 * order * HY_W
        bwd = jnp.dot(h[:lf], w3_ref[:, c0 + HY_W:c0 + 2 * HY_W], precision=hp,
                      preferred_element_type=F32)
        fwd = jnp.dot(h[lf:], w3_ref[:, c0:c0 + HY_W], precision=hp,
                      preferred_element_type=F32)
        taps = jnp.where(first_row, 0.0, jnp.concatenate([bwd, fwd], axis=0) * decay)
        taps = taps / jnp.sum(jnp.abs(taps), axis=0, keepdims=True)
        t = taps.T.astype(BF16).astype(F32)
        lo = pltpu.bitcast(t, jnp.uint32) >> 16
        hi = pltpu.bitcast(pltpu.roll(t, 1, axis=1), jnp.uint32) & jnp.uint32(0xFFFF0000)
        o_ref[order] = pltpu.bitcast(lo | hi, jnp.int32)


def _hyena_filters(lf, w1, b1, w2, b2, w3, freq):
    t = jnp.linspace(0.0, 1.0, lf, dtype=F32)[:, None]
    bands = jnp.linspace(1e-4, HY_BANDS - 1, HY_BANDS, dtype=F32)
    w = 2.0 * math.pi * jnp.arange(lf, dtype=F32)[:, None] / lf
    z = jnp.concatenate([t, jnp.cos(bands * w), jnp.sin(bands * w)], axis=-1)
    z = jnp.concatenate([z[:1], z[:0:-1], z], axis=0)
    z = jnp.pad(z, ((0, 0), (0, FEAT_PAD - z.shape[1])))
    w1p = jnp.pad(w1, ((0, FEAT_PAD - w1.shape[0]), (0, 0)))
    min_decay = math.log(HY_TARGET) / HY_SLOW_DECAY
    max_decay = math.log(HY_TARGET) / HY_FAST_DECAY
    deltas = jnp.abs(jnp.linspace(min_decay, max_decay, HY_W, dtype=F32))[None, :]
    full = lambda a: pl.BlockSpec(a.shape, lambda o: (0,) * a.ndim)
    ins = (z, w1p, b1[None, :], w2, b2[None, :], w3, freq, deltas)
    return pl.pallas_call(
        functools.partial(_filter_kernel, lf=lf),
        grid=(1,),
        in_specs=[full(a) for a in ins],
        out_specs=pl.BlockSpec((2, HY_W, 2 * lf), lambda o: (0, 0, 0)),
        out_shape=jax.ShapeDtypeStruct((2, HY_W, 2 * lf), jnp.int32),
        compiler_params=_cparams(("arbitrary",)),
        name="hyena_filter",
    )(*ins)


def _hyena_kernel(cw_ref, cb_ref, db_ref, v_ref, x1_ref, x2_ref, g_ref, taps_ref, shift_ref,
                  o_ref, *, cw, n_blk, batch):
    rows = n_blk * batch
    c0 = pl.program_id(0) * cw
    lane = lax.broadcasted_iota(jnp.int32, (1, TOEP), 1)
    zero_blk = jnp.zeros((batch, TOEP), F32)

    def short_conv(p_b, ch):
        shifted = jnp.dot(p_b, shift_ref[...], preferred_element_type=F32)
        p = p_b.astype(F32)
        prev = shifted[:, :TOEP]
        nxt = shifted[:, TOEP:]
        if n_blk > 1:
            prev_edge = jnp.concatenate([zero_blk, prev[:rows - batch]], axis=0)
            next_edge = jnp.concatenate([nxt[batch:], zero_blk], axis=0)
        else:
            prev_edge = jnp.zeros_like(p)
            next_edge = jnp.zeros_like(p)
        prev = jnp.where(lane == 0, prev_edge, prev)
        nxt = jnp.where(lane == TOEP - 1, next_edge, nxt)
        return cb_ref[ch] + cw_ref[0, ch] * prev + cw_ref[1, ch] * p + cw_ref[2, ch] * nxt

    def long_conv(z, order, ci):
        words = taps_ref[order, ci]
        skew = pltpu.roll(jnp.broadcast_to(words, (TOEP // 2, words.shape[1])), 0, axis=1,
                          stride=2, stride_axis=0)
        skew = pltpu.bitcast(skew, BF16)
        zb = z.astype(BF16)
        acc = [jnp.zeros((batch, TOEP), F32) for _ in range(n_blk)]
        for d in range(-(n_blk - 1), n_blk):
            n_out = n_blk - abs(d)
            src = max(0, -d) * batch
            col = (n_blk + d) * TOEP
            part = jnp.dot(zb[src:src + n_out * batch], skew[:, col:col + TOEP],
                           preferred_element_type=F32)
            for k in range(n_out):
                acc[max(0, d) + k] = acc[max(0, d) + k] + part[k * batch:(k + 1) * batch]
        return jnp.concatenate(acc, axis=0) if n_blk > 1 else acc[0]

    def body(k, carry):
        cis = [k * HY_UNROLL + u for u in range(HY_UNROLL)]
        zs = [short_conv(v_ref[ci], c0 + ci) for ci in cis]
        for order, x_ref in enumerate((x1_ref, x2_ref)):
            xs = [short_conv(x_ref[ci], (order + 1) * HY_W + c0 + ci) for ci in cis]
            ys = [long_conv(z, order, ci) for z, ci in zip(zs, cis)]
            zs = [x * (y + z * db_ref[order, c0 + ci]) for x, y, z, ci in zip(xs, ys, zs, cis)]
        for z, ci in zip(zs, cis):
            o_ref[ci] = (z * _silu(g_ref[ci].astype(F32))).astype(o_ref.dtype)
        return carry

    lax.fori_loop(0, cw // HY_UNROLL, body, 0)


def _hyena(hy_t, taps, conv_w, conv_b, d_bias, n_blk, batch, cw=HY_CW):
    rows = n_blk * batch
    smem = pl.BlockSpec(memory_space=pltpu.SMEM)
    slab = lambda off: pl.BlockSpec((cw, rows, TOEP), lambda c: (off // cw + c, 0, 0))
    taps4 = taps.reshape(2, HY_W, 1, taps.shape[-1])
    pos = jnp.arange(TOEP)
    shift = jnp.concatenate([pos[:, None] == (pos[None, :] - 1) % TOEP,
                             pos[:, None] == (pos[None, :] + 1) % TOEP], axis=1).astype(BF16)
    return pl.pallas_call(
        functools.partial(_hyena_kernel, cw=cw, n_blk=n_blk, batch=batch),
        grid=(HY_W // cw,),
        in_specs=[smem, smem, smem, slab(0), slab(HY_W), slab(2 * HY_W), slab(3 * HY_W),
                  pl.BlockSpec((2, cw, 1, taps.shape[-1]), lambda c: (0, c, 0, 0)),
                  pl.BlockSpec((TOEP, 2 * TOEP), lambda c: (0, 0))],
        out_specs=pl.BlockSpec((cw, rows, TOEP), lambda c: (c, 0, 0)),
        out_shape=jax.ShapeDtypeStruct((HY_W, rows, TOEP), BF16),
        compiler_params=_cparams(("arbitrary",)),
        name="hyena_mixer",
    )(conv_w, conv_b, d_bias, hy_t, hy_t, hy_t, hy_t, taps4, shift)


def _out_kernel(x_ref, gate_ref, a_ref, b_ref, c_ref, w_ref, o_ref):
    o_ref[0] = x_ref[0] + gate_ref[0] * _mix_out(a_ref[0], b_ref[0], c_ref[0], w_ref)


def _out_projection(x, gate, a, bmix, c, w_out, tm):
    b, t, d = x.shape
    per_batch = gate.shape[0] > 1
    tok = lambda w: pl.BlockSpec((1, tm, w), lambda i, j: (i, j, 0))
    return pl.pallas_call(
        _out_kernel,
        grid=(b, t // tm),
        in_specs=[tok(d),
                  pl.BlockSpec((1, 1, d), (lambda i, j: (i, 0, 0)) if per_batch
                               else (lambda i, j: (0, 0, 0))),
                  tok(Q_W), tok(HY_W), tok(Q_W),
                  pl.BlockSpec(w_out.shape, lambda i, j: (0, 0))],
        out_specs=tok(d),
        out_shape=jax.ShapeDtypeStruct((b, t, d), F32),
        compiler_params=_cparams(("parallel", "arbitrary")),
        name="out_projection",
    )(x, gate, a, bmix, c, w_out)


def _rope_tables(t_len):
    pos = jnp.arange(t_len)
    n_freq = ROPE_FREQS
    inv_freq = ROPE_THETA ** (-jnp.arange(n_freq, dtype=F32) / n_freq)
    ang = jnp.stack([(pos // GRID_W).astype(F32)[:, None] * inv_freq,
                     (pos % GRID_W).astype(F32)[:, None] * inv_freq], axis=1)
    cos = jnp.cos(ang)[:, :, None, :]
    sin = jnp.sin(ang)[:, :, None, :]
    cos = jnp.broadcast_to(cos, (t_len, 2, 2, n_freq)).reshape(t_len, HEAD_DIM)
    sin = jnp.concatenate([-sin, sin], axis=2).reshape(t_len, HEAD_DIM)
    return jnp.tile(cos, (1, 2)), jnp.tile(sin, (1, 2))


def _to_channel_major(hy, n_blk):
    b, t, c = hy.shape
    return hy.reshape(b, n_blk, TOEP, c).transpose(3, 1, 0, 2).reshape(c, n_blk * b, TOEP)


def _to_token_major(y, n_blk, batch):
    c = y.shape[0]
    return y.reshape(c, n_blk, batch, TOEP).transpose(2, 1, 3, 0).reshape(batch, n_blk * TOEP, c)


def kernel(x, c, ctx, c_ctx, norm_g, w_mod, b_mod, w_in, w_out, qn_a, kn_a, qn_c, kn_c, sink_c,
           hy_conv_w, hy_conv_b, hy_w1, hy_b1, hy_w2, hy_b2, hy_w3, hy_freq, hy_bias):
    depth = w_in.shape[0]
    batch, seq, _ = x.shape
    n_ctx = ctx.shape[1]
    n_blk = seq // TOEP
    assert x.shape[2] == D_MODEL and seq % (ATTN_TQ * ATTN_TILES) == 0 and seq % GRID_W == 0
    assert seq % TOEP == 0 and n_ctx == TOEP and seq >= ATTN_TQ + 2 * WINDOW
    assert batch % 16 == 0 and HY_W % HY_CW == 0 and HY_CW % HY_UNROLL == 0

    rows = -(-(batch + 1) // 8) * 8
    c_all = jnp.concatenate([c, c_ctx[None]], axis=0)
    c_all = jnp.pad(c_all, ((0, rows - batch - 1), (0, 0)))
    mod = _modulation(c_all, w_mod, b_mod)

    cos_l, sin_l = _rope_tables(seq)
    cos_c = jnp.ones((n_ctx, LANES), F32)
    sin_c = jnp.zeros((n_ctx, LANES), F32)
    blk = jnp.arange(TOEP) // HEAD_DIM
    bd = (blk[:, None] == blk[None, :]).astype(BF16)
    w_in_b = w_in.astype(BF16)
    w_out_b = w_out.astype(BF16)

    pending = None
    for l in range(depth):
        last = l == depth - 1
        mod_x = mod[l, :batch, None, :]
        mod_c = mod[l, batch:batch + 1, None, :]
        ng = norm_g[l][None, :]
        gq_a = jnp.concatenate([jnp.tile(qn_a[l], N_HEADS), jnp.tile(kn_a[l], N_KV)])[None, :]
        gq_c = jnp.concatenate([jnp.tile(qn_c[l], N_HEADS), jnp.tile(kn_c[l], N_KV)])[None, :]

        outs = _projection(x, mod_x, ng, w_in_b[l], cos_l, sin_l, bd, gq_a, gq_c,
                           tm=min(seq, PROJ_TM), prev=pending)
        if pending is not None:
            x, outs = outs[0], outs[1:]
        qa, ka, va, ga, hy, qc, kc, vc, gc = outs
        if last:
            c_off = 2 * D_MODEL + Q_W
            w_kv = jnp.concatenate([w_in_b[l][:, Q_W:Q_W + 2 * KV_W],
                                    w_in_b[l][:, c_off:c_off + 2 * KV_W]], axis=1)
            gk = jnp.concatenate([jnp.tile(kn_a[l], N_KV), jnp.tile(kn_c[l], N_KV)])[None, :]
            ka_c, va_c, kc_c, vc_c = _kv_projection(ctx, mod_c, ng, w_kv, bd, gk, tm=n_ctx)
        else:
            qa_c, ka_c, va_c, ga_c, hy_c, qc_c, kc_c, vc_c, gc_c = _projection(
                ctx, mod_c, ng, w_in_b[l], cos_c, sin_c, bd, gq_a, gq_c, tm=n_ctx)

        a_out = _attention(qa, ga, ka_c, va_c, ka, va, None, "full", 2 * ATTN_TQ, ATTN_TILES // 2,
                           "attn_global")
        c_out = _attention(qc, gc, kc_c, vc_c, kc, vc, sink_c[l], "window", ATTN_TQ, ATTN_TILES,
                           "attn_window")
        taps = _hyena_filters(seq, hy_w1[l], hy_b1[l], hy_w2[l], hy_b2[l], hy_w3[l], hy_freq[l])
        b_out = _hyena(_to_channel_major(hy, n_blk), taps, hy_conv_w[l], hy_conv_b[l],
                       hy_bias[l], n_blk, batch)
        b_out = _to_token_major(b_out, n_blk, batch)
        pending = (mod_x[:, :, 2 * D_MODEL:], a_out, b_out, c_out, w_out_b[l])

        if not last:
            a_c = _attention(qa_c, ga_c, ka_c, va_c, None, None, None, None, n_ctx, 1,
                             "attn_ctx_a")
            c_c = _attention(qc_c, gc_c, kc_c, vc_c, None, None, sink_c[l], None, n_ctx, 1,
                             "attn_ctx_c")
            taps_c = _hyena_filters(n_ctx, hy_w1[l], hy_b1[l], hy_w2[l], hy_b2[l], hy_w3[l],
                                    hy_freq[l])
            b_c = _hyena(_to_channel_major(hy_c, 1), taps_c, hy_conv_w[l], hy_conv_b[l],
                         hy_bias[l], 1, batch)
            b_c = _to_token_major(b_c, 1, batch)
            ctx = _out_projection(ctx, mod_c[:, :, 2 * D_MODEL:], a_c, b_c, c_c, w_out_b[l], n_ctx)
    gate, a_out, b_out, c_out, w_o = pending
    return _out_projection(x, gate, a_out, b_out, c_out, w_o, min(seq, PROJ_TM))
```

```python
import functools
import math

import jax
import jax.numpy as jnp
from jax import lax
from jax.experimental import pallas as pl
from jax.experimental.pallas import tpu as pltpu

F32 = jnp.float32
BF16 = jnp.bfloat16

D_MODEL = 1024
HEAD_DIM = 64
N_HEADS = 6
N_KV = 2
Q_W = N_HEADS * HEAD_DIM
KV_W = N_KV * HEAD_DIM
HY_W = 256
GRID_W = 64
WINDOW = 128
ROPE_THETA = 10000.0
NORM_EPS = 1e-6
NEG_INF = -1e30
HY_BANDS = 16
HY_FAST_DECAY = 0.3
HY_SLOW_DECAY = 1.5
HY_TARGET = 1e-2
LANES = 128
TOEP = 256
FEAT_PAD = 128
ROPE_FREQS = HEAD_DIM // 4
ROPE_PAIR = 2 * ROPE_FREQS

PROJ_TM = 1024
PROJ_SUB = 256
ATTN_TQ = 256
ATTN_TILES = 8
KEY_CHUNK = 768
HY_CW = 8
HY_UNROLL = 4
VMEM_LIMIT = 56 * 1024 * 1024

_NT = (((1,), (1,)), ((), ()))
LOG2E = 1.4426950408889634
Q_SCALE = LOG2E / math.sqrt(HEAD_DIM)


def _cparams(sem):
    return pltpu.CompilerParams(dimension_semantics=sem, vmem_limit_bytes=VMEM_LIMIT)


def _silu(x):
    return x * (1.0 / (1.0 + jnp.exp(-x)))


def _mod_kernel(c_ref, w_ref, b_ref, o_ref):
    s = _silu(c_ref[...])
    o_ref[0] = jnp.dot(s, w_ref[0], precision=lax.Precision.HIGHEST,
                       preferred_element_type=F32) + b_ref[0]


def _modulation(c_all, w_mod, b_mod):
    depth, d, n = w_mod.shape
    rows = c_all.shape[0]
    nb = n // d
    return pl.pallas_call(
        _mod_kernel,
        grid=(depth, nb),
        in_specs=[pl.BlockSpec((rows, d), lambda l, j: (0, 0)),
                  pl.BlockSpec((1, d, d), lambda l, j: (l, 0, j)),
                  pl.BlockSpec((1, 1, d), lambda l, j: (l, 0, j))],
        out_specs=pl.BlockSpec((1, rows, d), lambda l, j: (l, 0, j)),
        out_shape=jax.ShapeDtypeStruct((depth, rows, n), F32),
        compiler_params=_cparams(("arbitrary", "arbitrary")),
        name="modulation",
    )(c_all, w_mod, b_mod.reshape(depth, 1, n))


def _head_sum_squares(x):
    lane = lax.broadcasted_iota(jnp.int32, (1, LANES), 1)
    low = lane < HEAD_DIM
    out = []
    for s in range(0, x.shape[1], LANES):
        sq = x[:, s:s + LANES] * x[:, s:s + LANES]
        s_low = jnp.sum(jnp.where(low, sq, 0.0), axis=-1, keepdims=True)
        s_high = jnp.sum(jnp.where(low, 0.0, sq), axis=-1, keepdims=True)
        out.append(jnp.where(low, s_low, s_high))
    return jnp.concatenate(out, axis=1)


def _head_norm_rope(qk, gain, cos, sin):
    width = qk.shape[1]
    y = qk * lax.rsqrt(_head_sum_squares(qk) * (1.0 / HEAD_DIM) + NORM_EPS) * gain
    lane = lax.broadcasted_iota(jnp.int32, (1, LANES), 1)
    first_half = (lane % ROPE_PAIR) < ROPE_FREQS
    out = []
    for s in range(0, width, LANES):
        ys = y[:, s:s + LANES]
        partner = jnp.where(first_half, pltpu.roll(ys, LANES - ROPE_FREQS, axis=1),
                            pltpu.roll(ys, ROPE_FREQS, axis=1))
        out.append(ys * cos + partner * sin)
    return jnp.concatenate(out, axis=1)


def _with_swapped(t):
    return jnp.concatenate([t, pltpu.roll(t, HEAD_DIM, axis=1)], axis=1).astype(BF16)


def _mix_out(a, b, c, w_ref):
    acc = jnp.dot(a, w_ref[0:Q_W], preferred_element_type=F32)
    acc = acc + jnp.dot(b, w_ref[Q_W:Q_W + HY_W], preferred_element_type=F32)
    return acc + jnp.dot(c, w_ref[Q_W + HY_W:], preferred_element_type=F32)


def _proj_kernel(*refs, prev_out):
    if prev_out:
        gate_ref, a_ref, b_ref, c_ref, wo_ref = refs[:5]
        refs = refs[5:]
    (x_ref, mod_ref, ng_ref, w_ref, cos_ref, sin_ref, gq_a_ref, gq_c_ref) = refs[:8]
    outs = refs[8:]
    if prev_out:
        xo_ref, outs = outs[0], outs[1:]
    qa_ref, ka_ref, va_ref, ga_ref, hy_ref, qc_ref, kc_ref, vc_ref, gc_ref = outs
    shift = mod_ref[0, :, 0:D_MODEL]
    scale = mod_ref[0, :, D_MODEL:2 * D_MODEL]
    tm = x_ref.shape[1]
    sub = min(tm, PROJ_SUB)
    for r0 in range(0, tm, sub):
        rows = slice(r0, r0 + sub)
        x = x_ref[0, rows]
        if prev_out:
            x = x + gate_ref[0] * _mix_out(a_ref[0, rows], b_ref[0, rows], c_ref[0, rows], wo_ref)
            xo_ref[0, rows] = x
        ms = jnp.mean(x * x, axis=-1, keepdims=True)
        y = x * lax.rsqrt(ms + NORM_EPS) * ng_ref[...]
        h = (y * (1.0 + scale) + shift).astype(BF16)
        cos = cos_ref[rows]
        sin = sin_ref[rows]

        def attn_branch(col0, gain_ref, q_ref, k_ref, v_ref, g_ref):
            acc = jnp.dot(h, w_ref[:, col0:col0 + D_MODEL], preferred_element_type=F32)
            qk = _head_norm_rope(acc[:, 0:Q_W + KV_W], gain_ref[...], cos, sin)
            q_ref[0, rows] = (qk[:, 0:Q_W] * Q_SCALE).astype(BF16)
            k_ref[0, rows] = _with_swapped(qk[:, Q_W:Q_W + KV_W])
            v_ref[0, rows] = _with_swapped(acc[:, Q_W + KV_W:Q_W + 2 * KV_W])
            g_ref[0, rows] = _silu(acc[:, Q_W + 2 * KV_W:]).astype(BF16)

        attn_branch(0, gq_a_ref, qa_ref, ka_ref, va_ref, ga_ref)
        hy_ref[0, rows] = jnp.dot(h, w_ref[:, D_MODEL:2 * D_MODEL],
                                  preferred_element_type=F32).astype(BF16)
        attn_branch(2 * D_MODEL, gq_c_ref, qc_ref, kc_ref, vc_ref, gc_ref)


def _projection(x, mod, norm_g, w_in, cos, sin, gq_a, gq_c, tm, prev=None):
    b, t, d = x.shape
    tok = lambda w: pl.BlockSpec((1, tm, w), lambda i, j: (i, j, 0))
    once = lambda a: pl.BlockSpec(a.shape, lambda i, j: (0,) * a.ndim,
                                  pipeline_mode=pl.Buffered(1))
    per_batch = lambda a: pl.BlockSpec((1, 1, a.shape[2]), (lambda i, j: (i, 0, 0))
                                       if a.shape[0] > 1 else (lambda i, j: (0, 0, 0)))
    widths = (Q_W, 2 * LANES, 2 * LANES, Q_W, D_MODEL, Q_W, 2 * LANES, 2 * LANES, Q_W)
    args = [x, mod, norm_g, w_in, cos, sin, gq_a, gq_c]
    in_specs = [tok(d), per_batch(mod), once(norm_g), once(w_in),
                pl.BlockSpec((tm, LANES), lambda i, j: (j, 0)),
                pl.BlockSpec((tm, LANES), lambda i, j: (j, 0)),
                once(gq_a), once(gq_c)]
    out_specs = [tok(w) for w in widths]
    out_shape = [jax.ShapeDtypeStruct((b, t, w), BF16) for w in widths]
    if prev is not None:
        gate, a, bmix, c, w_out = prev
        args = [gate, a, bmix, c, w_out] + args
        in_specs = [per_batch(gate), tok(Q_W), tok(HY_W), tok(Q_W), once(w_out)] + in_specs
        out_specs = [tok(d)] + out_specs
        out_shape = [jax.ShapeDtypeStruct((b, t, d), F32)] + out_shape
    return pl.pallas_call(
        functools.partial(_proj_kernel, prev_out=prev is not None),
        grid=(b, t // tm),
        in_specs=in_specs,
        out_specs=out_specs,
        out_shape=out_shape,
        compiler_params=_cparams(("parallel", "arbitrary")),
        name="in_projection",
    )(*args)


def _proj_kv_kernel(x_ref, mod_ref, ng_ref, w_ref, gk_ref, ka_ref, va_ref, kc_ref, vc_ref):
    x = x_ref[0]
    ms = jnp.mean(x * x, axis=-1, keepdims=True)
    y = x * lax.rsqrt(ms + NORM_EPS) * ng_ref[...]
    h = (y * (1.0 + mod_ref[0, :, D_MODEL:2 * D_MODEL]) + mod_ref[0, :, 0:D_MODEL]).astype(BF16)
    acc = jnp.dot(h, w_ref[...], preferred_element_type=F32)
    k = jnp.concatenate([acc[:, 0:KV_W], acc[:, 2 * KV_W:3 * KV_W]], axis=1)
    k = k * lax.rsqrt(_head_sum_squares(k) * (1.0 / HEAD_DIM) + NORM_EPS) * gk_ref[...]
    ka_ref[0] = _with_swapped(k[:, 0:KV_W])
    kc_ref[0] = _with_swapped(k[:, KV_W:])
    va_ref[0] = _with_swapped(acc[:, KV_W:2 * KV_W])
    vc_ref[0] = _with_swapped(acc[:, 3 * KV_W:])


def _kv_projection(x, mod, norm_g, w_kv, gk, tm):
    b, t, d = x.shape
    tok = lambda w: pl.BlockSpec((1, tm, w), lambda i, j: (i, j, 0))
    full = lambda a: pl.BlockSpec(a.shape, lambda i, j: (0,) * a.ndim)
    return pl.pallas_call(
        _proj_kv_kernel,
        grid=(b, t // tm),
        in_specs=[tok(d), full(mod), full(norm_g), full(w_kv), full(gk)],
        out_specs=[tok(2 * LANES)] * 4,
        out_shape=[jax.ShapeDtypeStruct((b, t, 2 * LANES), BF16)] * 4,
        compiler_params=_cparams(("parallel", "arbitrary")),
        name="kv_projection",
    )(x, mod, norm_g, w_kv, gk)


def _attn_kernel(*refs, tq, n_sub, lat_mode, has_sink, n_ctx, lat_len):
    refs = list(refs)
    sink_ref = refs.pop(0) if has_sink else None
    q_ref, g_ref, kc_ref, vc_ref = refs[:4]
    refs = refs[4:]
    if lat_mode is not None:
        kl_ref, vl_ref = refs[:2]
        refs = refs[2:]
    o_ref, kpad_s, vaug_s = refs
    i = pl.program_id(1)

    @pl.when(i == 0)
    def _stage_keys():
        lo = lax.broadcasted_iota(jnp.int32, (1, LANES), 1) < HEAD_DIM

        def fill(row0, n, k_ref, v_ref):
            rows = slice(row0, row0 + n)
            for dst, src, width in ((kpad_s, k_ref, LANES), (vaug_s, v_ref, 2 * LANES)):
                t = src[0, :, 0:LANES]
                sw = src[0, :, LANES:2 * LANES]
                zero = jnp.zeros_like(t)
                for blk, val in enumerate((jnp.where(lo, t, zero), jnp.where(lo, zero, sw),
                                           jnp.where(lo, sw, zero), jnp.where(lo, zero, t))):
                    dst[rows, blk * width:blk * width + LANES] = val

        fill(0, n_ctx, kc_ref, vc_ref)
        if lat_mode is not None:
            fill(n_ctx, lat_len, kl_ref, vl_ref)

    @pl.when((i == 0) & (pl.program_id(0) == 0))
    def _stage_ones():
        for blk in range(2 * N_KV):
            vaug_s[:, (2 * blk + 1) * LANES:(2 * blk + 2) * LANES] = jnp.ones(
                (vaug_s.shape[0], LANES), BF16)

    n_keys = n_ctx + lat_len
    for sub, pair in [(s_, p_) for s_ in range(n_sub) for p_ in range(N_HEADS // 2)]:
        q_rows = slice(sub * tq, (sub + 1) * tq)
        if lat_mode == "window":
            if pair == 0:
                span = tq + 2 * WINDOW
                q0 = (i * n_sub + sub) * tq
                start = jnp.clip(q0 - WINDOW, 0, lat_len - span)
                qpos = q0 + lax.broadcasted_iota(jnp.int32, (tq, span), 0)
                kpos = start + lax.broadcasted_iota(jnp.int32, (tq, span), 1)
                valid = jnp.abs(qpos - kpos) <= WINDOW
                band = pl.ds(pl.multiple_of(n_ctx + start, WINDOW), span)
            key_rows = [(slice(0, n_ctx), None), (band, valid)]
        else:
            key_rows = [(slice(r, min(r + KEY_CHUNK, n_keys)), None)
                        for r in range(0, n_keys, KEY_CHUNK)]
        lanes = slice(pair * LANES, (pair + 1) * LANES)
        qp = q_ref[0, q_rows, lanes]
        acc = jnp.zeros((tq, LANES), F32)
        for parity in range(2):
            head = 2 * pair + parity
            blk = 2 * (head // (N_HEADS // N_KV)) + parity
            kcols = slice(blk * LANES, (blk + 1) * LANES)
            vcols = slice(2 * blk * LANES, 2 * (blk + 1) * LANES)
            scores = []
            for rows, mask in key_rows:
                s = lax.dot_general(qp, kpad_s[rows, kcols], _NT, preferred_element_type=F32)
                scores.append(s if mask is None else jnp.where(mask, s, NEG_INF))
            m = functools.reduce(jnp.maximum,
                                 [jnp.max(s, axis=-1, keepdims=True) for s in scores])
            if has_sink:
                sink = sink_ref[head] * LOG2E
                m = jnp.maximum(m, sink)
            o = functools.reduce(jnp.add, [
                jnp.dot(jnp.exp2(s - m).astype(BF16), vaug_s[rows, vcols],
                        preferred_element_type=F32)
                for s, (rows, _) in zip(scores, key_rows)])
            den = o[:, LANES:]
            if has_sink:
                den = den + jnp.exp2(sink - m)
            acc = acc + o[:, :LANES] / den
        o_ref[0, q_rows, lanes] = (acc * g_ref[0, q_rows, lanes].astype(F32)).astype(BF16)


def _attention(q, g, k_ctx, v_ctx, k_lat, v_lat, sink, lat_mode, tq, n_sub, name):
    b, t, _ = q.shape
    n_ctx = k_ctx.shape[1]
    has_sink = sink is not None
    tok = pl.BlockSpec((1, n_sub * tq, Q_W), lambda i, j: (i, j, 0))
    whole = lambda n: pl.BlockSpec((1, n, 2 * LANES), lambda i, j: (i, 0, 0))
    args, specs = [], []
    if has_sink:
        args.append(sink)
        specs.append(pl.BlockSpec(memory_space=pltpu.SMEM))
    args += [q, g, k_ctx, v_ctx]
    specs += [tok, tok, whole(n_ctx), whole(n_ctx)]
    lat_len = 0
    if lat_mode is not None:
        lat_len = k_lat.shape[1]
        args += [k_lat, v_lat]
        specs += [whole(lat_len), whole(lat_len)]
    n_keys = n_ctx + lat_len
    return pl.pallas_call(
        functools.partial(_attn_kernel, tq=tq, n_sub=n_sub, lat_mode=lat_mode,
                          has_sink=has_sink, n_ctx=n_ctx, lat_len=lat_len),
        grid=(b, t // (n_sub * tq)),
        in_specs=specs,
        out_specs=tok,
        out_shape=jax.ShapeDtypeStruct((b, t, Q_W), BF16),
        scratch_shapes=[pltpu.VMEM((n_keys, 2 * N_KV * LANES), BF16),
                        pltpu.VMEM((n_keys, 4 * N_KV * LANES), BF16)],
        compiler_params=_cparams(("arbitrary", "arbitrary")),
        name=name,
    )(*args)


def _filter_kernel(z_ref, w1_ref, b1_ref, w2_ref, b2_ref, w3_ref, fr_ref, dl_ref, o_ref, *, lf):
    hp = lax.Precision.HIGHEST
    z = z_ref[...]
    h = jnp.sin(fr_ref[0:1, :] * (jnp.dot(z, w1_ref[...], precision=hp,
                                          preferred_element_type=F32) + b1_ref[...]))
    h = jnp.sin(fr_ref[1:2, :] * (jnp.dot(h, w2_ref[...], precision=hp,
                                          preferred_element_type=F32) + b2_ref[...]))
    decay = jnp.exp(-z[:, 0:1] * dl_ref[...])
    first_row = lax.broadcasted_iota(jnp.int32, decay.shape, 0) == 0
    for order in range(2):
        c0 = 2 * order * HY_W
        bwd = jnp.dot(h[:lf], w3_ref[:, c0 + HY_W:c0 + 2 * HY_W], precision=hp,
                      preferred_element_type=F32)
        fwd = jnp.dot(h[lf:], w3_ref[:, c0:c0 + HY_W], precision=hp,
                      preferred_element_type=F32)
        taps = jnp.where(first_row, 0.0, jnp.concatenate([bwd, fwd], axis=0) * decay)
        taps = taps / jnp.sum(jnp.abs(taps), axis=0, keepdims=True)
        t = taps.T.astype(BF16).astype(F32)
        lo = pltpu.bitcast(t, jnp.uint32) >> 16
        hi = pltpu.bitcast(pltpu.roll(t, 1, axis=1), jnp.uint32) & jnp.uint32(0xFFFF0000)
        o_ref[order] = pltpu.bitcast(lo | hi, jnp.int32)


def _hyena_filters(lf, w1, b1, w2, b2, w3, freq):
    t = jnp.linspace(0.0, 1.0, lf, dtype=F32)[:, None]
    bands = jnp.linspace(1e-4, HY_BANDS - 1, HY_BANDS, dtype=F32)
    w = 2.0 * math.pi * jnp.arange(lf, dtype=F32)[:, None] / lf
    z = jnp.concatenate([t, jnp.cos(bands * w), jnp.sin(bands * w)], axis=-1)
    z = jnp.concatenate([z[:1], z[:0:-1], z], axis=0)
    z = jnp.pad(z, ((0, 0), (0, FEAT_PAD - z.shape[1])))
    w1p = jnp.pad(w1, ((0, FEAT_PAD - w1.shape[0]), (0, 0)))
    min_decay = math.log(HY_TARGET) / HY_SLOW_DECAY
    max_decay = math.log(HY_TARGET) / HY_FAST_DECAY
    deltas = jnp.abs(jnp.linspace(min_decay, max_decay, HY_W, dtype=F32))[None, :]
    full = lambda a: pl.BlockSpec(a.shape, lambda o: (0,) * a.ndim)
    ins = (z, w1p, b1[None, :], w2, b2[None, :], w3, freq, deltas)
    return pl.pallas_call(
        functools.partial(_filter_kernel, lf=lf),
        grid=(1,),
        in_specs=[full(a) for a in ins],
        out_specs=pl.BlockSpec((2, HY_W, 2 * lf), lambda o: (0, 0, 0)),
        out_shape=jax.ShapeDtypeStruct((2, HY_W, 2 * lf), jnp.int32),
        compiler_params=_cparams(("arbitrary",)),
        name="hyena_filter",
    )(*ins)


def _hyena_kernel(cw_ref, cb_ref, db_ref, v_ref, x1_ref, x2_ref, g_ref, taps_ref, shift_ref,
                  o_ref, *, cw, n_blk, batch):
    rows = n_blk * batch
    c0 = pl.program_id(0) * cw
    lane = lax.broadcasted_iota(jnp.int32, (1, TOEP), 1)
    zero_blk = jnp.zeros((batch, TOEP), F32)

    def short_conv(p_b, ch):
        shifted = jnp.dot(p_b, shift_ref[...], preferred_element_type=F32)
        p = p_b.astype(F32)
        prev = shifted[:, :TOEP]
        nxt = shifted[:, TOEP:]
        if n_blk > 1:
            prev_edge = jnp.concatenate([zero_blk, prev[:rows - batch]], axis=0)
            next_edge = jnp.concatenate([nxt[batch:], zero_blk], axis=0)
        else:
            prev_edge = jnp.zeros_like(p)
            next_edge = jnp.zeros_like(p)
        prev = jnp.where(lane == 0, prev_edge, prev)
        nxt = jnp.where(lane == TOEP - 1, next_edge, nxt)
        return cb_ref[ch] + cw_ref[0, ch] * prev + cw_ref[1, ch] * p + cw_ref[2, ch] * nxt

    def long_conv(z, order, ci):
        words = taps_ref[order, ci]
        skew = pltpu.roll(jnp.broadcast_to(words, (TOEP // 2, words.shape[1])), 0, axis=1,
                          stride=2, stride_axis=0)
        skew = pltpu.bitcast(skew, BF16)
        zb = z.astype(BF16)
        acc = [jnp.zeros((batch, TOEP), F32) for _ in range(n_blk)]
        for d in range(-(n_blk - 1), n_blk):
            n_out = n_blk - abs(d)
            src = max(0, -d) * batch
            col = (n_blk + d) * TOEP
            part = jnp.dot(zb[src:src + n_out * batch], skew[:, col:col + TOEP],
                           preferred_element_type=F32)
            for k in range(n_out):
                acc[max(0, d) + k] = acc[max(0, d) + k] + part[k * batch:(k + 1) * batch]
        return jnp.concatenate(acc, axis=0) if n_blk > 1 else acc[0]

    def body(k, carry):
        cis = [k * HY_UNROLL + u for u in range(HY_UNROLL)]
        zs = [short_conv(v_ref[ci], c0 + ci) for ci in cis]
        for order, x_ref in enumerate((x1_ref, x2_ref)):
            xs = [short_conv(x_ref[ci], (order + 1) * HY_W + c0 + ci) for ci in cis]
            ys = [long_conv(z, order, ci) for z, ci in zip(zs, cis)]
            zs = [x * (y + z * db_ref[order, c0 + ci]) for x, y, z, ci in zip(xs, ys, zs, cis)]
        for z, ci in zip(zs, cis):
            o_ref[ci] = (z * _silu(g_ref[ci].astype(F32))).astype(o_ref.dtype)
        return carry

    lax.fori_loop(0, cw // HY_UNROLL, body, 0)


def _hyena(hy_t, taps, conv_w, conv_b, d_bias, n_blk, batch, cw=HY_CW):
    rows = n_blk * batch
    smem = pl.BlockSpec(memory_space=pltpu.SMEM)
    slab = lambda off: pl.BlockSpec((cw, rows, TOEP), lambda c: (off // cw + c, 0, 0))
    taps4 = taps.reshape(2, HY_W, 1, taps.shape[-1])
    pos = jnp.arange(TOEP)
    shift = jnp.concatenate([pos[:, None] == (pos[None, :] - 1) % TOEP,
                             pos[:, None] == (pos[None, :] + 1) % TOEP], axis=1).astype(BF16)
    return pl.pallas_call(
        functools.partial(_hyena_kernel, cw=cw, n_blk=n_blk, batch=batch),
        grid=(HY_W // cw,),
        in_specs=[smem, smem, smem, slab(0), slab(HY_W), slab(2 * HY_W), slab(3 * HY_W),
                  pl.BlockSpec((2, cw, 1, taps.shape[-1]), lambda c: (0, c, 0, 0)),
                  pl.BlockSpec((TOEP, 2 * TOEP), lambda c: (0, 0))],
        out_specs=pl.BlockSpec((cw, rows, TOEP), lambda c: (c, 0, 0)),
        out_shape=jax.ShapeDtypeStruct((HY_W, rows, TOEP), BF16),
        compiler_params=_cparams(("arbitrary",)),
        name="hyena_mixer",
    )(conv_w, conv_b, d_bias, hy_t, hy_t, hy_t, hy_t, taps4, shift)


def _out_kernel(x_ref, gate_ref, a_ref, b_ref, c_ref, w_ref, o_ref):
    o_ref[0] = x_ref[0] + gate_ref[0] * _mix_out(a_ref[0], b_ref[0], c_ref[0], w_ref)


def _out_projection(x, gate, a, bmix, c, w_out, tm):
    b, t, d = x.shape
    per_batch = gate.shape[0] > 1
    tok = lambda w: pl.BlockSpec((1, tm, w), lambda i, j: (i, j, 0))
    return pl.pallas_call(
        _out_kernel,
        grid=(b, t // tm),
        in_specs=[tok(d),
                  pl.BlockSpec((1, 1, d), (lambda i, j: (i, 0, 0)) if per_batch
                               else (lambda i, j: (0, 0, 0))),
                  tok(Q_W), tok(HY_W), tok(Q_W),
                  pl.BlockSpec(w_out.shape, lambda i, j: (0, 0))],
        out_specs=tok(d),
        out_shape=jax.ShapeDtypeStruct((b, t, d), F32),
        compiler_params=_cparams(("parallel", "arbitrary")),
        name="out_projection",
    )(x, gate, a, bmix, c, w_out)


def _rope_tables(t_len):
    pos = jnp.arange(t_len)
    n_freq = ROPE_FREQS
    inv_freq = ROPE_THETA ** (-jnp.arange(n_freq, dtype=F32) / n_freq)
    ang = jnp.stack([(pos // GRID_W).astype(F32)[:, None] * inv_freq,
                     (pos % GRID_W).astype(F32)[:, None] * inv_freq], axis=1)
    cos = jnp.cos(ang)[:, :, None, :]
    sin = jnp.sin(ang)[:, :, None, :]
    cos = jnp.broadcast_to(cos, (t_len, 2, 2, n_freq)).reshape(t_len, HEAD_DIM)
    sin = jnp.concatenate([-sin, sin], axis=2).reshape(t_len, HEAD_DIM)
    return jnp.tile(cos, (1, 2)), jnp.tile(sin, (1, 2))


def _to_channel_major(hy, n_blk):
    b, t, c = hy.shape
    return hy.reshape(b, n_blk, TOEP, c).transpose(3, 1, 0, 2).reshape(c, n_blk * b, TOEP)


def _to_token_major(y, n_blk, batch):
    c = y.shape[0]
    return y.reshape(c, n_blk, batch, TOEP).transpose(2, 1, 3, 0).reshape(batch, n_blk * TOEP, c)


def kernel(x, c, ctx, c_ctx, norm_g, w_mod, b_mod, w_in, w_out, qn_a, kn_a, qn_c, kn_c, sink_c,
           hy_conv_w, hy_conv_b, hy_w1, hy_b1, hy_w2, hy_b2, hy_w3, hy_freq, hy_bias):
    depth = w_in.shape[0]
    batch, seq, _ = x.shape
    n_ctx = ctx.shape[1]
    n_blk = seq // TOEP
    assert x.shape[2] == D_MODEL and seq % (ATTN_TQ * ATTN_TILES) == 0 and seq % GRID_W == 0
    assert seq % TOEP == 0 and n_ctx == TOEP and seq >= ATTN_TQ + 2 * WINDOW
    assert batch % 16 == 0 and HY_W % HY_CW == 0 and HY_CW % HY_UNROLL == 0

    rows = -(-(batch + 1) // 8) * 8
    c_all = jnp.concatenate([c, c_ctx[None]], axis=0)
    c_all = jnp.pad(c_all, ((0, rows - batch - 1), (0, 0)))
    mod = _modulation(c_all, w_mod, b_mod)

    cos_l, sin_l = _rope_tables(seq)
    cos_c = jnp.ones((n_ctx, LANES), F32)
    sin_c = jnp.zeros((n_ctx, LANES), F32)
    w_in_b = w_in.astype(BF16)
    w_out_b = w_out.astype(BF16)

    pending = None
    for l in range(depth):
        last = l == depth - 1
        mod_x = mod[l, :batch, None, :]
        mod_c = mod[l, batch:batch + 1, None, :]
        ng = norm_g[l][None, :]
        gq_a = jnp.concatenate([jnp.tile(qn_a[l], N_HEADS), jnp.tile(kn_a[l], N_KV)])[None, :]
        gq_c = jnp.concatenate([jnp.tile(qn_c[l], N_HEADS), jnp.tile(kn_c[l], N_KV)])[None, :]

        outs = _projection(x, mod_x, ng, w_in_b[l], cos_l, sin_l, gq_a, gq_c,
                           tm=min(seq, PROJ_TM), prev=pending)
        if pending is not None:
            x, outs = outs[0], outs[1:]
        qa, ka, va, ga, hy, qc, kc, vc, gc = outs
        if last:
            c_off = 2 * D_MODEL + Q_W
            w_kv = jnp.concatenate([w_in_b[l][:, Q_W:Q_W + 2 * KV_W],
                                    w_in_b[l][:, c_off:c_off + 2 * KV_W]], axis=1)
            gk = jnp.concatenate([jnp.tile(kn_a[l], N_KV), jnp.tile(kn_c[l], N_KV)])[None, :]
            ka_c, va_c, kc_c, vc_c = _kv_projection(ctx, mod_c, ng, w_kv, gk, tm=n_ctx)
        else:
            qa_c, ka_c, va_c, ga_c, hy_c, qc_c, kc_c, vc_c, gc_c = _projection(
                ctx, mod_c, ng, w_in_b[l], cos_c, sin_c, gq_a, gq_c, tm=n_ctx)

        a_out = _attention(qa, ga, ka_c, va_c, ka, va, None, "full", ATTN_TQ, ATTN_TILES,
                           "attn_global")
        c_out = _attention(qc, gc, kc_c, vc_c, kc, vc, sink_c[l], "window", ATTN_TQ, ATTN_TILES,
                           "attn_window")
        taps = _hyena_filters(seq, hy_w1[l], hy_b1[l], hy_w2[l], hy_b2[l], hy_w3[l], hy_freq[l])
        b_out = _hyena(_to_channel_major(hy, n_blk), taps, hy_conv_w[l], hy_conv_b[l],
                       hy_bias[l], n_blk, batch)
        b_out = _to_token_major(b_out, n_blk, batch)
        pending = (mod_x[:, :, 2 * D_MODEL:], a_out, b_out, c_out, w_out_b[l])

        if not last:
            a_c = _attention(qa_c, ga_c, ka_c, va_c, None, None, None, None, n_ctx, 1,
                             "attn_ctx_a")
            c_c = _attention(qc_c, gc_c, kc_c, vc_c, None, None, sink_c[l], None, n_ctx, 1,
                             "attn_ctx_c")
            taps_c = _hyena_filters(n_ctx, hy_w1[l], hy_b1[l], hy_w2[l], hy_b2[l], hy_w3[l],
                                    hy_freq[l])
            b_c = _hyena(_to_channel_major(hy_c, 1), taps_c, hy_conv_w[l], hy_conv_b[l],
                         hy_bias[l], 1, batch)
            b_c = _to_token_major(b_c, 1, batch)
            ctx = _out_projection(ctx, mod_c[:, :, 2 * D_MODEL:], a_c, b_c, c_c, w_out_b[l], n_ctx)
    gate, a_out, b_out, c_out, w_o = pending
    return _out_projection(x, gate, a_out, b_out, c_out, w_o, min(seq, PROJ_TM))
```

```python
import functools
import math

import jax
import jax.numpy as jnp
from jax import lax
from jax.experimental import pallas as pl
from jax.experimental.pallas import tpu as pltpu

F32 = jnp.float32
BF16 = jnp.bfloat16

D_MODEL = 1024
HEAD_DIM = 64
N_HEADS = 6
N_KV = 2
Q_W = N_HEADS * HEAD_DIM
KV_W = N_KV * HEAD_DIM
HY_W = 256
GRID_W = 64
WINDOW = 128
ROPE_THETA = 10000.0
NORM_EPS = 1e-6
NEG_INF = -1e30
HY_BANDS = 16
HY_FAST_DECAY = 0.3
HY_SLOW_DECAY = 1.5
HY_TARGET = 1e-2
LANES = 128
TOEP = 256
FEAT_PAD = 128
ROPE_FREQS = HEAD_DIM // 4
ROPE_PAIR = 2 * ROPE_FREQS

PROJ_TM = 1024
PROJ_SUB = 256
ATTN_TQ = 256
ATTN_TILES = 8
KEY_CHUNK = 768
HY_CW = 8
HY_UNROLL = 4
VMEM_LIMIT = 56 * 1024 * 1024

_NT = (((1,), (1,)), ((), ()))
LOG2E = 1.4426950408889634
Q_SCALE = LOG2E / math.sqrt(HEAD_DIM)


def _cparams(sem):
    return pltpu.CompilerParams(dimension_semantics=sem, vmem_limit_bytes=VMEM_LIMIT)


def _silu(x):
    return x * (1.0 / (1.0 + jnp.exp(-x)))


def _mod_kernel(c_ref, w_ref, b_ref, o_ref):
    s = _silu(c_ref[...])
    o_ref[0] = jnp.dot(s, w_ref[0], precision=lax.Precision.HIGHEST,
                       preferred_element_type=F32) + b_ref[0]


def _modulation(c_all, w_mod, b_mod):
    depth, d, n = w_mod.shape
    rows = c_all.shape[0]
    nb = n // d
    return pl.pallas_call(
        _mod_kernel,
        grid=(depth, nb),
        in_specs=[pl.BlockSpec((rows, d), lambda l, j: (0, 0)),
                  pl.BlockSpec((1, d, d), lambda l, j: (l, 0, j)),
                  pl.BlockSpec((1, 1, d), lambda l, j: (l, 0, j))],
        out_specs=pl.BlockSpec((1, rows, d), lambda l, j: (l, 0, j)),
        out_shape=jax.ShapeDtypeStruct((depth, rows, n), F32),
        compiler_params=_cparams(("arbitrary", "arbitrary")),
        name="modulation",
    )(c_all, w_mod, b_mod.reshape(depth, 1, n))


def _head_sum_squares(x):
    lane = lax.broadcasted_iota(jnp.int32, (1, LANES), 1)
    low = lane < HEAD_DIM
    out = []
    for s in range(0, x.shape[1], LANES):
        sq = x[:, s:s + LANES] * x[:, s:s + LANES]
        s_low = jnp.sum(jnp.where(low, sq, 0.0), axis=-1, keepdims=True)
        s_high = jnp.sum(jnp.where(low, 0.0, sq), axis=-1, keepdims=True)
        out.append(jnp.where(low, s_low, s_high))
    return jnp.concatenate(out, axis=1)


def _head_norm_rope(qk, gain, cos, sin):
    width = qk.shape[1]
    y = qk * lax.rsqrt(_head_sum_squares(qk) * (1.0 / HEAD_DIM) + NORM_EPS) * gain
    lane = lax.broadcasted_iota(jnp.int32, (1, LANES), 1)
    first_half = (lane % ROPE_PAIR) < ROPE_FREQS
    out = []
    for s in range(0, width, LANES):
        ys = y[:, s:s + LANES]
        partner = jnp.where(first_half, pltpu.roll(ys, LANES - ROPE_FREQS, axis=1),
                            pltpu.roll(ys, ROPE_FREQS, axis=1))
        out.append(ys * cos + partner * sin)
    return jnp.concatenate(out, axis=1)


def _with_swapped(t):
    return jnp.concatenate([t, pltpu.roll(t, HEAD_DIM, axis=1)], axis=1).astype(BF16)


def _mix_out(a, b, c, w_ref):
    return jnp.dot(jnp.concatenate([a, b, c], axis=1), w_ref[...], preferred_element_type=F32)


def _proj_kernel(*refs, prev_out):
    if prev_out:
        gate_ref, a_ref, b_ref, c_ref, wo_ref = refs[:5]
        refs = refs[5:]
    (x_ref, mod_ref, ng_ref, w_ref, cos_ref, sin_ref, gq_a_ref, gq_c_ref) = refs[:8]
    outs = refs[8:]
    if prev_out:
        xo_ref, outs = outs[0], outs[1:]
    qa_ref, ka_ref, va_ref, ga_ref, hy_ref, qc_ref, kc_ref, vc_ref, gc_ref = outs
    shift = mod_ref[0, :, 0:D_MODEL]
    scale = mod_ref[0, :, D_MODEL:2 * D_MODEL]
    tm = x_ref.shape[1]
    sub = min(tm, PROJ_SUB)
    for r0 in range(0, tm, sub):
        rows = slice(r0, r0 + sub)
        x = x_ref[0, rows]
        if prev_out:
            x = x + gate_ref[0] * _mix_out(a_ref[0, rows], b_ref[0, rows], c_ref[0, rows], wo_ref)
            xo_ref[0, rows] = x
        ms = jnp.mean(x * x, axis=-1, keepdims=True)
        y = x * lax.rsqrt(ms + NORM_EPS) * ng_ref[...]
        h = (y * (1.0 + scale) + shift).astype(BF16)
        cos = cos_ref[rows]
        sin = sin_ref[rows]

        def attn_branch(col0, gain_ref, q_ref, k_ref, v_ref, g_ref):
            acc = jnp.dot(h, w_ref[:, col0:col0 + D_MODEL], preferred_element_type=F32)
            qk = _head_norm_rope(acc[:, 0:Q_W + KV_W], gain_ref[...], cos, sin)
            q_ref[0, rows] = (qk[:, 0:Q_W] * Q_SCALE).astype(BF16)
            k_ref[0, rows] = _with_swapped(qk[:, Q_W:Q_W + KV_W])
            v_ref[0, rows] = _with_swapped(acc[:, Q_W + KV_W:Q_W + 2 * KV_W])
            g_ref[0, rows] = _silu(acc[:, Q_W + 2 * KV_W:]).astype(BF16)

        attn_branch(0, gq_a_ref, qa_ref, ka_ref, va_ref, ga_ref)
        hy_ref[0, rows] = jnp.dot(h, w_ref[:, D_MODEL:2 * D_MODEL],
                                  preferred_element_type=F32).astype(BF16)
        attn_branch(2 * D_MODEL, gq_c_ref, qc_ref, kc_ref, vc_ref, gc_ref)


def _projection(x, mod, norm_g, w_in, cos, sin, gq_a, gq_c, tm, prev=None):
    b, t, d = x.shape
    tok = lambda w: pl.BlockSpec((1, tm, w), lambda i, j: (i, j, 0))
    once = lambda a: pl.BlockSpec(a.shape, lambda i, j: (0,) * a.ndim,
                                  pipeline_mode=pl.Buffered(1))
    per_batch = lambda a: pl.BlockSpec((1, 1, a.shape[2]), (lambda i, j: (i, 0, 0))
                                       if a.shape[0] > 1 else (lambda i, j: (0, 0, 0)))
    widths = (Q_W, 2 * LANES, 2 * LANES, Q_W, D_MODEL, Q_W, 2 * LANES, 2 * LANES, Q_W)
    args = [x, mod, norm_g, w_in, cos, sin, gq_a, gq_c]
    in_specs = [tok(d), per_batch(mod), once(norm_g), once(w_in),
                pl.BlockSpec((tm, LANES), lambda i, j: (j, 0)),
                pl.BlockSpec((tm, LANES), lambda i, j: (j, 0)),
                once(gq_a), once(gq_c)]
    out_specs = [tok(w) for w in widths]
    out_shape = [jax.ShapeDtypeStruct((b, t, w), BF16) for w in widths]
    if prev is not None:
        gate, a, bmix, c, w_out = prev
        args = [gate, a, bmix, c, w_out] + args
        in_specs = [per_batch(gate), tok(Q_W), tok(HY_W), tok(Q_W), once(w_out)] + in_specs
        out_specs = [tok(d)] + out_specs
        out_shape = [jax.ShapeDtypeStruct((b, t, d), F32)] + out_shape
    return pl.pallas_call(
        functools.partial(_proj_kernel, prev_out=prev is not None),
        grid=(b, t // tm),
        in_specs=in_specs,
        out_specs=out_specs,
        out_shape=out_shape,
        compiler_params=_cparams(("parallel", "arbitrary")),
        name="in_projection",
    )(*args)


def _proj_kv_kernel(x_ref, mod_ref, ng_ref, w_ref, gk_ref, ka_ref, va_ref, kc_ref, vc_ref):
    x = x_ref[0]
    ms = jnp.mean(x * x, axis=-1, keepdims=True)
    y = x * lax.rsqrt(ms + NORM_EPS) * ng_ref[...]
    h = (y * (1.0 + mod_ref[0, :, D_MODEL:2 * D_MODEL]) + mod_ref[0, :, 0:D_MODEL]).astype(BF16)
    acc = jnp.dot(h, w_ref[...], preferred_element_type=F32)
    k = jnp.concatenate([acc[:, 0:KV_W], acc[:, 2 * KV_W:3 * KV_W]], axis=1)
    k = k * lax.rsqrt(_head_sum_squares(k) * (1.0 / HEAD_DIM) + NORM_EPS) * gk_ref[...]
    ka_ref[0] = _with_swapped(k[:, 0:KV_W])
    kc_ref[0] = _with_swapped(k[:, KV_W:])
    va_ref[0] = _with_swapped(acc[:, KV_W:2 * KV_W])
    vc_ref[0] = _with_swapped(acc[:, 3 * KV_W:])


def _kv_projection(x, mod, norm_g, w_kv, gk, tm):
    b, t, d = x.shape
    tok = lambda w: pl.BlockSpec((1, tm, w), lambda i, j: (i, j, 0))
    full = lambda a: pl.BlockSpec(a.shape, lambda i, j: (0,) * a.ndim)
    return pl.pallas_call(
        _proj_kv_kernel,
        grid=(b, t // tm),
        in_specs=[tok(d), full(mod), full(norm_g), full(w_kv), full(gk)],
        out_specs=[tok(2 * LANES)] * 4,
        out_shape=[jax.ShapeDtypeStruct((b, t, 2 * LANES), BF16)] * 4,
        compiler_params=_cparams(("parallel", "arbitrary")),
        name="kv_projection",
    )(x, mod, norm_g, w_kv, gk)


def _attn_kernel(*refs, tq, n_sub, lat_mode, has_sink, n_ctx, lat_len):
    refs = list(refs)
    sink_ref = refs.pop(0) if has_sink else None
    q_ref, g_ref, kc_ref, vc_ref = refs[:4]
    refs = refs[4:]
    if lat_mode is not None:
        kl_ref, vl_ref = refs[:2]
        refs = refs[2:]
    o_ref, kpad_s, vaug_s = refs
    i = pl.program_id(1)

    @pl.when(i == 0)
    def _stage_keys():
        lo = lax.broadcasted_iota(jnp.int32, (1, LANES), 1) < HEAD_DIM

        def fill(row0, n, k_ref, v_ref):
            rows = slice(row0, row0 + n)
            for dst, src, width in ((kpad_s, k_ref, LANES), (vaug_s, v_ref, 2 * LANES)):
                t = src[0, :, 0:LANES]
                sw = src[0, :, LANES:2 * LANES]
                zero = jnp.zeros_like(t)
                for blk, val in enumerate((jnp.where(lo, t, zero), jnp.where(lo, zero, sw),
                                           jnp.where(lo, sw, zero), jnp.where(lo, zero, t))):
                    dst[rows, blk * width:blk * width + LANES] = val

        fill(0, n_ctx, kc_ref, vc_ref)
        if lat_mode is not None:
            fill(n_ctx, lat_len, kl_ref, vl_ref)

    @pl.when((i == 0) & (pl.program_id(0) == 0))
    def _stage_ones():
        for blk in range(2 * N_KV):
            vaug_s[:, (2 * blk + 1) * LANES:(2 * blk + 2) * LANES] = jnp.ones(
                (vaug_s.shape[0], LANES), BF16)

    n_keys = n_ctx + lat_len
    for sub, pair in [(s_, p_) for s_ in range(n_sub) for p_ in range(N_HEADS // 2)]:
        q_rows = slice(sub * tq, (sub + 1) * tq)
        if lat_mode == "window":
            if pair == 0:
                span = tq + 2 * WINDOW
                q0 = (i * n_sub + sub) * tq
                start = jnp.clip(q0 - WINDOW, 0, lat_len - span)
                qpos = q0 + lax.broadcasted_iota(jnp.int32, (tq, span), 0)
                kpos = start + lax.broadcasted_iota(jnp.int32, (tq, span), 1)
                valid = jnp.abs(qpos - kpos) <= WINDOW
                band = pl.ds(pl.multiple_of(n_ctx + start, WINDOW), span)
            key_rows = [(slice(0, n_ctx), None), (band, valid)]
        else:
            key_rows = [(slice(r, min(r + KEY_CHUNK, n_keys)), None)
                        for r in range(0, n_keys, KEY_CHUNK)]
        lanes = slice(pair * LANES, (pair + 1) * LANES)
        qp = q_ref[0, q_rows, lanes]
        acc = jnp.zeros((tq, LANES), F32)
        for parity in range(2):
            head = 2 * pair + parity
            blk = 2 * (head // (N_HEADS // N_KV)) + parity
            kcols = slice(blk * LANES, (blk + 1) * LANES)
            vcols = slice(2 * blk * LANES, 2 * (blk + 1) * LANES)
            scores = []
            for rows, mask in key_rows:
                s = lax.dot_general(qp, kpad_s[rows, kcols], _NT, preferred_element_type=F32)
                scores.append(s if mask is None else jnp.where(mask, s, NEG_INF))
            m = functools.reduce(jnp.maximum,
                                 [jnp.max(s, axis=-1, keepdims=True) for s in scores])
            if has_sink:
                sink = sink_ref[head] * LOG2E
                m = jnp.maximum(m, sink)
            o = functools.reduce(jnp.add, [
                jnp.dot(jnp.exp2(s - m).astype(BF16), vaug_s[rows, vcols],
                        preferred_element_type=F32)
                for s, (rows, _) in zip(scores, key_rows)])
            den = o[:, LANES:]
            if has_sink:
                den = den + jnp.exp2(sink - m)
            acc = acc + o[:, :LANES] / den
        o_ref[0, q_rows, lanes] = (acc * g_ref[0, q_rows, lanes].astype(F32)).astype(BF16)


def _attention(q, g, k_ctx, v_ctx, k_lat, v_lat, sink, lat_mode, tq, n_sub, name):
    b, t, _ = q.shape
    n_ctx = k_ctx.shape[1]
    has_sink = sink is not None
    tok = pl.BlockSpec((1, n_sub * tq, Q_W), lambda i, j: (i, j, 0))
    whole = lambda n: pl.BlockSpec((1, n, 2 * LANES), lambda i, j: (i, 0, 0))
    args, specs = [], []
    if has_sink:
        args.append(sink)
        specs.append(pl.BlockSpec(memory_space=pltpu.SMEM))
    args += [q, g, k_ctx, v_ctx]
    specs += [tok, tok, whole(n_ctx), whole(n_ctx)]
    lat_len = 0
    if lat_mode is not None:
        lat_len = k_lat.shape[1]
        args += [k_lat, v_lat]
        specs += [whole(lat_len), whole(lat_len)]
    n_keys = n_ctx + lat_len
    return pl.pallas_call(
        functools.partial(_attn_kernel, tq=tq, n_sub=n_sub, lat_mode=lat_mode,
                          has_sink=has_sink, n_ctx=n_ctx, lat_len=lat_len),
        grid=(b, t // (n_sub * tq)),
        in_specs=specs,
        out_specs=tok,
        out_shape=jax.ShapeDtypeStruct((b, t, Q_W), BF16),
        scratch_shapes=[pltpu.VMEM((n_keys, 2 * N_KV * LANES), BF16),
                        pltpu.VMEM((n_keys, 4 * N_KV * LANES), BF16)],
        compiler_params=_cparams(("arbitrary", "arbitrary")),
        name=name,
    )(*args)


def _filter_kernel(z_ref, w1_ref, b1_ref, w2_ref, b2_ref, w3_ref, fr_ref, dl_ref, o_ref, *, lf):
    hp = lax.Precision.HIGHEST
    z = z_ref[...]
    h = jnp.sin(fr_ref[0:1, :] * (jnp.dot(z, w1_ref[...], precision=hp,
                                          preferred_element_type=F32) + b1_ref[...]))
    h = jnp.sin(fr_ref[1:2, :] * (jnp.dot(h, w2_ref[...], precision=hp,
                                          preferred_element_type=F32) + b2_ref[...]))
    decay = jnp.exp(-z[:, 0:1] * dl_ref[...])
    first_row = lax.broadcasted_iota(jnp.int32, decay.shape, 0) == 0
    for order in range(2):
        c0 = 2 * order * HY_W
        bwd = jnp.dot(h[:lf], w3_ref[:, c0 + HY_W:c0 + 2 * HY_W], precision=hp,
                      preferred_element_type=F32)
        fwd = jnp.dot(h[lf:], w3_ref[:, c0:c0 + HY_W], precision=hp,
                      preferred_element_type=F32)
        taps = jnp.where(first_row, 0.0, jnp.concatenate([bwd, fwd], axis=0) * decay)
        taps = taps / jnp.sum(jnp.abs(taps), axis=0, keepdims=True)
        t = taps.T.astype(BF16).astype(F32)
        lo = pltpu.bitcast(t, jnp.uint32) >> 16
        hi = pltpu.bitcast(pltpu.roll(t, 1, axis=1), jnp.uint32) & jnp.uint32(0xFFFF0000)
        o_ref[order] = pltpu.bitcast(lo | hi, jnp.int32)


def _hyena_filters(lf, w1, b1, w2, b2, w3, freq):
    t = jnp.linspace(0.0, 1.0, lf, dtype=F32)[:, None]
    bands = jnp.linspace(1e-4, HY_BANDS - 1, HY_BANDS, dtype=F32)
    w = 2.0 * math.pi * jnp.arange(lf, dtype=F32)[:, None] / lf
    z = jnp.concatenate([t, jnp.cos(bands * w), jnp.sin(bands * w)], axis=-1)
    z = jnp.concatenate([z[:1], z[:0:-1], z], axis=0)
    z = jnp.pad(z, ((0, 0), (0, FEAT_PAD - z.shape[1])))
    w1p = jnp.pad(w1, ((0, FEAT_PAD - w1.shape[0]), (0, 0)))
    min_decay = math.log(HY_TARGET) / HY_SLOW_DECAY
    max_decay = math.log(HY_TARGET) / HY_FAST_DECAY
    deltas = jnp.abs(jnp.linspace(min_decay, max_decay, HY_W, dtype=F32))[None, :]
    full = lambda a: pl.BlockSpec(a.shape, lambda o: (0,) * a.ndim)
    ins = (z, w1p, b1[None, :], w2, b2[None, :], w3, freq, deltas)
    return pl.pallas_call(
        functools.partial(_filter_kernel, lf=lf),
        grid=(1,),
        in_specs=[full(a) for a in ins],
        out_specs=pl.BlockSpec((2, HY_W, 2 * lf), lambda o: (0, 0, 0)),
        out_shape=jax.ShapeDtypeStruct((2, HY_W, 2 * lf), jnp.int32),
        compiler_params=_cparams(("arbitrary",)),
        name="hyena_filter",
    )(*ins)


def _hyena_kernel(cw_ref, cb_ref, db_ref, v_ref, x1_ref, x2_ref, g_ref, taps_ref, shift_ref,
                  o_ref, *, cw, n_blk, batch):
    rows = n_blk * batch
    c0 = pl.program_id(0) * cw
    lane = lax.broadcasted_iota(jnp.int32, (1, TOEP), 1)
    zero_blk = jnp.zeros((batch, TOEP), F32)

    def short_conv(p_b, ch):
        shifted = jnp.dot(p_b, shift_ref[...], preferred_element_type=F32)
        p = p_b.astype(F32)
        prev = shifted[:, :TOEP]
        nxt = shifted[:, TOEP:]
        if n_blk > 1:
            prev_edge = jnp.concatenate([zero_blk, prev[:rows - batch]], axis=0)
            next_edge = jnp.concatenate([nxt[batch:], zero_blk], axis=0)
        else:
            prev_edge = jnp.zeros_like(p)
            next_edge = jnp.zeros_like(p)
        prev = jnp.where(lane == 0, prev_edge, prev)
        nxt = jnp.where(lane == TOEP - 1, next_edge, nxt)
        return cb_ref[ch] + cw_ref[0, ch] * prev + cw_ref[1, ch] * p + cw_ref[2, ch] * nxt

    def long_conv(z, order, ci):
        words = taps_ref[order, ci]
        skew = pltpu.roll(jnp.broadcast_to(words, (TOEP // 2, words.shape[1])), 0, axis=1,
                          stride=2, stride_axis=0)
        skew = pltpu.bitcast(skew, BF16)
        zb = z.astype(BF16)
        acc = [jnp.zeros((batch, TOEP), F32) for _ in range(n_blk)]
        for d in range(-(n_blk - 1), n_blk):
            n_out = n_blk - abs(d)
            src = max(0, -d) * batch
            col = (n_blk + d) * TOEP
            part = jnp.dot(zb[src:src + n_out * batch], skew[:, col:col + TOEP],
                           preferred_element_type=F32)
            for k in range(n_out):
                acc[max(0, d) + k] = acc[max(0, d) + k] + part[k * batch:(k + 1) * batch]
        return jnp.concatenate(acc, axis=0) if n_blk > 1 else acc[0]

    def body(k, carry):
        cis = [k * HY_UNROLL + u for u in range(HY_UNROLL)]
        zs = [short_conv(v_ref[ci], c0 + ci) for ci in cis]
        for order, x_ref in enumerate((x1_ref, x2_ref)):
            xs = [short_conv(x_ref[ci], (order + 1) * HY_W + c0 + ci) for ci in cis]
            ys = [long_conv(z, order, ci) for z, ci in zip(zs, cis)]
            zs = [x * (y + z * db_ref[order, c0 + ci]) for x, y, z, ci in zip(xs, ys, zs, cis)]
        for z, ci in zip(zs, cis):
            o_ref[ci] = (z * _silu(g_ref[ci].astype(F32))).astype(o_ref.dtype)
        return carry

    lax.fori_loop(0, cw // HY_UNROLL, body, 0)


def _hyena(hy_t, taps, conv_w, conv_b, d_bias, n_blk, batch, cw=HY_CW):
    rows = n_blk * batch
    smem = pl.BlockSpec(memory_space=pltpu.SMEM)
    slab = lambda off: pl.BlockSpec((cw, rows, TOEP), lambda c: (off // cw + c, 0, 0))
    taps4 = taps.reshape(2, HY_W, 1, taps.shape[-1])
    pos = jnp.arange(TOEP)
    shift = jnp.concatenate([pos[:, None] == (pos[None, :] - 1) % TOEP,
                             pos[:, None] == (pos[None, :] + 1) % TOEP], axis=1).astype(BF16)
    return pl.pallas_call(
        functools.partial(_hyena_kernel, cw=cw, n_blk=n_blk, batch=batch),
        grid=(HY_W // cw,),
        in_specs=[smem, smem, smem, slab(0), slab(HY_W), slab(2 * HY_W), slab(3 * HY_W),
                  pl.BlockSpec((2, cw, 1, taps.shape[-1]), lambda c: (0, c, 0, 0)),
                  pl.BlockSpec((TOEP, 2 * TOEP), lambda c: (0, 0))],
        out_specs=pl.BlockSpec((cw, rows, TOEP), lambda c: (c, 0, 0)),
        out_shape=jax.ShapeDtypeStruct((HY_W, rows, TOEP), BF16),
        compiler_params=_cparams(("arbitrary",)),
        name="hyena_mixer",
    )(conv_w, conv_b, d_bias, hy_t, hy_t, hy_t, hy_t, taps4, shift)


def _out_kernel(x_ref, gate_ref, a_ref, b_ref, c_ref, w_ref, o_ref):
    o_ref[0] = x_ref[0] + gate_ref[0] * _mix_out(a_ref[0], b_ref[0], c_ref[0], w_ref)


def _out_projection(x, gate, a, bmix, c, w_out, tm):
    b, t, d = x.shape
    per_batch = gate.shape[0] > 1
    tok = lambda w: pl.BlockSpec((1, tm, w), lambda i, j: (i, j, 0))
    return pl.pallas_call(
        _out_kernel,
        grid=(b, t // tm),
        in_specs=[tok(d),
                  pl.BlockSpec((1, 1, d), (lambda i, j: (i, 0, 0)) if per_batch
                               else (lambda i, j: (0, 0, 0))),
                  tok(Q_W), tok(HY_W), tok(Q_W),
                  pl.BlockSpec(w_out.shape, lambda i, j: (0, 0))],
        out_specs=tok(d),
        out_shape=jax.ShapeDtypeStruct((b, t, d), F32),
        compiler_params=_cparams(("parallel", "arbitrary")),
        name="out_projection",
    )(x, gate, a, bmix, c, w_out)


def _rope_tables(t_len):
    pos = jnp.arange(t_len)
    n_freq = ROPE_FREQS
    inv_freq = ROPE_THETA ** (-jnp.arange(n_freq, dtype=F32) / n_freq)
    ang = jnp.stack([(pos // GRID_W).astype(F32)[:, None] * inv_freq,
                     (pos % GRID_W).astype(F32)[:, None] * inv_freq], axis=1)
    cos = jnp.cos(ang)[:, :, None, :]
    sin = jnp.sin(ang)[:, :, None, :]
    cos = jnp.broadcast_to(cos, (t_len, 2, 2, n_freq)).reshape(t_len, HEAD_DIM)
    sin = jnp.concatenate([-sin, sin], axis=2).reshape(t_len, HEAD_DIM)
    return jnp.tile(cos, (1, 2)), jnp.tile(sin, (1, 2))


def _to_channel_major(hy, n_blk):
    b, t, c = hy.shape
    return hy.reshape(b, n_blk, TOEP, c).transpose(3, 1, 0, 2).reshape(c, n_blk * b, TOEP)


def _to_token_major(y, n_blk, batch):
    c = y.shape[0]
    return y.reshape(c, n_blk, batch, TOEP).transpose(2, 1, 3, 0).reshape(batch, n_blk * TOEP, c)


def kernel(x, c, ctx, c_ctx, norm_g, w_mod, b_mod, w_in, w_out, qn_a, kn_a, qn_c, kn_c, sink_c,
           hy_conv_w, hy_conv_b, hy_w1, hy_b1, hy_w2, hy_b2, hy_w3, hy_freq, hy_bias):
    depth = w_in.shape[0]
    batch, seq, _ = x.shape
    n_ctx = ctx.shape[1]
    n_blk = seq // TOEP
    assert x.shape[2] == D_MODEL and seq % (ATTN_TQ * ATTN_TILES) == 0 and seq % GRID_W == 0
    assert seq % TOEP == 0 and n_ctx == TOEP and seq >= ATTN_TQ + 2 * WINDOW
    assert batch % 16 == 0 and HY_W % HY_CW == 0 and HY_CW % HY_UNROLL == 0

    rows = -(-(batch + 1) // 8) * 8
    c_all = jnp.concatenate([c, c_ctx[None]], axis=0)
    c_all = jnp.pad(c_all, ((0, rows - batch - 1), (0, 0)))
    mod = _modulation(c_all, w_mod, b_mod)

    cos_l, sin_l = _rope_tables(seq)
    cos_c = jnp.ones((n_ctx, LANES), F32)
    sin_c = jnp.zeros((n_ctx, LANES), F32)
    w_in_b = w_in.astype(BF16)
    w_out_b = w_out.astype(BF16)

    pending = None
    for l in range(depth):
        last = l == depth - 1
        mod_x = mod[l, :batch, None, :]
        mod_c = mod[l, batch:batch + 1, None, :]
        ng = norm_g[l][None, :]
        gq_a = jnp.concatenate([jnp.tile(qn_a[l], N_HEADS), jnp.tile(kn_a[l], N_KV)])[None, :]
        gq_c = jnp.concatenate([jnp.tile(qn_c[l], N_HEADS), jnp.tile(kn_c[l], N_KV)])[None, :]

        outs = _projection(x, mod_x, ng, w_in_b[l], cos_l, sin_l, gq_a, gq_c,
                           tm=min(seq, PROJ_TM), prev=pending)
        if pending is not None:
            x, outs = outs[0], outs[1:]
        qa, ka, va, ga, hy, qc, kc, vc, gc = outs
        if last:
            c_off = 2 * D_MODEL + Q_W
            w_kv = jnp.concatenate([w_in_b[l][:, Q_W:Q_W + 2 * KV_W],
                                    w_in_b[l][:, c_off:c_off + 2 * KV_W]], axis=1)
            gk = jnp.concatenate([jnp.tile(kn_a[l], N_KV), jnp.tile(kn_c[l], N_KV)])[None, :]
            ka_c, va_c, kc_c, vc_c = _kv_projection(ctx, mod_c, ng, w_kv, gk, tm=n_ctx)
        else:
            qa_c, ka_c, va_c, ga_c, hy_c, qc_c, kc_c, vc_c, gc_c = _projection(
                ctx, mod_c, ng, w_in_b[l], cos_c, sin_c, gq_a, gq_c, tm=n_ctx)

        a_out = _attention(qa, ga, ka_c, va_c, ka, va, None, "full", ATTN_TQ, ATTN_TILES,
                           "attn_global")
        c_out = _attention(qc, gc, kc_c, vc_c, kc, vc, sink_c[l], "window", ATTN_TQ, ATTN_TILES,
                           "attn_window")
        taps = _hyena_filters(seq, hy_w1[l], hy_b1[l], hy_w2[l], hy_b2[l], hy_w3[l], hy_freq[l])
        b_out = _hyena(_to_channel_major(hy, n_blk), taps, hy_conv_w[l], hy_conv_b[l],
                       hy_bias[l], n_blk, batch)
        b_out = _to_token_major(b_out, n_blk, batch)
        pending = (mod_x[:, :, 2 * D_MODEL:], a_out, b_out, c_out, w_out_b[l])

        if not last:
            a_c = _attention(qa_c, ga_c, ka_c, va_c, None, None, None, None, n_ctx, 1,
                             "attn_ctx_a")
            c_c = _attention(qc_c, gc_c, kc_c, vc_c, None, None, sink_c[l], None, n_ctx, 1,
                             "attn_ctx_c")
            taps_c = _hyena_filters(n_ctx, hy_w1[l], hy_b1[l], hy_w2[l], hy_b2[l], hy_w3[l],
                                    hy_freq[l])
            b_c = _hyena(_to_channel_major(hy_c, 1), taps_c, hy_conv_w[l], hy_conv_b[l],
                         hy_bias[l], 1, batch)
            b_c = _to_token_major(b_c, 1, batch)
            ctx = _out_projection(ctx, mod_c[:, :, 2 * D_MODEL:], a_c, b_c, c_c, w_out_b[l], n_ctx)
    gate, a_out, b_out, c_out, w_o = pending
    return _out_projection(x, gate, a_out, b_out, c_out, w_o, min(seq, PROJ_TM))
```

```python
import functools
import math

import jax
import jax.numpy as jnp
from jax import lax
from jax.experimental import pallas as pl
from jax.experimental.pallas import tpu as pltpu

F32 = jnp.float32
BF16 = jnp.bfloat16

D_MODEL = 1024
HEAD_DIM = 64
N_HEADS = 6
N_KV = 2
Q_W = N_HEADS * HEAD_DIM
KV_W = N_KV * HEAD_DIM
HY_W = 256
GRID_W = 64
WINDOW = 128
ROPE_THETA = 10000.0
NORM_EPS = 1e-6
NEG_INF = -1e30
HY_BANDS = 16
HY_FAST_DECAY = 0.3
HY_SLOW_DECAY = 1.5
HY_TARGET = 1e-2
LANES = 128
TOEP = 256
FEAT_PAD = 128
ROPE_FREQS = HEAD_DIM // 4
ROPE_PAIR = 2 * ROPE_FREQS

PROJ_TM = 1024
PROJ_SUB = 256
ATTN_TQ = 256
ATTN_TILES = 8
KEY_CHUNK = 768
HY_CW = 8
HY_UNROLL = 4
VMEM_LIMIT = 56 * 1024 * 1024

_NT = (((1,), (1,)), ((), ()))
LOG2E = 1.4426950408889634
Q_SCALE = LOG2E / math.sqrt(HEAD_DIM)


def _cparams(sem):
    return pltpu.CompilerParams(dimension_semantics=sem, vmem_limit_bytes=VMEM_LIMIT)


def _silu(x):
    return x * (1.0 / (1.0 + jnp.exp(-x)))


def _mod_kernel(c_ref, w_ref, b_ref, o_ref):
    s = _silu(c_ref[...])
    o_ref[0] = jnp.dot(s, w_ref[0], precision=lax.Precision.HIGHEST,
                       preferred_element_type=F32) + b_ref[0]


def _modulation(c_all, w_mod, b_mod):
    depth, d, n = w_mod.shape
    rows = c_all.shape[0]
    nb = n // d
    return pl.pallas_call(
        _mod_kernel,
        grid=(depth, nb),
        in_specs=[pl.BlockSpec((rows, d), lambda l, j: (0, 0)),
                  pl.BlockSpec((1, d, d), lambda l, j: (l, 0, j)),
                  pl.BlockSpec((1, 1, d), lambda l, j: (l, 0, j))],
        out_specs=pl.BlockSpec((1, rows, d), lambda l, j: (l, 0, j)),
        out_shape=jax.ShapeDtypeStruct((depth, rows, n), F32),
        compiler_params=_cparams(("arbitrary", "arbitrary")),
        name="modulation",
    )(c_all, w_mod, b_mod.reshape(depth, 1, n))


def _head_sum_squares(x):
    lane = lax.broadcasted_iota(jnp.int32, (1, LANES), 1)
    low = lane < HEAD_DIM
    out = []
    for s in range(0, x.shape[1], LANES):
        sq = x[:, s:s + LANES] * x[:, s:s + LANES]
        s_low = jnp.sum(jnp.where(low, sq, 0.0), axis=-1, keepdims=True)
        s_high = jnp.sum(jnp.where(low, 0.0, sq), axis=-1, keepdims=True)
        out.append(jnp.where(low, s_low, s_high))
    return jnp.concatenate(out, axis=1)


def _head_norm_rope(qk, gain, cos, sin):
    width = qk.shape[1]
    y = qk * lax.rsqrt(_head_sum_squares(qk) * (1.0 / HEAD_DIM) + NORM_EPS) * gain
    lane = lax.broadcasted_iota(jnp.int32, (1, LANES), 1)
    first_half = (lane % ROPE_PAIR) < ROPE_FREQS
    out = []
    for s in range(0, width, LANES):
        ys = y[:, s:s + LANES]
        partner = jnp.where(first_half, pltpu.roll(ys, LANES - ROPE_FREQS, axis=1),
                            pltpu.roll(ys, ROPE_FREQS, axis=1))
        out.append(ys * cos + partner * sin)
    return jnp.concatenate(out, axis=1)


def _with_swapped(t):
    return jnp.concatenate([t, pltpu.roll(t, HEAD_DIM, axis=1)], axis=1).astype(BF16)


def _mix_out(a, b, c, w_ref):
    return jnp.dot(jnp.concatenate([a, b, c], axis=1), w_ref[...], preferred_element_type=F32)


def _proj_kernel(*refs, prev_out):
    if prev_out:
        gate_ref, a_ref, b_ref, c_ref, wo_ref = refs[:5]
        refs = refs[5:]
    (x_ref, mod_ref, ng_ref, w_ref, cos_ref, sin_ref, gq_a_ref, gq_c_ref) = refs[:8]
    outs = refs[8:]
    if prev_out:
        xo_ref, outs = outs[0], outs[1:]
    qa_ref, ka_ref, va_ref, ga_ref, hy_ref, qc_ref, kc_ref, vc_ref, gc_ref = outs
    shift = mod_ref[0, :, 0:D_MODEL]
    scale = mod_ref[0, :, D_MODEL:2 * D_MODEL]
    tm = x_ref.shape[1]
    sub = min(tm, PROJ_SUB)
    for r0 in range(0, tm, sub):
        rows = slice(r0, r0 + sub)
        x = x_ref[0, rows]
        if prev_out:
            x = x + gate_ref[0] * _mix_out(a_ref[0, rows], b_ref[0, rows], c_ref[0, rows], wo_ref)
            xo_ref[0, rows] = x
        ms = jnp.mean(x * x, axis=-1, keepdims=True)
        y = x * lax.rsqrt(ms + NORM_EPS) * ng_ref[...]
        h = (y * (1.0 + scale) + shift).astype(BF16)
        cos = cos_ref[rows]
        sin = sin_ref[rows]

        def attn_branch(col0, gain_ref, q_ref, k_ref, v_ref, g_ref):
            acc = jnp.dot(h, w_ref[:, col0:col0 + D_MODEL], preferred_element_type=F32)
            qk = _head_norm_rope(acc[:, 0:Q_W + KV_W], gain_ref[...], cos, sin)
            q_ref[0, rows] = (qk[:, 0:Q_W] * Q_SCALE).astype(BF16)
            k_ref[0, rows] = _with_swapped(qk[:, Q_W:Q_W + KV_W])
            v_ref[0, rows] = _with_swapped(acc[:, Q_W + KV_W:Q_W + 2 * KV_W])
            g_ref[0, rows] = _silu(acc[:, Q_W + 2 * KV_W:]).astype(BF16)

        attn_branch(0, gq_a_ref, qa_ref, ka_ref, va_ref, ga_ref)
        hy_ref[0, rows] = jnp.dot(h, w_ref[:, D_MODEL:2 * D_MODEL],
                                  preferred_element_type=F32).astype(BF16)
        attn_branch(2 * D_MODEL, gq_c_ref, qc_ref, kc_ref, vc_ref, gc_ref)


def _projection(x, mod, norm_g, w_in, cos, sin, gq_a, gq_c, tm, prev=None):
    b, t, d = x.shape
    tok = lambda w: pl.BlockSpec((1, tm, w), lambda i, j: (i, j, 0))
    once = lambda a: pl.BlockSpec(a.shape, lambda i, j: (0,) * a.ndim,
                                  pipeline_mode=pl.Buffered(1))
    per_batch = lambda a: pl.BlockSpec((1, 1, a.shape[2]), (lambda i, j: (i, 0, 0))
                                       if a.shape[0] > 1 else (lambda i, j: (0, 0, 0)))
    widths = (Q_W, 2 * LANES, 2 * LANES, Q_W, D_MODEL, Q_W, 2 * LANES, 2 * LANES, Q_W)
    args = [x, mod, norm_g, w_in, cos, sin, gq_a, gq_c]
    in_specs = [tok(d), per_batch(mod), once(norm_g), once(w_in),
                pl.BlockSpec((tm, LANES), lambda i, j: (j, 0)),
                pl.BlockSpec((tm, LANES), lambda i, j: (j, 0)),
                once(gq_a), once(gq_c)]
    out_specs = [tok(w) for w in widths]
    out_shape = [jax.ShapeDtypeStruct((b, t, w), BF16) for w in widths]
    if prev is not None:
        gate, a, bmix, c, w_out = prev
        args = [gate, a, bmix, c, w_out] + args
        in_specs = [per_batch(gate), tok(Q_W), tok(HY_W), tok(Q_W), once(w_out)] + in_specs
        out_specs = [tok(d)] + out_specs
        out_shape = [jax.ShapeDtypeStruct((b, t, d), F32)] + out_shape
    return pl.pallas_call(
        functools.partial(_proj_kernel, prev_out=prev is not None),
        grid=(b, t // tm),
        in_specs=in_specs,
        out_specs=out_specs,
        out_shape=out_shape,
        compiler_params=_cparams(("parallel", "arbitrary")),
        name="in_projection",
    )(*args)


def _proj_kv_kernel(x_ref, mod_ref, ng_ref, w_ref, gk_ref, ka_ref, va_ref, kc_ref, vc_ref):
    x = x_ref[0]
    ms = jnp.mean(x * x, axis=-1, keepdims=True)
    y = x * lax.rsqrt(ms + NORM_EPS) * ng_ref[...]
    h = (y * (1.0 + mod_ref[0, :, D_MODEL:2 * D_MODEL]) + mod_ref[0, :, 0:D_MODEL]).astype(BF16)
    acc = jnp.dot(h, w_ref[...], preferred_element_type=F32)
    k = jnp.concatenate([acc[:, 0:KV_W], acc[:, 2 * KV_W:3 * KV_W]], axis=1)
    k = k * lax.rsqrt(_head_sum_squares(k) * (1.0 / HEAD_DIM) + NORM_EPS) * gk_ref[...]
    ka_ref[0] = _with_swapped(k[:, 0:KV_W])
    kc_ref[0] = _with_swapped(k[:, KV_W:])
    va_ref[0] = _with_swapped(acc[:, KV_W:2 * KV_W])
    vc_ref[0] = _with_swapped(acc[:, 3 * KV_W:])


def _kv_projection(x, mod, norm_g, w_kv, gk, tm):
    b, t, d = x.shape
    tok = lambda w: pl.BlockSpec((1, tm, w), lambda i, j: (i, j, 0))
    full = lambda a: pl.BlockSpec(a.shape, lambda i, j: (0,) * a.ndim)
    return pl.pallas_call(
        _proj_kv_kernel,
        grid=(b, t // tm),
        in_specs=[tok(d), full(mod), full(norm_g), full(w_kv), full(gk)],
        out_specs=[tok(2 * LANES)] * 4,
        out_shape=[jax.ShapeDtypeStruct((b, t, 2 * LANES), BF16)] * 4,
        compiler_params=_cparams(("parallel", "arbitrary")),
        name="kv_projection",
    )(x, mod, norm_g, w_kv, gk)


def _attn_kernel(*refs, tq, n_sub, lat_mode, has_sink, n_ctx, lat_len):
    refs = list(refs)
    sink_ref = refs.pop(0) if has_sink else None
    q_ref, g_ref, kc_ref, vc_ref = refs[:4]
    refs = refs[4:]
    if lat_mode is not None:
        kl_ref, vl_ref = refs[:2]
        refs = refs[2:]
    o_ref, kpad_s, vaug_s = refs
    i = pl.program_id(1)

    @pl.when(i == 0)
    def _stage_keys():
        lo = lax.broadcasted_iota(jnp.int32, (1, LANES), 1) < HEAD_DIM

        def fill(row0, n, k_ref, v_ref):
            rows = slice(row0, row0 + n)
            for dst, src, width in ((kpad_s, k_ref, LANES), (vaug_s, v_ref, 2 * LANES)):
                t = src[0, :, 0:LANES]
                sw = src[0, :, LANES:2 * LANES]
                zero = jnp.zeros_like(t)
                for blk, val in enumerate((jnp.where(lo, t, zero), jnp.where(lo, zero, sw),
                                           jnp.where(lo, sw, zero), jnp.where(lo, zero, t))):
                    dst[rows, blk * width:blk * width + LANES] = val

        fill(0, n_ctx, kc_ref, vc_ref)
        if lat_mode is not None:
            fill(n_ctx, lat_len, kl_ref, vl_ref)

    @pl.when((i == 0) & (pl.program_id(0) == 0))
    def _stage_ones():
        for blk in range(2 * N_KV):
            vaug_s[:, (2 * blk + 1) * LANES:(2 * blk + 2) * LANES] = jnp.ones(
                (vaug_s.shape[0], LANES), BF16)

    n_keys = n_ctx + lat_len
    for sub, pair in [(s_, p_) for s_ in range(n_sub) for p_ in range(N_HEADS // 2)]:
        q_rows = slice(sub * tq, (sub + 1) * tq)
        if lat_mode == "window":
            if pair == 0:
                span = tq + 2 * WINDOW
                q0 = (i * n_sub + sub) * tq
                start = jnp.clip(q0 - WINDOW, 0, lat_len - span)
                qpos = q0 + lax.broadcasted_iota(jnp.int32, (tq, span), 0)
                kpos = start + lax.broadcasted_iota(jnp.int32, (tq, span), 1)
                valid = jnp.abs(qpos - kpos) <= WINDOW
                band = pl.ds(pl.multiple_of(n_ctx + start, WINDOW), span)
            key_rows = [(slice(0, n_ctx), None), (band, valid)]
        else:
            key_rows = [(slice(r, min(r + KEY_CHUNK, n_keys)), None)
                        for r in range(0, n_keys, KEY_CHUNK)]
        lanes = slice(pair * LANES, (pair + 1) * LANES)
        qp = q_ref[0, q_rows, lanes]
        acc = jnp.zeros((tq, LANES), F32)
        for parity in range(2):
            head = 2 * pair + parity
            blk = 2 * (head // (N_HEADS // N_KV)) + parity
            kcols = slice(blk * LANES, (blk + 1) * LANES)
            vcols = slice(2 * blk * LANES, 2 * (blk + 1) * LANES)
            scores = []
            for rows, mask in key_rows:
                s = lax.dot_general(qp, kpad_s[rows, kcols], _NT, preferred_element_type=F32)
                scores.append(s if mask is None else jnp.where(mask, s, NEG_INF))
            m = functools.reduce(jnp.maximum,
                                 [jnp.max(s, axis=-1, keepdims=True) for s in scores])
            if has_sink:
                sink = sink_ref[head] * LOG2E
                m = jnp.maximum(m, sink)
            o = functools.reduce(jnp.add, [
                jnp.dot(jnp.exp2(s - m).astype(BF16), vaug_s[rows, vcols],
                        preferred_element_type=F32)
                for s, (rows, _) in zip(scores, key_rows)])
            den = o[:, LANES:]
            if has_sink:
                den = den + jnp.exp2(sink - m)
            acc = acc + o[:, :LANES] / den
        o_ref[0, q_rows, lanes] = (acc * g_ref[0, q_rows, lanes].astype(F32)).astype(BF16)


def _attention(q, g, k_ctx, v_ctx, k_lat, v_lat, sink, lat_mode, tq, n_sub, name):
    b, t, _ = q.shape
    n_ctx = k_ctx.shape[1]
    has_sink = sink is not None
    tok = pl.BlockSpec((1, n_sub * tq, Q_W), lambda i, j: (i, j, 0))
    whole = lambda n: pl.BlockSpec((1, n, 2 * LANES), lambda i, j: (i, 0, 0))
    args, specs = [], []
    if has_sink:
        args.append(sink)
        specs.append(pl.BlockSpec(memory_space=pltpu.SMEM))
    args += [q, g, k_ctx, v_ctx]
    specs += [tok, tok, whole(n_ctx), whole(n_ctx)]
    lat_len = 0
    if lat_mode is not None:
        lat_len = k_lat.shape[1]
        args += [k_lat, v_lat]
        specs += [whole(lat_len), whole(lat_len)]
    n_keys = n_ctx + lat_len
    return pl.pallas_call(
        functools.partial(_attn_kernel, tq=tq, n_sub=n_sub, lat_mode=lat_mode,
                          has_sink=has_sink, n_ctx=n_ctx, lat_len=lat_len),
        grid=(b, t // (n_sub * tq)),
        in_specs=specs,
        out_specs=tok,
        out_shape=jax.ShapeDtypeStruct((b, t, Q_W), BF16),
        scratch_shapes=[pltpu.VMEM((n_keys, 2 * N_KV * LANES), BF16),
                        pltpu.VMEM((n_keys, 4 * N_KV * LANES), BF16)],
        compiler_params=_cparams(("arbitrary", "arbitrary")),
        name=name,
    )(*args)


def _filter_kernel(z_ref, w1_ref, b1_ref, w2_ref, b2_ref, w3_ref, fr_ref, dl_ref, o_ref, *, lf):
    hp = lax.Precision.HIGHEST
    z = z_ref[...]
    h = jnp.sin(fr_ref[0:1, :] * (jnp.dot(z, w1_ref[...], precision=hp,
                                          preferred_element_type=F32) + b1_ref[...]))
    h = jnp.sin(fr_ref[1:2, :] * (jnp.dot(h, w2_ref[...], precision=hp,
                                          preferred_element_type=F32) + b2_ref[...]))
    decay = jnp.exp(-z[:, 0:1] * dl_ref[...])
    first_row = lax.broadcasted_iota(jnp.int32, decay.shape, 0) == 0
    for order in range(2):
        c0 = 2 * order * HY_W
        bwd = jnp.dot(h[:lf], w3_ref[:, c0 + HY_W:c0 + 2 * HY_W], precision=hp,
                      preferred_element_type=F32)
        fwd = jnp.dot(h[lf:], w3_ref[:, c0:c0 + HY_W], precision=hp,
                      preferred_element_type=F32)
        taps = jnp.where(first_row, 0.0, jnp.concatenate([bwd, fwd], axis=0) * decay)
        taps = taps / jnp.sum(jnp.abs(taps), axis=0, keepdims=True)
        t = taps.T.astype(BF16).astype(F32)
        lo = pltpu.bitcast(t, jnp.uint32) >> 16
        hi = pltpu.bitcast(pltpu.roll(t, 1, axis=1), jnp.uint32) & jnp.uint32(0xFFFF0000)
        o_ref[order] = pltpu.bitcast(lo | hi, jnp.int32)


def _hyena_filters(lf, w1, b1, w2, b2, w3, freq):
    t = jnp.linspace(0.0, 1.0, lf, dtype=F32)[:, None]
    bands = jnp.linspace(1e-4, HY_BANDS - 1, HY_BANDS, dtype=F32)
    w = 2.0 * math.pi * jnp.arange(lf, dtype=F32)[:, None] / lf
    z = jnp.concatenate([t, jnp.cos(bands * w), jnp.sin(bands * w)], axis=-1)
    z = jnp.concatenate([z[:1], z[:0:-1], z], axis=0)
    z = jnp.pad(z, ((0, 0), (0, FEAT_PAD - z.shape[1])))
    w1p = jnp.pad(w1, ((0, FEAT_PAD - w1.shape[0]), (0, 0)))
    min_decay = math.log(HY_TARGET) / HY_SLOW_DECAY
    max_decay = math.log(HY_TARGET) / HY_FAST_DECAY
    deltas = jnp.abs(jnp.linspace(min_decay, max_decay, HY_W, dtype=F32))[None, :]
    full = lambda a: pl.BlockSpec(a.shape, lambda o: (0,) * a.ndim)
    ins = (z, w1p, b1[None, :], w2, b2[None, :], w3, freq, deltas)
    return pl.pallas_call(
        functools.partial(_filter_kernel, lf=lf),
        grid=(1,),
        in_specs=[full(a) for a in ins],
        out_specs=pl.BlockSpec((2, HY_W, 2 * lf), lambda o: (0, 0, 0)),
        out_shape=jax.ShapeDtypeStruct((2, HY_W, 2 * lf), jnp.int32),
        compiler_params=_cparams(("arbitrary",)),
        name="hyena_filter",
    )(*ins)


def _hyena_kernel(cw_ref, cb_ref, db_ref, v_ref, x1_ref, x2_ref, g_ref, taps_ref, shift_ref,
                  o_ref, *, cw, n_blk, batch):
    rows = n_blk * batch
    c0 = pl.program_id(0) * cw
    lane = lax.broadcasted_iota(jnp.int32, (1, TOEP), 1)
    zero_blk = jnp.zeros((batch, TOEP), F32)

    def short_conv(p_b, ch):
        shifted = jnp.dot(p_b, shift_ref[...], preferred_element_type=F32)
        p = p_b.astype(F32)
        prev = shifted[:, :TOEP]
        nxt = shifted[:, TOEP:]
        if n_blk > 1:
            prev_edge = jnp.concatenate([zero_blk, prev[:rows - batch]], axis=0)
            next_edge = jnp.concatenate([nxt[batch:], zero_blk], axis=0)
        else:
            prev_edge = jnp.zeros_like(p)
            next_edge = jnp.zeros_like(p)
        prev = jnp.where(lane == 0, prev_edge, prev)
        nxt = jnp.where(lane == TOEP - 1, next_edge, nxt)
        return cb_ref[ch] + cw_ref[0, ch] * prev + cw_ref[1, ch] * p + cw_ref[2, ch] * nxt

    def long_conv(z, order, ci):
        words = taps_ref[order, ci]
        skew = pltpu.roll(jnp.broadcast_to(words, (TOEP // 2, words.shape[1])), 0, axis=1,
                          stride=2, stride_axis=0)
        skew = pltpu.bitcast(skew, BF16)
        zb = z.astype(BF16)
        acc = [jnp.zeros((batch, TOEP), F32) for _ in range(n_blk)]
        for d in range(-(n_blk - 1), n_blk):
            n_out = n_blk - abs(d)
            src = max(0, -d) * batch
            col = (n_blk + d) * TOEP
            part = jnp.dot(zb[src:src + n_out * batch], skew[:, col:col + TOEP],
                           preferred_element_type=F32)
            for k in range(n_out):
                acc[max(0, d) + k] = acc[max(0, d) + k] + part[k * batch:(k + 1) * batch]
        return jnp.concatenate(acc, axis=0) if n_blk > 1 else acc[0]

    def body(k, carry):
        cis = [k * HY_UNROLL + u for u in range(HY_UNROLL)]
        zs = [short_conv(v_ref[ci], c0 + ci) for ci in cis]
        for order, x_ref in enumerate((x1_ref, x2_ref)):
            xs = [short_conv(x_ref[ci], (order + 1) * HY_W + c0 + ci) for ci in cis]
            ys = [long_conv(z, order, ci) for z, ci in zip(zs, cis)]
            zs = [x * (y + z * db_ref[order, c0 + ci]) for x, y, z, ci in zip(xs, ys, zs, cis)]
        for z, ci in zip(zs, cis):
            o_ref[ci] = (z * _silu(g_ref[ci].astype(F32))).astype(o_ref.dtype)
        return carry

    lax.fori_loop(0, cw // HY_UNROLL, body, 0)


def _hyena(hy_t, taps, conv_w, conv_b, d_bias, n_blk, batch, cw=HY_CW):
    rows = n_blk * batch
    smem = pl.BlockSpec(memory_space=pltpu.SMEM)
    slab = lambda off: pl.BlockSpec((cw, rows, TOEP), lambda c: (off // cw + c, 0, 0))
    taps4 = taps.reshape(2, HY_W, 1, taps.shape[-1])
    pos = jnp.arange(TOEP)
    shift = jnp.concatenate([pos[:, None] == (pos[None, :] - 1) % TOEP,
                             pos[:, None] == (pos[None, :] + 1) % TOEP], axis=1).astype(BF16)
    return pl.pallas_call(
        functools.partial(_hyena_kernel, cw=cw, n_blk=n_blk, batch=batch),
        grid=(HY_W // cw,),
        in_specs=[smem, smem, smem, slab(0), slab(HY_W), slab(2 * HY_W), slab(3 * HY_W),
                  pl.BlockSpec((2, cw, 1, taps.shape[-1]), lambda c: (0, c, 0, 0)),
                  pl.BlockSpec((TOEP, 2 * TOEP), lambda c: (0, 0))],
        out_specs=pl.BlockSpec((cw, rows, TOEP), lambda c: (c, 0, 0)),
        out_shape=jax.ShapeDtypeStruct((HY_W, rows, TOEP), BF16),
        compiler_params=_cparams(("arbitrary",)),
        name="hyena_mixer",
    )(conv_w, conv_b, d_bias, hy_t, hy_t, hy_t, hy_t, taps4, shift)


def _out_kernel(x_ref, gate_ref, a_ref, b_ref, c_ref, w_ref, o_ref):
    o_ref[0] = x_ref[0] + gate_ref[0] * _mix_out(a_ref[0], b_ref[0], c_ref[0], w_ref)


def _out_projection(x, gate, a, bmix, c, w_out, tm):
    b, t, d = x.shape
    per_batch = gate.shape[0] > 1
    tok = lambda w: pl.BlockSpec((1, tm, w), lambda i, j: (i, j, 0))
    return pl.pallas_call(
        _out_kernel,
        grid=(b, t // tm),
        in_specs=[tok(d),
                  pl.BlockSpec((1, 1, d), (lambda i, j: (i, 0, 0)) if per_batch
                               else (lambda i, j: (0, 0, 0))),
                  tok(Q_W), tok(HY_W), tok(Q_W),
                  pl.BlockSpec(w_out.shape, lambda i, j: (0, 0))],
        out_specs=tok(d),
        out_shape=jax.ShapeDtypeStruct((b, t, d), F32),
        compiler_params=_cparams(("parallel", "arbitrary")),
        name="out_projection",
    )(x, gate, a, bmix, c, w_out)


def _rope_tables(t_len):
    pos = jnp.arange(t_len)
    n_freq = ROPE_FREQS
    inv_freq = ROPE_THETA ** (-jnp.arange(n_freq, dtype=F32) / n_freq)
    ang = jnp.stack([(pos // GRID_W).astype(F32)[:, None] * inv_freq,
                     (pos % GRID_W).astype(F32)[:, None] * inv_freq], axis=1)
    cos = jnp.cos(ang)[:, :, None, :]
    sin = jnp.sin(ang)[:, :, None, :]
    cos = jnp.broadcast_to(cos, (t_len, 2, 2, n_freq)).reshape(t_len, HEAD_DIM)
    sin = jnp.concatenate([-sin, sin], axis=2).reshape(t_len, HEAD_DIM)
    return jnp.tile(cos, (1, 2)), jnp.tile(sin, (1, 2))


def _to_channel_major(hy, n_blk):
    b, t, c = hy.shape
    return hy.reshape(b, n_blk, TOEP, c).transpose(3, 1, 0, 2).reshape(c, n_blk * b, TOEP)


def _to_token_major(y, n_blk, batch):
    c = y.shape[0]
    return y.reshape(c, n_blk, batch, TOEP).transpose(2, 1, 3, 0).reshape(batch, n_blk * TOEP, c)


def kernel(x, c, ctx, c_ctx, norm_g, w_mod, b_mod, w_in, w_out, qn_a, kn_a, qn_c, kn_c, sink_c,
           hy_conv_w, hy_conv_b, hy_w1, hy_b1, hy_w2, hy_b2, hy_w3, hy_freq, hy_bias):
    depth = w_in.shape[0]
    batch, seq, _ = x.shape
    n_ctx = ctx.shape[1]
    n_blk = seq // TOEP
    assert x.shape[2] == D_MODEL and seq % (ATTN_TQ * ATTN_TILES) == 0 and seq % GRID_W == 0
    assert seq % TOEP == 0 and n_ctx == TOEP and seq >= ATTN_TQ + 2 * WINDOW
    assert batch % 16 == 0 and HY_W % HY_CW == 0 and HY_CW % HY_UNROLL == 0

    rows = -(-(batch + 1) // 8) * 8
    c_all = jnp.concatenate([c, c_ctx[None]], axis=0)
    c_all = jnp.pad(c_all, ((0, rows - batch - 1), (0, 0)))
    mod = _modulation(c_all, w_mod, b_mod)

    cos_l, sin_l = _rope_tables(seq)
    grp = max(1, math.gcd(batch, PROJ_TM // n_ctx))
    fold = lambda t: t.reshape(batch // grp, grp * n_ctx, t.shape[-1])
    unfold = lambda t: t.reshape(batch, n_ctx, t.shape[-1])
    cos_c = jnp.ones((grp * n_ctx, LANES), F32)
    sin_c = jnp.zeros((grp * n_ctx, LANES), F32)
    w_in_b = w_in.astype(BF16)
    w_out_b = w_out.astype(BF16)

    pending = None
    for l in range(depth):
        last = l == depth - 1
        mod_x = mod[l, :batch, None, :]
        mod_c = mod[l, batch:batch + 1, None, :]
        ng = norm_g[l][None, :]
        gq_a = jnp.concatenate([jnp.tile(qn_a[l], N_HEADS), jnp.tile(kn_a[l], N_KV)])[None, :]
        gq_c = jnp.concatenate([jnp.tile(qn_c[l], N_HEADS), jnp.tile(kn_c[l], N_KV)])[None, :]

        outs = _projection(x, mod_x, ng, w_in_b[l], cos_l, sin_l, gq_a, gq_c,
                           tm=min(seq, PROJ_TM), prev=pending)
        if pending is not None:
            x, outs = outs[0], outs[1:]
        qa, ka, va, ga, hy, qc, kc, vc, gc = outs
        if last:
            c_off = 2 * D_MODEL + Q_W
            w_kv = jnp.concatenate([w_in_b[l][:, Q_W:Q_W + 2 * KV_W],
                                    w_in_b[l][:, c_off:c_off + 2 * KV_W]], axis=1)
            gk = jnp.concatenate([jnp.tile(kn_a[l], N_KV), jnp.tile(kn_c[l], N_KV)])[None, :]
            ka_c, va_c, kc_c, vc_c = map(unfold, _kv_projection(
                fold(ctx), mod_c, ng, w_kv, gk, tm=grp * n_ctx))
        else:
            qa_c, ka_c, va_c, ga_c, hy_c, qc_c, kc_c, vc_c, gc_c = map(unfold, _projection(
                fold(ctx), mod_c, ng, w_in_b[l], cos_c, sin_c, gq_a, gq_c, tm=grp * n_ctx))

        a_out = _attention(qa, ga, ka_c, va_c, ka, va, None, "full", ATTN_TQ, ATTN_TILES,
                           "attn_global")
        c_out = _attention(qc, gc, kc_c, vc_c, kc, vc, sink_c[l], "window", ATTN_TQ, ATTN_TILES,
                           "attn_window")
        taps = _hyena_filters(seq, hy_w1[l], hy_b1[l], hy_w2[l], hy_b2[l], hy_w3[l], hy_freq[l])
        b_out = _hyena(_to_channel_major(hy, n_blk), taps, hy_conv_w[l], hy_conv_b[l],
                       hy_bias[l], n_blk, batch)
        b_out = _to_token_major(b_out, n_blk, batch)
        pending = (mod_x[:, :, 2 * D_MODEL:], a_out, b_out, c_out, w_out_b[l])

        if not last:
            a_c = _attention(qa_c, ga_c, ka_c, va_c, None, None, None, None, n_ctx, 1,
                             "attn_ctx_a")
            c_c = _attention(qc_c, gc_c, kc_c, vc_c, None, None, sink_c[l], None, n_ctx, 1,
                             "attn_ctx_c")
            taps_c = _hyena_filters(n_ctx, hy_w1[l], hy_b1[l], hy_w2[l], hy_b2[l], hy_w3[l],
                                    hy_freq[l])
            b_c = _hyena(_to_channel_major(hy_c, 1), taps_c, hy_conv_w[l], hy_conv_b[l],
                         hy_bias[l], 1, batch)
            b_c = _to_token_major(b_c, 1, batch)
            ctx = unfold(_out_projection(fold(ctx), mod_c[:, :, 2 * D_MODEL:], fold(a_c),
                                         fold(b_c), fold(c_c), w_out_b[l], grp * n_ctx))
    gate, a_out, b_out, c_out, w_o = pending
    return _out_projection(x, gate, a_out, b_out, c_out, w_o, min(seq, PROJ_TM))
```

```python
import functools
import math

import jax
import jax.numpy as jnp
from jax import lax
from jax.experimental import pallas as pl
from jax.experimental.pallas import tpu as pltpu

F32 = jnp.float32
BF16 = jnp.bfloat16

D_MODEL = 1024
HEAD_DIM = 64
N_HEADS = 6
N_KV = 2
Q_W = N_HEADS * HEAD_DIM
KV_W = N_KV * HEAD_DIM
HY_W = 256
GRID_W = 64
WINDOW = 128
ROPE_THETA = 10000.0
NORM_EPS = 1e-6
NEG_INF = -1e30
HY_BANDS = 16
HY_FAST_DECAY = 0.3
HY_SLOW_DECAY = 1.5
HY_TARGET = 1e-2
LANES = 128
TOEP = 256
FEAT_PAD = 128
ROPE_FREQS = HEAD_DIM // 4
ROPE_PAIR = 2 * ROPE_FREQS

PROJ_TM = 1024
PROJ_SUB = 256
ATTN_TQ = 256
ATTN_TILES = 8
KEY_CHUNK = 1152
HY_CW = 8
HY_UNROLL = 4
VMEM_LIMIT = 56 * 1024 * 1024

_NT = (((1,), (1,)), ((), ()))
LOG2E = 1.4426950408889634
Q_SCALE = LOG2E / math.sqrt(HEAD_DIM)


def _cparams(sem):
    return pltpu.CompilerParams(dimension_semantics=sem, vmem_limit_bytes=VMEM_LIMIT)


def _silu(x):
    return x * (1.0 / (1.0 + jnp.exp(-x)))


def _mod_kernel(c_ref, w_ref, b_ref, o_ref):
    s = _silu(c_ref[...])
    o_ref[0] = jnp.dot(s, w_ref[0], precision=lax.Precision.HIGHEST,
                       preferred_element_type=F32) + b_ref[0]


def _modulation(c_all, w_mod, b_mod):
    depth, d, n = w_mod.shape
    rows = c_all.shape[0]
    nb = n // d
    return pl.pallas_call(
        _mod_kernel,
        grid=(depth, nb),
        in_specs=[pl.BlockSpec((rows, d), lambda l, j: (0, 0)),
                  pl.BlockSpec((1, d, d), lambda l, j: (l, 0, j)),
                  pl.BlockSpec((1, 1, d), lambda l, j: (l, 0, j))],
        out_specs=pl.BlockSpec((1, rows, d), lambda l, j: (l, 0, j)),
        out_shape=jax.ShapeDtypeStruct((depth, rows, n), F32),
        compiler_params=_cparams(("arbitrary", "arbitrary")),
        name="modulation",
    )(c_all, w_mod, b_mod.reshape(depth, 1, n))


def _head_sum_squares(x):
    lane = lax.broadcasted_iota(jnp.int32, (1, LANES), 1)
    low = lane < HEAD_DIM
    out = []
    for s in range(0, x.shape[1], LANES):
        sq = x[:, s:s + LANES] * x[:, s:s + LANES]
        s_low = jnp.sum(jnp.where(low, sq, 0.0), axis=-1, keepdims=True)
        s_high = jnp.sum(jnp.where(low, 0.0, sq), axis=-1, keepdims=True)
        out.append(jnp.where(low, s_low, s_high))
    return jnp.concatenate(out, axis=1)


def _head_norm_rope(qk, gain, cos, sin):
    width = qk.shape[1]
    y = qk * lax.rsqrt(_head_sum_squares(qk) * (1.0 / HEAD_DIM) + NORM_EPS) * gain
    lane = lax.broadcasted_iota(jnp.int32, (1, LANES), 1)
    first_half = (lane % ROPE_PAIR) < ROPE_FREQS
    out = []
    for s in range(0, width, LANES):
        ys = y[:, s:s + LANES]
        partner = jnp.where(first_half, pltpu.roll(ys, LANES - ROPE_FREQS, axis=1),
                            pltpu.roll(ys, ROPE_FREQS, axis=1))
        out.append(ys * cos + partner * sin)
    return jnp.concatenate(out, axis=1)


def _with_swapped(t):
    return jnp.concatenate([t, pltpu.roll(t, HEAD_DIM, axis=1)], axis=1).astype(BF16)


def _mix_out(a, b, c, w_ref):
    return jnp.dot(jnp.concatenate([a, b, c], axis=1), w_ref[...], preferred_element_type=F32)


def _proj_kernel(*refs, prev_out):
    if prev_out:
        gate_ref, a_ref, b_ref, c_ref, wo_ref = refs[:5]
        refs = refs[5:]
    (x_ref, mod_ref, ng_ref, w_ref, cos_ref, sin_ref, gq_a_ref, gq_c_ref) = refs[:8]
    outs = refs[8:]
    if prev_out:
        xo_ref, outs = outs[0], outs[1:]
    qa_ref, ka_ref, va_ref, ga_ref, hy_ref, qc_ref, kc_ref, vc_ref, gc_ref = outs
    shift = mod_ref[0, :, 0:D_MODEL]
    scale = mod_ref[0, :, D_MODEL:2 * D_MODEL]
    tm = x_ref.shape[1]
    sub = min(tm, PROJ_SUB)
    for r0 in range(0, tm, sub):
        rows = slice(r0, r0 + sub)
        x = x_ref[0, rows]
        if prev_out:
            x = x + gate_ref[0] * _mix_out(a_ref[0, rows], b_ref[0, rows], c_ref[0, rows], wo_ref)
            xo_ref[0, rows] = x
        ms = jnp.mean(x * x, axis=-1, keepdims=True)
        y = x * lax.rsqrt(ms + NORM_EPS) * ng_ref[...]
        h = (y * (1.0 + scale) + shift).astype(BF16)
        cos = cos_ref[rows]
        sin = sin_ref[rows]

        def attn_branch(col0, gain_ref, q_ref, k_ref, v_ref, g_ref):
            acc = jnp.dot(h, w_ref[:, col0:col0 + D_MODEL], preferred_element_type=F32)
            qk = _head_norm_rope(acc[:, 0:Q_W + KV_W], gain_ref[...], cos, sin)
            q_ref[0, rows] = (qk[:, 0:Q_W] * Q_SCALE).astype(BF16)
            k_ref[0, rows] = _with_swapped(qk[:, Q_W:Q_W + KV_W])
            v_ref[0, rows] = _with_swapped(acc[:, Q_W + KV_W:Q_W + 2 * KV_W])
            g_ref[0, rows] = _silu(acc[:, Q_W + 2 * KV_W:]).astype(BF16)

        attn_branch(0, gq_a_ref, qa_ref, ka_ref, va_ref, ga_ref)
        hy_ref[0, rows] = jnp.dot(h, w_ref[:, D_MODEL:2 * D_MODEL],
                                  preferred_element_type=F32).astype(BF16)
        attn_branch(2 * D_MODEL, gq_c_ref, qc_ref, kc_ref, vc_ref, gc_ref)


def _projection(x, mod, norm_g, w_in, cos, sin, gq_a, gq_c, tm, prev=None):
    b, t, d = x.shape
    tok = lambda w: pl.BlockSpec((1, tm, w), lambda i, j: (i, j, 0))
    once = lambda a: pl.BlockSpec(a.shape, lambda i, j: (0,) * a.ndim,
                                  pipeline_mode=pl.Buffered(1))
    per_batch = lambda a: pl.BlockSpec((1, 1, a.shape[2]), (lambda i, j: (i, 0, 0))
                                       if a.shape[0] > 1 else (lambda i, j: (0, 0, 0)))
    widths = (Q_W, 2 * LANES, 2 * LANES, Q_W, D_MODEL, Q_W, 2 * LANES, 2 * LANES, Q_W)
    args = [x, mod, norm_g, w_in, cos, sin, gq_a, gq_c]
    in_specs = [tok(d), per_batch(mod), once(norm_g), once(w_in),
                pl.BlockSpec((tm, LANES), lambda i, j: (j, 0)),
                pl.BlockSpec((tm, LANES), lambda i, j: (j, 0)),
                once(gq_a), once(gq_c)]
    out_specs = [tok(w) for w in widths]
    out_shape = [jax.ShapeDtypeStruct((b, t, w), BF16) for w in widths]
    if prev is not None:
        gate, a, bmix, c, w_out = prev
        args = [gate, a, bmix, c, w_out] + args
        in_specs = [per_batch(gate), tok(Q_W), tok(HY_W), tok(Q_W), once(w_out)] + in_specs
        out_specs = [tok(d)] + out_specs
        out_shape = [jax.ShapeDtypeStruct((b, t, d), F32)] + out_shape
    return pl.pallas_call(
        functools.partial(_proj_kernel, prev_out=prev is not None),
        grid=(b, t // tm),
        in_specs=in_specs,
        out_specs=out_specs,
        out_shape=out_shape,
        compiler_params=_cparams(("parallel", "arbitrary")),
        name="in_projection",
    )(*args)


def _proj_kv_kernel(x_ref, mod_ref, ng_ref, w_ref, gk_ref, ka_ref, va_ref, kc_ref, vc_ref):
    x = x_ref[0]
    ms = jnp.mean(x * x, axis=-1, keepdims=True)
    y = x * lax.rsqrt(ms + NORM_EPS) * ng_ref[...]
    h = (y * (1.0 + mod_ref[0, :, D_MODEL:2 * D_MODEL]) + mod_ref[0, :, 0:D_MODEL]).astype(BF16)
    acc = jnp.dot(h, w_ref[...], preferred_element_type=F32)
    k = jnp.concatenate([acc[:, 0:KV_W], acc[:, 2 * KV_W:3 * KV_W]], axis=1)
    k = k * lax.rsqrt(_head_sum_squares(k) * (1.0 / HEAD_DIM) + NORM_EPS) * gk_ref[...]
    ka_ref[0] = _with_swapped(k[:, 0:KV_W])
    kc_ref[0] = _with_swapped(k[:, KV_W:])
    va_ref[0] = _with_swapped(acc[:, KV_W:2 * KV_W])
    vc_ref[0] = _with_swapped(acc[:, 3 * KV_W:])


def _kv_projection(x, mod, norm_g, w_kv, gk, tm):
    b, t, d = x.shape
    tok = lambda w: pl.BlockSpec((1, tm, w), lambda i, j: (i, j, 0))
    full = lambda a: pl.BlockSpec(a.shape, lambda i, j: (0,) * a.ndim)
    return pl.pallas_call(
        _proj_kv_kernel,
        grid=(b, t // tm),
        in_specs=[tok(d), full(mod), full(norm_g), full(w_kv), full(gk)],
        out_specs=[tok(2 * LANES)] * 4,
        out_shape=[jax.ShapeDtypeStruct((b, t, 2 * LANES), BF16)] * 4,
        compiler_params=_cparams(("parallel", "arbitrary")),
        name="kv_projection",
    )(x, mod, norm_g, w_kv, gk)


def _attn_kernel(*refs, tq, n_sub, lat_mode, has_sink, n_ctx, lat_len):
    refs = list(refs)
    sink_ref = refs.pop(0) if has_sink else None
    q_ref, g_ref, kc_ref, vc_ref = refs[:4]
    refs = refs[4:]
    if lat_mode is not None:
        kl_ref, vl_ref = refs[:2]
        refs = refs[2:]
    o_ref, kpad_s, vaug_s = refs
    i = pl.program_id(1)

    @pl.when(i == 0)
    def _stage_keys():
        lo = lax.broadcasted_iota(jnp.int32, (1, LANES), 1) < HEAD_DIM

        def fill(row0, n, k_ref, v_ref):
            rows = slice(row0, row0 + n)
            for dst, src, width in ((kpad_s, k_ref, LANES), (vaug_s, v_ref, 2 * LANES)):
                t = src[0, :, 0:LANES]
                sw = src[0, :, LANES:2 * LANES]
                zero = jnp.zeros_like(t)
                for blk, val in enumerate((jnp.where(lo, t, zero), jnp.where(lo, zero, sw),
                                           jnp.where(lo, sw, zero), jnp.where(lo, zero, t))):
                    dst[rows, blk * width:blk * width + LANES] = val

        fill(0, n_ctx, kc_ref, vc_ref)
        if lat_mode is not None:
            fill(n_ctx, lat_len, kl_ref, vl_ref)

    @pl.when((i == 0) & (pl.program_id(0) == 0))
    def _stage_ones():
        for blk in range(2 * N_KV):
            vaug_s[:, (2 * blk + 1) * LANES:(2 * blk + 2) * LANES] = jnp.ones(
                (vaug_s.shape[0], LANES), BF16)

    n_keys = n_ctx + lat_len
    for sub, pair in [(s_, p_) for s_ in range(n_sub) for p_ in range(N_HEADS // 2)]:
        q_rows = slice(sub * tq, (sub + 1) * tq)
        if lat_mode == "window":
            if pair == 0:
                span = tq + 2 * WINDOW
                q0 = (i * n_sub + sub) * tq
                start = jnp.clip(q0 - WINDOW, 0, lat_len - span)
                qpos = q0 + lax.broadcasted_iota(jnp.int32, (tq, span), 0)
                kpos = start + lax.broadcasted_iota(jnp.int32, (tq, span), 1)
                valid = jnp.abs(qpos - kpos) <= WINDOW
                band = pl.ds(pl.multiple_of(n_ctx + start, WINDOW), span)
            key_rows = [(slice(0, n_ctx), None), (band, valid)]
        else:
            key_rows = [(slice(r, min(r + KEY_CHUNK, n_keys)), None)
                        for r in range(0, n_keys, KEY_CHUNK)]
        lanes = slice(pair * LANES, (pair + 1) * LANES)
        qp = q_ref[0, q_rows, lanes]
        acc = jnp.zeros((tq, LANES), F32)
        for parity in range(2):
            head = 2 * pair + parity
            blk = 2 * (head // (N_HEADS // N_KV)) + parity
            kcols = slice(blk * LANES, (blk + 1) * LANES)
            vcols = slice(2 * blk * LANES, 2 * (blk + 1) * LANES)
            scores = []
            for rows, mask in key_rows:
                s = lax.dot_general(qp, kpad_s[rows, kcols], _NT, preferred_element_type=F32)
                scores.append(s if mask is None else jnp.where(mask, s, NEG_INF))
            m = functools.reduce(jnp.maximum,
                                 [jnp.max(s, axis=-1, keepdims=True) for s in scores])
            if has_sink:
                sink = sink_ref[head] * LOG2E
                m = jnp.maximum(m, sink)
            o = functools.reduce(jnp.add, [
                jnp.dot(jnp.exp2(s - m).astype(BF16), vaug_s[rows, vcols],
                        preferred_element_type=F32)
                for s, (rows, _) in zip(scores, key_rows)])
            den = o[:, LANES:]
            if has_sink:
                den = den + jnp.exp2(sink - m)
            acc = acc + o[:, :LANES] / den
        o_ref[0, q_rows, lanes] = (acc * g_ref[0, q_rows, lanes].astype(F32)).astype(BF16)


def _attention(q, g, k_ctx, v_ctx, k_lat, v_lat, sink, lat_mode, tq, n_sub, name):
    b, t, _ = q.shape
    n_ctx = k_ctx.shape[1]
    has_sink = sink is not None
    tok = pl.BlockSpec((1, n_sub * tq, Q_W), lambda i, j: (i, j, 0))
    whole = lambda n: pl.BlockSpec((1, n, 2 * LANES), lambda i, j: (i, 0, 0))
    args, specs = [], []
    if has_sink:
        args.append(sink)
        specs.append(pl.BlockSpec(memory_space=pltpu.SMEM))
    args += [q, g, k_ctx, v_ctx]
    specs += [tok, tok, whole(n_ctx), whole(n_ctx)]
    lat_len = 0
    if lat_mode is not None:
        lat_len = k_lat.shape[1]
        args += [k_lat, v_lat]
        specs += [whole(lat_len), whole(lat_len)]
    n_keys = n_ctx + lat_len
    return pl.pallas_call(
        functools.partial(_attn_kernel, tq=tq, n_sub=n_sub, lat_mode=lat_mode,
                          has_sink=has_sink, n_ctx=n_ctx, lat_len=lat_len),
        grid=(b, t // (n_sub * tq)),
        in_specs=specs,
        out_specs=tok,
        out_shape=jax.ShapeDtypeStruct((b, t, Q_W), BF16),
        scratch_shapes=[pltpu.VMEM((n_keys, 2 * N_KV * LANES), BF16),
                        pltpu.VMEM((n_keys, 4 * N_KV * LANES), BF16)],
        compiler_params=_cparams(("arbitrary", "arbitrary")),
        name=name,
    )(*args)


def _filter_kernel(z_ref, w1_ref, b1_ref, w2_ref, b2_ref, w3_ref, fr_ref, dl_ref, o_ref, *, lf):
    hp = lax.Precision.HIGHEST
    z = z_ref[...]
    h = jnp.sin(fr_ref[0:1, :] * (jnp.dot(z, w1_ref[...], precision=hp,
                                          preferred_element_type=F32) + b1_ref[...]))
    h = jnp.sin(fr_ref[1:2, :] * (jnp.dot(h, w2_ref[...], precision=hp,
                                          preferred_element_type=F32) + b2_ref[...]))
    decay = jnp.exp(-z[:, 0:1] * dl_ref[...])
    first_row = lax.broadcasted_iota(jnp.int32, decay.shape, 0) == 0
    for order in range(2):
        c0 = 2 * order * HY_W
        bwd = jnp.dot(h[:lf], w3_ref[:, c0 + HY_W:c0 + 2 * HY_W], precision=hp,
                      preferred_element_type=F32)
        fwd = jnp.dot(h[lf:], w3_ref[:, c0:c0 + HY_W], precision=hp,
                      preferred_element_type=F32)
        taps = jnp.where(first_row, 0.0, jnp.concatenate([bwd, fwd], axis=0) * decay)
        taps = taps / jnp.sum(jnp.abs(taps), axis=0, keepdims=True)
        t = taps.T.astype(BF16).astype(F32)
        lo = pltpu.bitcast(t, jnp.uint32) >> 16
        hi = pltpu.bitcast(pltpu.roll(t, 1, axis=1), jnp.uint32) & jnp.uint32(0xFFFF0000)
        o_ref[order] = pltpu.bitcast(lo | hi, jnp.int32)


def _hyena_filters(lf, w1, b1, w2, b2, w3, freq):
    t = jnp.linspace(0.0, 1.0, lf, dtype=F32)[:, None]
    bands = jnp.linspace(1e-4, HY_BANDS - 1, HY_BANDS, dtype=F32)
    w = 2.0 * math.pi * jnp.arange(lf, dtype=F32)[:, None] / lf
    z = jnp.concatenate([t, jnp.cos(bands * w), jnp.sin(bands * w)], axis=-1)
    z = jnp.concatenate([z[:1], z[:0:-1], z], axis=0)
    z = jnp.pad(z, ((0, 0), (0, FEAT_PAD - z.shape[1])))
    w1p = jnp.pad(w1, ((0, FEAT_PAD - w1.shape[0]), (0, 0)))
    min_decay = math.log(HY_TARGET) / HY_SLOW_DECAY
    max_decay = math.log(HY_TARGET) / HY_FAST_DECAY
    deltas = jnp.abs(jnp.linspace(min_decay, max_decay, HY_W, dtype=F32))[None, :]
    full = lambda a: pl.BlockSpec(a.shape, lambda o: (0,) * a.ndim)
    ins = (z, w1p, b1[None, :], w2, b2[None, :], w3, freq, deltas)
    return pl.pallas_call(
        functools.partial(_filter_kernel, lf=lf),
        grid=(1,),
        in_specs=[full(a) for a in ins],
        out_specs=pl.BlockSpec((2, HY_W, 2 * lf), lambda o: (0, 0, 0)),
        out_shape=jax.ShapeDtypeStruct((2, HY_W, 2 * lf), jnp.int32),
        compiler_params=_cparams(("arbitrary",)),
        name="hyena_filter",
    )(*ins)


def _hyena_kernel(cw_ref, cb_ref, db_ref, v_ref, x1_ref, x2_ref, g_ref, taps_ref, shift_ref,
                  o_ref, *, cw, n_blk, batch):
    rows = n_blk * batch
    c0 = pl.program_id(0) * cw
    lane = lax.broadcasted_iota(jnp.int32, (1, TOEP), 1)
    zero_blk = jnp.zeros((batch, TOEP), F32)

    def short_conv(p_b, ch):
        shifted = jnp.dot(p_b, shift_ref[...], preferred_element_type=F32)
        p = p_b.astype(F32)
        prev = shifted[:, :TOEP]
        nxt = shifted[:, TOEP:]
        if n_blk > 1:
            prev_edge = jnp.concatenate([zero_blk, prev[:rows - batch]], axis=0)
            next_edge = jnp.concatenate([nxt[batch:], zero_blk], axis=0)
        else:
            prev_edge = jnp.zeros_like(p)
            next_edge = jnp.zeros_like(p)
        prev = jnp.where(lane == 0, prev_edge, prev)
        nxt = jnp.where(lane == TOEP - 1, next_edge, nxt)
        return cb_ref[ch] + cw_ref[0, ch] * prev + cw_ref[1, ch] * p + cw_ref[2, ch] * nxt

    def long_conv(z, order, ci):
        words = taps_ref[order, ci]
        skew = pltpu.roll(jnp.broadcast_to(words, (TOEP // 2, words.shape[1])), 0, axis=1,
                          stride=2, stride_axis=0)
        skew = pltpu.bitcast(skew, BF16)
        zb = z.astype(BF16)
        acc = [jnp.zeros((batch, TOEP), F32) for _ in range(n_blk)]
        for d in range(-(n_blk - 1), n_blk):
            n_out = n_blk - abs(d)
            src = max(0, -d) * batch
            col = (n_blk + d) * TOEP
            part = jnp.dot(zb[src:src + n_out * batch], skew[:, col:col + TOEP],
                           preferred_element_type=F32)
            for k in range(n_out):
                acc[max(0, d) + k] = acc[max(0, d) + k] + part[k * batch:(k + 1) * batch]
        return jnp.concatenate(acc, axis=0) if n_blk > 1 else acc[0]

    def body(k, carry):
        cis = [k * HY_UNROLL + u for u in range(HY_UNROLL)]
        zs = [short_conv(v_ref[ci], c0 + ci) for ci in cis]
        for order, x_ref in enumerate((x1_ref, x2_ref)):
            xs = [short_conv(x_ref[ci], (order + 1) * HY_W + c0 + ci) for ci in cis]
            ys = [long_conv(z, order, ci) for z, ci in zip(zs, cis)]
            zs = [x * (y + z * db_ref[order, c0 + ci]) for x, y, z, ci in zip(xs, ys, zs, cis)]
        for z, ci in zip(zs, cis):
            o_ref[ci] = (z * _silu(g_ref[ci].astype(F32))).astype(o_ref.dtype)
        return carry

    lax.fori_loop(0, cw // HY_UNROLL, body, 0)


def _hyena(hy_t, taps, conv_w, conv_b, d_bias, n_blk, batch, cw=HY_CW):
    rows = n_blk * batch
    smem = pl.BlockSpec(memory_space=pltpu.SMEM)
    slab = lambda off: pl.BlockSpec((cw, rows, TOEP), lambda c: (off // cw + c, 0, 0))
    taps4 = taps.reshape(2, HY_W, 1, taps.shape[-1])
    pos = jnp.arange(TOEP)
    shift = jnp.concatenate([pos[:, None] == (pos[None, :] - 1) % TOEP,
                             pos[:, None] == (pos[None, :] + 1) % TOEP], axis=1).astype(BF16)
    return pl.pallas_call(
        functools.partial(_hyena_kernel, cw=cw, n_blk=n_blk, batch=batch),
        grid=(HY_W // cw,),
        in_specs=[smem, smem, smem, slab(0), slab(HY_W), slab(2 * HY_W), slab(3 * HY_W),
                  pl.BlockSpec((2, cw, 1, taps.shape[-1]), lambda c: (0, c, 0, 0)),
                  pl.BlockSpec((TOEP, 2 * TOEP), lambda c: (0, 0))],
        out_specs=pl.BlockSpec((cw, rows, TOEP), lambda c: (c, 0, 0)),
        out_shape=jax.ShapeDtypeStruct((HY_W, rows, TOEP), BF16),
        compiler_params=_cparams(("arbitrary",)),
        name="hyena_mixer",
    )(conv_w, conv_b, d_bias, hy_t, hy_t, hy_t, hy_t, taps4, shift)


def _out_kernel(x_ref, gate_ref, a_ref, b_ref, c_ref, w_ref, o_ref):
    o_ref[0] = x_ref[0] + gate_ref[0] * _mix_out(a_ref[0], b_ref[0], c_ref[0], w_ref)


def _out_projection(x, gate, a, bmix, c, w_out, tm):
    b, t, d = x.shape
    per_batch = gate.shape[0] > 1
    tok = lambda w: pl.BlockSpec((1, tm, w), lambda i, j: (i, j, 0))
    return pl.pallas_call(
        _out_kernel,
        grid=(b, t // tm),
        in_specs=[tok(d),
                  pl.BlockSpec((1, 1, d), (lambda i, j: (i, 0, 0)) if per_batch
                               else (lambda i, j: (0, 0, 0))),
                  tok(Q_W), tok(HY_W), tok(Q_W),
                  pl.BlockSpec(w_out.shape, lambda i, j: (0, 0))],
        out_specs=tok(d),
        out_shape=jax.ShapeDtypeStruct((b, t, d), F32),
        compiler_params=_cparams(("parallel", "arbitrary")),
        name="out_projection",
    )(x, gate, a, bmix, c, w_out)


def _rope_tables(t_len):
    pos = jnp.arange(t_len)
    n_freq = ROPE_FREQS
    inv_freq = ROPE_THETA ** (-jnp.arange(n_freq, dtype=F32) / n_freq)
    ang = jnp.stack([(pos // GRID_W).astype(F32)[:, None] * inv_freq,
                     (pos % GRID_W).astype(F32)[:, None] * inv_freq], axis=1)
    cos = jnp.cos(ang)[:, :, None, :]
    sin = jnp.sin(ang)[:, :, None, :]
    cos = jnp.broadcast_to(cos, (t_len, 2, 2, n_freq)).reshape(t_len, HEAD_DIM)
    sin = jnp.concatenate([-sin, sin], axis=2).reshape(t_len, HEAD_DIM)
    return jnp.tile(cos, (1, 2)), jnp.tile(sin, (1, 2))


def _to_channel_major(hy, n_blk):
    b, t, c = hy.shape
    return hy.reshape(b, n_blk, TOEP, c).transpose(3, 1, 0, 2).reshape(c, n_blk * b, TOEP)


def _to_token_major(y, n_blk, batch):
    c = y.shape[0]
    return y.reshape(c, n_blk, batch, TOEP).transpose(2, 1, 3, 0).reshape(batch, n_blk * TOEP, c)


def kernel(x, c, ctx, c_ctx, norm_g, w_mod, b_mod, w_in, w_out, qn_a, kn_a, qn_c, kn_c, sink_c,
           hy_conv_w, hy_conv_b, hy_w1, hy_b1, hy_w2, hy_b2, hy_w3, hy_freq, hy_bias):
    depth = w_in.shape[0]
    batch, seq, _ = x.shape
    n_ctx = ctx.shape[1]
    n_blk = seq // TOEP
    assert x.shape[2] == D_MODEL and seq % (ATTN_TQ * ATTN_TILES) == 0 and seq % GRID_W == 0
    assert seq % TOEP == 0 and n_ctx == TOEP and seq >= ATTN_TQ + 2 * WINDOW
    assert batch % 16 == 0 and HY_W % HY_CW == 0 and HY_CW % HY_UNROLL == 0

    rows = -(-(batch + 1) // 8) * 8
    c_all = jnp.concatenate([c, c_ctx[None]], axis=0)
    c_all = jnp.pad(c_all, ((0, rows - batch - 1), (0, 0)))
    mod = _modulation(c_all, w_mod, b_mod)

    cos_l, sin_l = _rope_tables(seq)
    grp = max(1, math.gcd(batch, PROJ_TM // n_ctx))
    fold = lambda t: t.reshape(batch // grp, grp * n_ctx, t.shape[-1])
    unfold = lambda t: t.reshape(batch, n_ctx, t.shape[-1])
    cos_c = jnp.ones((grp * n_ctx, LANES), F32)
    sin_c = jnp.zeros((grp * n_ctx, LANES), F32)
    w_in_b = w_in.astype(BF16)
    w_out_b = w_out.astype(BF16)

    pending = None
    for l in range(depth):
        last = l == depth - 1
        mod_x = mod[l, :batch, None, :]
        mod_c = mod[l, batch:batch + 1, None, :]
        ng = norm_g[l][None, :]
        gq_a = jnp.concatenate([jnp.tile(qn_a[l], N_HEADS), jnp.tile(kn_a[l], N_KV)])[None, :]
        gq_c = jnp.concatenate([jnp.tile(qn_c[l], N_HEADS), jnp.tile(kn_c[l], N_KV)])[None, :]

        outs = _projection(x, mod_x, ng, w_in_b[l], cos_l, sin_l, gq_a, gq_c,
                           tm=min(seq, PROJ_TM), prev=pending)
        if pending is not None:
            x, outs = outs[0], outs[1:]
        qa, ka, va, ga, hy, qc, kc, vc, gc = outs
        if last:
            c_off = 2 * D_MODEL + Q_W
            w_kv = jnp.concatenate([w_in_b[l][:, Q_W:Q_W + 2 * KV_W],
                                    w_in_b[l][:, c_off:c_off + 2 * KV_W]], axis=1)
            gk = jnp.concatenate([jnp.tile(kn_a[l], N_KV), jnp.tile(kn_c[l], N_KV)])[None, :]
            ka_c, va_c, kc_c, vc_c = map(unfold, _kv_projection(
                fold(ctx), mod_c, ng, w_kv, gk, tm=grp * n_ctx))
        else:
            qa_c, ka_c, va_c, ga_c, hy_c, qc_c, kc_c, vc_c, gc_c = map(unfold, _projection(
                fold(ctx), mod_c, ng, w_in_b[l], cos_c, sin_c, gq_a, gq_c, tm=grp * n_ctx))

        a_out = _attention(qa, ga, ka_c, va_c, ka, va, None, "full", ATTN_TQ, ATTN_TILES,
                           "attn_global")
        c_out = _attention(qc, gc, kc_c, vc_c, kc, vc, sink_c[l], "window", ATTN_TQ, ATTN_TILES,
                           "attn_window")
        taps = _hyena_filters(seq, hy_w1[l], hy_b1[l], hy_w2[l], hy_b2[l], hy_w3[l], hy_freq[l])
        b_out = _hyena(_to_channel_major(hy, n_blk), taps, hy_conv_w[l], hy_conv_b[l],
                       hy_bias[l], n_blk, batch)
        b_out = _to_token_major(b_out, n_blk, batch)
        pending = (mod_x[:, :, 2 * D_MODEL:], a_out, b_out, c_out, w_out_b[l])

        if not last:
            a_c = _attention(qa_c, ga_c, ka_c, va_c, None, None, None, None, n_ctx, 1,
                             "attn_ctx_a")
            c_c = _attention(qc_c, gc_c, kc_c, vc_c, None, None, sink_c[l], None, n_ctx, 1,
                             "attn_ctx_c")
            taps_c = _hyena_filters(n_ctx, hy_w1[l], hy_b1[l], hy_w2[l], hy_b2[l], hy_w3[l],
                                    hy_freq[l])
            b_c = _hyena(_to_channel_major(hy_c, 1), taps_c, hy_conv_w[l], hy_conv_b[l],
                         hy_bias[l], 1, batch)
            b_c = _to_token_major(b_c, 1, batch)
            ctx = unfold(_out_projection(fold(ctx), mod_c[:, :, 2 * D_MODEL:], fold(a_c),
                                         fold(b_c), fold(c_c), w_out_b[l], grp * n_ctx))
    gate, a_out, b_out, c_out, w_o = pending
    return _out_projection(x, gate, a_out, b_out, c_out, w_o, min(seq, PROJ_TM))
```

```python
import functools
import math

import jax
import jax.numpy as jnp
from jax import lax
from jax.experimental import pallas as pl
from jax.experimental.pallas import tpu as pltpu

F32 = jnp.float32
BF16 = jnp.bfloat16

D_MODEL = 1024
HEAD_DIM = 64
N_HEADS = 6
N_KV = 2
Q_W = N_HEADS * HEAD_DIM
KV_W = N_KV * HEAD_DIM
HY_W = 256
GRID_W = 64
WINDOW = 128
ROPE_THETA = 10000.0
NORM_EPS = 1e-6
NEG_INF = -1e30
HY_BANDS = 16
HY_FAST_DECAY = 0.3
HY_SLOW_DECAY = 1.5
HY_TARGET = 1e-2
LANES = 128
TOEP = 256
FEAT_PAD = 128
ROPE_FREQS = HEAD_DIM // 4
ROPE_PAIR = 2 * ROPE_FREQS

PROJ_TM = 1024
PROJ_SUB = 256
ATTN_TQ = 256
ATTN_TILES = 8
KEY_CHUNK = 768
HY_CW = 8
HY_UNROLL = 4
VMEM_LIMIT = 56 * 1024 * 1024

_NT = (((1,), (1,)), ((), ()))
LOG2E = 1.4426950408889634
Q_SCALE = LOG2E / math.sqrt(HEAD_DIM)


def _cparams(sem):
    return pltpu.CompilerParams(dimension_semantics=sem, vmem_limit_bytes=VMEM_LIMIT)


def _silu(x):
    return x * (1.0 / (1.0 + jnp.exp(-x)))


def _mod_kernel(c_ref, w_ref, b_ref, o_ref):
    s = _silu(c_ref[...])
    o_ref[0] = jnp.dot(s, w_ref[0], precision=lax.Precision.HIGHEST,
                       preferred_element_type=F32) + b_ref[0]


def _modulation(c_all, w_mod, b_mod):
    depth, d, n = w_mod.shape
    rows = c_all.shape[0]
    nb = n // d
    return pl.pallas_call(
        _mod_kernel,
        grid=(depth, nb),
        in_specs=[pl.BlockSpec((rows, d), lambda l, j: (0, 0)),
                  pl.BlockSpec((1, d, d), lambda l, j: (l, 0, j)),
                  pl.BlockSpec((1, 1, d), lambda l, j: (l, 0, j))],
        out_specs=pl.BlockSpec((1, rows, d), lambda l, j: (l, 0, j)),
        out_shape=jax.ShapeDtypeStruct((depth, rows, n), F32),
        compiler_params=_cparams(("arbitrary", "arbitrary")),
        name="modulation",
    )(c_all, w_mod, b_mod.reshape(depth, 1, n))


def _head_sum_squares(x):
    lane = lax.broadcasted_iota(jnp.int32, (1, LANES), 1)
    low = lane < HEAD_DIM
    out = []
    for s in range(0, x.shape[1], LANES):
        sq = x[:, s:s + LANES] * x[:, s:s + LANES]
        s_low = jnp.sum(jnp.where(low, sq, 0.0), axis=-1, keepdims=True)
        s_high = jnp.sum(jnp.where(low, 0.0, sq), axis=-1, keepdims=True)
        out.append(jnp.where(low, s_low, s_high))
    return jnp.concatenate(out, axis=1)


def _head_norm_rope(qk, gain, cos, sin):
    width = qk.shape[1]
    y = qk * lax.rsqrt(_head_sum_squares(qk) * (1.0 / HEAD_DIM) + NORM_EPS) * gain
    lane = lax.broadcasted_iota(jnp.int32, (1, LANES), 1)
    first_half = (lane % ROPE_PAIR) < ROPE_FREQS
    out = []
    for s in range(0, width, LANES):
        ys = y[:, s:s + LANES]
        partner = jnp.where(first_half, pltpu.roll(ys, LANES - ROPE_FREQS, axis=1),
                            pltpu.roll(ys, ROPE_FREQS, axis=1))
        out.append(ys * cos + partner * sin)
    return jnp.concatenate(out, axis=1)


def _with_swapped(t):
    return jnp.concatenate([t, pltpu.roll(t, HEAD_DIM, axis=1)], axis=1).astype(BF16)


def _mix_out(a, b, c, w_ref):
    return jnp.dot(jnp.concatenate([a, b, c], axis=1), w_ref[...], preferred_element_type=F32)


def _proj_kernel(*refs, prev_out):
    if prev_out:
        gate_ref, a_ref, b_ref, c_ref, wo_ref = refs[:5]
        refs = refs[5:]
    (x_ref, mod_ref, ng_ref, w_ref, cos_ref, sin_ref, gq_a_ref, gq_c_ref) = refs[:8]
    outs = refs[8:]
    if prev_out:
        xo_ref, outs = outs[0], outs[1:]
    qa_ref, ka_ref, va_ref, ga_ref, hy_ref, qc_ref, kc_ref, vc_ref, gc_ref = outs
    shift = mod_ref[0, :, 0:D_MODEL]
    scale = mod_ref[0, :, D_MODEL:2 * D_MODEL]
    tm = x_ref.shape[1]
    sub = min(tm, PROJ_SUB)
    for r0 in range(0, tm, sub):
        rows = slice(r0, r0 + sub)
        x = x_ref[0, rows]
        if prev_out:
            x = x + gate_ref[0] * _mix_out(a_ref[0, rows], b_ref[0, rows], c_ref[0, rows], wo_ref)
            xo_ref[0, rows] = x
        ms = jnp.mean(x * x, axis=-1, keepdims=True)
        y = x * lax.rsqrt(ms + NORM_EPS) * ng_ref[...]
        h = (y * (1.0 + scale) + shift).astype(BF16)
        cos = cos_ref[rows]
        sin = sin_ref[rows]

        def attn_branch(col0, gain_ref, q_ref, k_ref, v_ref, g_ref):
            acc = jnp.dot(h, w_ref[:, col0:col0 + D_MODEL], preferred_element_type=F32)
            qk = _head_norm_rope(acc[:, 0:Q_W + KV_W], gain_ref[...], cos, sin)
            q_ref[0, rows] = (qk[:, 0:Q_W] * Q_SCALE).astype(BF16)
            k_ref[0, rows] = _with_swapped(qk[:, Q_W:Q_W + KV_W])
            v_ref[0, rows] = _with_swapped(acc[:, Q_W + KV_W:Q_W + 2 * KV_W])
            g_ref[0, rows] = _silu(acc[:, Q_W + 2 * KV_W:]).astype(BF16)

        attn_branch(0, gq_a_ref, qa_ref, ka_ref, va_ref, ga_ref)
        hy_ref[0, rows] = jnp.dot(h, w_ref[:, D_MODEL:2 * D_MODEL],
                                  preferred_element_type=F32).astype(BF16)
        attn_branch(2 * D_MODEL, gq_c_ref, qc_ref, kc_ref, vc_ref, gc_ref)


def _projection(x, mod, norm_g, w_in, cos, sin, gq_a, gq_c, tm, prev=None):
    b, t, d = x.shape
    tok = lambda w: pl.BlockSpec((1, tm, w), lambda i, j: (i, j, 0))
    once = lambda a: pl.BlockSpec(a.shape, lambda i, j: (0,) * a.ndim,
                                  pipeline_mode=pl.Buffered(1))
    per_batch = lambda a: pl.BlockSpec((1, 1, a.shape[2]), (lambda i, j: (i, 0, 0))
                                       if a.shape[0] > 1 else (lambda i, j: (0, 0, 0)))
    widths = (Q_W, 2 * LANES, 2 * LANES, Q_W, D_MODEL, Q_W, 2 * LANES, 2 * LANES, Q_W)
    args = [x, mod, norm_g, w_in, cos, sin, gq_a, gq_c]
    in_specs = [tok(d), per_batch(mod), once(norm_g), once(w_in),
                pl.BlockSpec((tm, LANES), lambda i, j: (j, 0)),
                pl.BlockSpec((tm, LANES), lambda i, j: (j, 0)),
                once(gq_a), once(gq_c)]
    out_specs = [tok(w) for w in widths]
    out_shape = [jax.ShapeDtypeStruct((b, t, w), BF16) for w in widths]
    if prev is not None:
        gate, a, bmix, c, w_out = prev
        args = [gate, a, bmix, c, w_out] + args
        in_specs = [per_batch(gate), tok(Q_W), tok(HY_W), tok(Q_W), once(w_out)] + in_specs
        out_specs = [tok(d)] + out_specs
        out_shape = [jax.ShapeDtypeStruct((b, t, d), F32)] + out_shape
    return pl.pallas_call(
        functools.partial(_proj_kernel, prev_out=prev is not None),
        grid=(b, t // tm),
        in_specs=in_specs,
        out_specs=out_specs,
        out_shape=out_shape,
        compiler_params=_cparams(("parallel", "arbitrary")),
        name="in_projection",
    )(*args)


def _proj_kv_kernel(x_ref, mod_ref, ng_ref, w_ref, gk_ref, ka_ref, va_ref, kc_ref, vc_ref):
    x = x_ref[0]
    ms = jnp.mean(x * x, axis=-1, keepdims=True)
    y = x * lax.rsqrt(ms + NORM_EPS) * ng_ref[...]
    h = (y * (1.0 + mod_ref[0, :, D_MODEL:2 * D_MODEL]) + mod_ref[0, :, 0:D_MODEL]).astype(BF16)
    acc = jnp.dot(h, w_ref[...], preferred_element_type=F32)
    k = jnp.concatenate([acc[:, 0:KV_W], acc[:, 2 * KV_W:3 * KV_W]], axis=1)
    k = k * lax.rsqrt(_head_sum_squares(k) * (1.0 / HEAD_DIM) + NORM_EPS) * gk_ref[...]
    ka_ref[0] = _with_swapped(k[:, 0:KV_W])
    kc_ref[0] = _with_swapped(k[:, KV_W:])
    va_ref[0] = _with_swapped(acc[:, KV_W:2 * KV_W])
    vc_ref[0] = _with_swapped(acc[:, 3 * KV_W:])


def _kv_projection(x, mod, norm_g, w_kv, gk, tm):
    b, t, d = x.shape
    tok = lambda w: pl.BlockSpec((1, tm, w), lambda i, j: (i, j, 0))
    full = lambda a: pl.BlockSpec(a.shape, lambda i, j: (0,) * a.ndim)
    return pl.pallas_call(
        _proj_kv_kernel,
        grid=(b, t // tm),
        in_specs=[tok(d), full(mod), full(norm_g), full(w_kv), full(gk)],
        out_specs=[tok(2 * LANES)] * 4,
        out_shape=[jax.ShapeDtypeStruct((b, t, 2 * LANES), BF16)] * 4,
        compiler_params=_cparams(("parallel", "arbitrary")),
        name="kv_projection",
    )(x, mod, norm_g, w_kv, gk)


def _attn_kernel(*refs, tq, n_sub, lat_mode, has_sink, n_ctx, lat_len):
    refs = list(refs)
    sink_ref = refs.pop(0) if has_sink else None
    q_ref, g_ref, kc_ref, vc_ref = refs[:4]
    refs = refs[4:]
    if lat_mode is not None:
        kl_ref, vl_ref = refs[:2]
        refs = refs[2:]
    o_ref, kpad_s, vaug_s = refs
    i = pl.program_id(1)

    @pl.when(i == 0)
    def _stage_keys():
        lo = lax.broadcasted_iota(jnp.int32, (1, LANES), 1) < HEAD_DIM

        def fill(row0, n, k_ref, v_ref):
            rows = slice(row0, row0 + n)
            for dst, src, width in ((kpad_s, k_ref, LANES), (vaug_s, v_ref, 2 * LANES)):
                t = src[0, :, 0:LANES]
                sw = src[0, :, LANES:2 * LANES]
                zero = jnp.zeros_like(t)
                for blk, val in enumerate((jnp.where(lo, t, zero), jnp.where(lo, zero, sw),
                                           jnp.where(lo, sw, zero), jnp.where(lo, zero, t))):
                    dst[rows, blk * width:blk * width + LANES] = val

        fill(0, n_ctx, kc_ref, vc_ref)
        if lat_mode is not None:
            fill(n_ctx, lat_len, kl_ref, vl_ref)

    @pl.when((i == 0) & (pl.program_id(0) == 0))
    def _stage_ones():
        for blk in range(2 * N_KV):
            vaug_s[:, (2 * blk + 1) * LANES:(2 * blk + 2) * LANES] = jnp.ones(
                (vaug_s.shape[0], LANES), BF16)

    n_keys = n_ctx + lat_len
    for sub, pair in [(s_, p_) for s_ in range(n_sub) for p_ in range(N_HEADS // 2)]:
        q_rows = slice(sub * tq, (sub + 1) * tq)
        if lat_mode == "window":
            if pair == 0:
                span = tq + 2 * WINDOW
                q0 = (i * n_sub + sub) * tq
                start = jnp.clip(q0 - WINDOW, 0, lat_len - span)
                qpos = q0 + lax.broadcasted_iota(jnp.int32, (tq, span), 0)
                kpos = start + lax.broadcasted_iota(jnp.int32, (tq, span), 1)
                valid = jnp.abs(qpos - kpos) <= WINDOW
                band = pl.ds(pl.multiple_of(n_ctx + start, WINDOW), span)
            key_rows = [(slice(0, n_ctx), None), (band, valid)]
        else:
            key_rows = [(slice(r, min(r + KEY_CHUNK, n_keys)), None)
                        for r in range(0, n_keys, KEY_CHUNK)]
        lanes = slice(pair * LANES, (pair + 1) * LANES)
        qp = q_ref[0, q_rows, lanes]
        acc = jnp.zeros((tq, LANES), F32)
        for parity in range(2):
            head = 2 * pair + parity
            blk = 2 * (head // (N_HEADS // N_KV)) + parity
            kcols = slice(blk * LANES, (blk + 1) * LANES)
            vcols = slice(2 * blk * LANES, 2 * (blk + 1) * LANES)
            scores = []
            for rows, mask in key_rows:
                s = lax.dot_general(qp, kpad_s[rows, kcols], _NT, preferred_element_type=F32)
                scores.append(s if mask is None else jnp.where(mask, s, NEG_INF))
            m = functools.reduce(jnp.maximum,
                                 [jnp.max(s, axis=-1, keepdims=True) for s in scores])
            if has_sink:
                sink = sink_ref[head] * LOG2E
                m = jnp.maximum(m, sink)
            o = functools.reduce(jnp.add, [
                jnp.dot(jnp.exp2(s - m).astype(BF16), vaug_s[rows, vcols],
                        preferred_element_type=F32)
                for s, (rows, _) in zip(scores, key_rows)])
            den = o[:, LANES:]
            if has_sink:
                den = den + jnp.exp2(sink - m)
            acc = acc + o[:, :LANES] / den
        o_ref[0, q_rows, lanes] = (acc * g_ref[0, q_rows, lanes].astype(F32)).astype(BF16)


def _attention(q, g, k_ctx, v_ctx, k_lat, v_lat, sink, lat_mode, tq, n_sub, name):
    b, t, _ = q.shape
    n_ctx = k_ctx.shape[1]
    has_sink = sink is not None
    tok = pl.BlockSpec((1, n_sub * tq, Q_W), lambda i, j: (i, j, 0))
    whole = lambda n: pl.BlockSpec((1, n, 2 * LANES), lambda i, j: (i, 0, 0))
    args, specs = [], []
    if has_sink:
        args.append(sink)
        specs.append(pl.BlockSpec(memory_space=pltpu.SMEM))
    args += [q, g, k_ctx, v_ctx]
    specs += [tok, tok, whole(n_ctx), whole(n_ctx)]
    lat_len = 0
    if lat_mode is not None:
        lat_len = k_lat.shape[1]
        args += [k_lat, v_lat]
        specs += [whole(lat_len), whole(lat_len)]
    n_keys = n_ctx + lat_len
    return pl.pallas_call(
        functools.partial(_attn_kernel, tq=tq, n_sub=n_sub, lat_mode=lat_mode,
                          has_sink=has_sink, n_ctx=n_ctx, lat_len=lat_len),
        grid=(b, t // (n_sub * tq)),
        in_specs=specs,
        out_specs=tok,
        out_shape=jax.ShapeDtypeStruct((b, t, Q_W), BF16),
        scratch_shapes=[pltpu.VMEM((n_keys, 2 * N_KV * LANES), BF16),
                        pltpu.VMEM((n_keys, 4 * N_KV * LANES), BF16)],
        compiler_params=_cparams(("arbitrary", "arbitrary")),
        name=name,
    )(*args)


def _filter_kernel(z_ref, w1_ref, b1_ref, w2_ref, b2_ref, w3_ref, fr_ref, dl_ref, o_ref, *, lf):
    hp = lax.Precision.HIGHEST
    z = z_ref[...]
    h = jnp.sin(fr_ref[0:1, :] * (jnp.dot(z, w1_ref[...], precision=hp,
                                          preferred_element_type=F32) + b1_ref[...]))
    h = jnp.sin(fr_ref[1:2, :] * (jnp.dot(h, w2_ref[...], precision=hp,
                                          preferred_element_type=F32) + b2_ref[...]))
    decay = jnp.exp(-z[:, 0:1] * dl_ref[...])
    first_row = lax.broadcasted_iota(jnp.int32, decay.shape, 0) == 0
    for order in range(2):
        c0 = 2 * order * HY_W
        bwd = jnp.dot(h[:lf], w3_ref[:, c0 + HY_W:c0 + 2 * HY_W], precision=hp,
                      preferred_element_type=F32)
        fwd = jnp.dot(h[lf:], w3_ref[:, c0:c0 + HY_W], precision=hp,
                      preferred_element_type=F32)
        taps = jnp.where(first_row, 0.0, jnp.concatenate([bwd, fwd], axis=0) * decay)
        taps = taps / jnp.sum(jnp.abs(taps), axis=0, keepdims=True)
        t = taps.T.astype(BF16).astype(F32)
        lo = pltpu.bitcast(t, jnp.uint32) >> 16
        hi = pltpu.bitcast(pltpu.roll(t, 1, axis=1), jnp.uint32) & jnp.uint32(0xFFFF0000)
        o_ref[order] = pltpu.bitcast(lo | hi, jnp.int32)


def _hyena_filters(lf, w1, b1, w2, b2, w3, freq):
    t = jnp.linspace(0.0, 1.0, lf, dtype=F32)[:, None]
    bands = jnp.linspace(1e-4, HY_BANDS - 1, HY_BANDS, dtype=F32)
    w = 2.0 * math.pi * jnp.arange(lf, dtype=F32)[:, None] / lf
    z = jnp.concatenate([t, jnp.cos(bands * w), jnp.sin(bands * w)], axis=-1)
    z = jnp.concatenate([z[:1], z[:0:-1], z], axis=0)
    z = jnp.pad(z, ((0, 0), (0, FEAT_PAD - z.shape[1])))
    w1p = jnp.pad(w1, ((0, FEAT_PAD - w1.shape[0]), (0, 0)))
    min_decay = math.log(HY_TARGET) / HY_SLOW_DECAY
    max_decay = math.log(HY_TARGET) / HY_FAST_DECAY
    deltas = jnp.abs(jnp.linspace(min_decay, max_decay, HY_W, dtype=F32))[None, :]
    full = lambda a: pl.BlockSpec(a.shape, lambda o: (0,) * a.ndim)
    ins = (z, w1p, b1[None, :], w2, b2[None, :], w3, freq, deltas)
    return pl.pallas_call(
        functools.partial(_filter_kernel, lf=lf),
        grid=(1,),
        in_specs=[full(a) for a in ins],
        out_specs=pl.BlockSpec((2, HY_W, 2 * lf), lambda o: (0, 0, 0)),
        out_shape=jax.ShapeDtypeStruct((2, HY_W, 2 * lf), jnp.int32),
        compiler_params=_cparams(("arbitrary",)),
        name="hyena_filter",
    )(*ins)


def _hyena_kernel(cw_ref, cb_ref, db_ref, v_ref, x1_ref, x2_ref, g_ref, taps_ref, shift_ref,
                  o_ref, *, cw, n_blk, batch):
    rows = n_blk * batch
    c0 = pl.program_id(0) * cw
    lane = lax.broadcasted_iota(jnp.int32, (1, TOEP), 1)
    zero_blk = jnp.zeros((batch, TOEP), F32)

    def short_conv(p_b, ch):
        shifted = jnp.dot(p_b, shift_ref[...], preferred_element_type=F32)
        p = p_b.astype(F32)
        prev = shifted[:, :TOEP]
        nxt = shifted[:, TOEP:]
        if n_blk > 1:
            prev_edge = jnp.concatenate([zero_blk, prev[:rows - batch]], axis=0)
            next_edge = jnp.concatenate([nxt[batch:], zero_blk], axis=0)
        else:
            prev_edge = jnp.zeros_like(p)
            next_edge = jnp.zeros_like(p)
        prev = jnp.where(lane == 0, prev_edge, prev)
        nxt = jnp.where(lane == TOEP - 1, next_edge, nxt)
        return cb_ref[ch] + cw_ref[0, ch] * prev + cw_ref[1, ch] * p + cw_ref[2, ch] * nxt

    def long_conv(z, order, ci):
        words = taps_ref[order, ci]
        skew = pltpu.roll(jnp.broadcast_to(words, (TOEP // 2, words.shape[1])), 0, axis=1,
                          stride=2, stride_axis=0)
        skew = pltpu.bitcast(skew, BF16)
        zb = z.astype(BF16)
        acc = [jnp.zeros((batch, TOEP), F32) for _ in range(n_blk)]
        for d in range(-(n_blk - 1), n_blk):
            n_out = n_blk - abs(d)
            src = max(0, -d) * batch
            col = (n_blk + d) * TOEP
            part = jnp.dot(zb[src:src + n_out * batch], skew[:, col:col + TOEP],
                           preferred_element_type=F32)
            for k in range(n_out):
                acc[max(0, d) + k] = acc[max(0, d) + k] + part[k * batch:(k + 1) * batch]
        return jnp.concatenate(acc, axis=0) if n_blk > 1 else acc[0]

    def body(k, carry):
        cis = [k * HY_UNROLL + u for u in range(HY_UNROLL)]
        zs = [short_conv(v_ref[ci], c0 + ci) for ci in cis]
        for order, x_ref in enumerate((x1_ref, x2_ref)):
            xs = [short_conv(x_ref[ci], (order + 1) * HY_W + c0 + ci) for ci in cis]
            ys = [long_conv(z, order, ci) for z, ci in zip(zs, cis)]
            zs = [x * (y + z * db_ref[order, c0 + ci]) for x, y, z, ci in zip(xs, ys, zs, cis)]
        for z, ci in zip(zs, cis):
            o_ref[ci] = (z * _silu(g_ref[ci].astype(F32))).astype(o_ref.dtype)
        return carry

    lax.fori_loop(0, cw // HY_UNROLL, body, 0)


def _hyena(hy_t, taps, conv_w, conv_b, d_bias, n_blk, batch, cw=HY_CW):
    rows = n_blk * batch
    smem = pl.BlockSpec(memory_space=pltpu.SMEM)
    slab = lambda off: pl.BlockSpec((cw, rows, TOEP), lambda c: (off // cw + c, 0, 0))
    taps4 = taps.reshape(2, HY_W, 1, taps.shape[-1])
    pos = jnp.arange(TOEP)
    shift = jnp.concatenate([pos[:, None] == (pos[None, :] - 1) % TOEP,
                             pos[:, None] == (pos[None, :] + 1) % TOEP], axis=1).astype(BF16)
    return pl.pallas_call(
        functools.partial(_hyena_kernel, cw=cw, n_blk=n_blk, batch=batch),
        grid=(HY_W // cw,),
        in_specs=[smem, smem, smem, slab(0), slab(HY_W), slab(2 * HY_W), slab(3 * HY_W),
                  pl.BlockSpec((2, cw, 1, taps.shape[-1]), lambda c: (0, c, 0, 0)),
                  pl.BlockSpec((TOEP, 2 * TOEP), lambda c: (0, 0))],
        out_specs=pl.BlockSpec((cw, rows, TOEP), lambda c: (c, 0, 0)),
        out_shape=jax.ShapeDtypeStruct((HY_W, rows, TOEP), BF16),
        compiler_params=_cparams(("arbitrary",)),
        name="hyena_mixer",
    )(conv_w, conv_b, d_bias, hy_t, hy_t, hy_t, hy_t, taps4, shift)


def _out_kernel(x_ref, gate_ref, a_ref, b_ref, c_ref, w_ref, o_ref):
    o_ref[0] = x_ref[0] + gate_ref[0] * _mix_out(a_ref[0], b_ref[0], c_ref[0], w_ref)


def _out_projection(x, gate, a, bmix, c, w_out, tm):
    b, t, d = x.shape
    per_batch = gate.shape[0] > 1
    tok = lambda w: pl.BlockSpec((1, tm, w), lambda i, j: (i, j, 0))
    return pl.pallas_call(
        _out_kernel,
        grid=(b, t // tm),
        in_specs=[tok(d),
                  pl.BlockSpec((1, 1, d), (lambda i, j: (i, 0, 0)) if per_batch
                               else (lambda i, j: (0, 0, 0))),
                  tok(Q_W), tok(HY_W), tok(Q_W),
                  pl.BlockSpec(w_out.shape, lambda i, j: (0, 0))],
        out_specs=tok(d),
        out_shape=jax.ShapeDtypeStruct((b, t, d), F32),
        compiler_params=_cparams(("parallel", "arbitrary")),
        name="out_projection",
    )(x, gate, a, bmix, c, w_out)


def _rope_tables(t_len):
    pos = jnp.arange(t_len)
    n_freq = ROPE_FREQS
    inv_freq = ROPE_THETA ** (-jnp.arange(n_freq, dtype=F32) / n_freq)
    ang = jnp.stack([(pos // GRID_W).astype(F32)[:, None] * inv_freq,
                     (pos % GRID_W).astype(F32)[:, None] * inv_freq], axis=1)
    cos = jnp.cos(ang)[:, :, None, :]
    sin = jnp.sin(ang)[:, :, None, :]
    cos = jnp.broadcast_to(cos, (t_len, 2, 2, n_freq)).reshape(t_len, HEAD_DIM)
    sin = jnp.concatenate([-sin, sin], axis=2).reshape(t_len, HEAD_DIM)
    return jnp.tile(cos, (1, 2)), jnp.tile(sin, (1, 2))


def _to_channel_major(hy, n_blk):
    b, t, c = hy.shape
    return hy.reshape(b, n_blk, TOEP, c).transpose(3, 1, 0, 2).reshape(c, n_blk * b, TOEP)


def _to_token_major(y, n_blk, batch):
    c = y.shape[0]
    return y.reshape(c, n_blk, batch, TOEP).transpose(2, 1, 3, 0).reshape(batch, n_blk * TOEP, c)


def kernel(x, c, ctx, c_ctx, norm_g, w_mod, b_mod, w_in, w_out, qn_a, kn_a, qn_c, kn_c, sink_c,
           hy_conv_w, hy_conv_b, hy_w1, hy_b1, hy_w2, hy_b2, hy_w3, hy_freq, hy_bias):
    depth = w_in.shape[0]
    batch, seq, _ = x.shape
    n_ctx = ctx.shape[1]
    n_blk = seq // TOEP
    assert x.shape[2] == D_MODEL and seq % (ATTN_TQ * ATTN_TILES) == 0 and seq % GRID_W == 0
    assert seq % TOEP == 0 and n_ctx == TOEP and seq >= ATTN_TQ + 2 * WINDOW
    assert batch % 16 == 0 and HY_W % HY_CW == 0 and HY_CW % HY_UNROLL == 0
    assert KEY_CHUNK % TOEP == 0

    rows = -(-(batch + 1) // 8) * 8
    c_all = jnp.concatenate([c, c_ctx[None]], axis=0)
    c_all = jnp.pad(c_all, ((0, rows - batch - 1), (0, 0)))
    mod = _modulation(c_all, w_mod, b_mod)

    cos_l, sin_l = _rope_tables(seq)
    grp = max(1, math.gcd(batch, PROJ_TM // n_ctx))
    fold = lambda t: t.reshape(batch // grp, grp * n_ctx, t.shape[-1])
    unfold = lambda t: t.reshape(batch, n_ctx, t.shape[-1])
    cos_c = jnp.ones((grp * n_ctx, LANES), F32)
    sin_c = jnp.zeros((grp * n_ctx, LANES), F32)
    w_in_b = w_in.astype(BF16)
    w_out_b = w_out.astype(BF16)

    pending = None
    for l in range(depth):
        last = l == depth - 1
        mod_x = mod[l, :batch, None, :]
        mod_c = mod[l, batch:batch + 1, None, :]
        ng = norm_g[l][None, :]
        gq_a = jnp.concatenate([jnp.tile(qn_a[l], N_HEADS), jnp.tile(kn_a[l], N_KV)])[None, :]
        gq_c = jnp.concatenate([jnp.tile(qn_c[l], N_HEADS), jnp.tile(kn_c[l], N_KV)])[None, :]

        outs = _projection(x, mod_x, ng, w_in_b[l], cos_l, sin_l, gq_a, gq_c,
                           tm=min(seq, PROJ_TM), prev=pending)
        if pending is not None:
            x, outs = outs[0], outs[1:]
        qa, ka, va, ga, hy, qc, kc, vc, gc = outs
        if last:
            c_off = 2 * D_MODEL + Q_W
            w_kv = jnp.concatenate([w_in_b[l][:, Q_W:Q_W + 2 * KV_W],
                                    w_in_b[l][:, c_off:c_off + 2 * KV_W]], axis=1)
            gk = jnp.concatenate([jnp.tile(kn_a[l], N_KV), jnp.tile(kn_c[l], N_KV)])[None, :]
            ka_c, va_c, kc_c, vc_c = map(unfold, _kv_projection(
                fold(ctx), mod_c, ng, w_kv, gk, tm=grp * n_ctx))
        else:
            qa_c, ka_c, va_c, ga_c, hy_c, qc_c, kc_c, vc_c, gc_c = map(unfold, _projection(
                fold(ctx), mod_c, ng, w_in_b[l], cos_c, sin_c, gq_a, gq_c, tm=grp * n_ctx))

        a_out = _attention(qa, ga, ka_c, va_c, ka, va, None, "full", ATTN_TQ, ATTN_TILES,
                           "attn_global")
        c_out = _attention(qc, gc, kc_c, vc_c, kc, vc, sink_c[l], "window", ATTN_TQ, ATTN_TILES,
                           "attn_window")
        taps = _hyena_filters(seq, hy_w1[l], hy_b1[l], hy_w2[l], hy_b2[l], hy_w3[l], hy_freq[l])
        b_out = _hyena(_to_channel_major(hy, n_blk), taps, hy_conv_w[l], hy_conv_b[l],
                       hy_bias[l], n_blk, batch)
        b_out = _to_token_major(b_out, n_blk, batch)
        pending = (mod_x[:, :, 2 * D_MODEL:], a_out, b_out, c_out, w_out_b[l])

        if not last:
            a_c = _attention(qa_c, ga_c, ka_c, va_c, None, None, None, None, n_ctx, 1,
                             "attn_ctx_a")
            c_c = _attention(qc_c, gc_c, kc_c, vc_c, None, None, sink_c[l], None, n_ctx, 1,
                             "attn_ctx_c")
            taps_c = _hyena_filters(n_ctx, hy_w1[l], hy_b1[l], hy_w2[l], hy_b2[l], hy_w3[l],
                                    hy_freq[l])
            b_c = _hyena(_to_channel_major(hy_c, 1), taps_c, hy_conv_w[l], hy_conv_b[l],
                         hy_bias[l], 1, batch)
            b_c = _to_token_major(b_c, 1, batch)
            ctx = unfold(_out_projection(fold(ctx), mod_c[:, :, 2 * D_MODEL:], fold(a_c),
                                         fold(b_c), fold(c_c), w_out_b[l], grp * n_ctx))
    gate, a_out, b_out, c_out, w_o = pending
    return _out_projection(x, gate, a_out, b_out, c_out, w_o, min(seq, PROJ_TM))
```

```python
import functools
import math

import jax
import jax.numpy as jnp
from jax import lax
from jax.experimental import pallas as pl
from jax.experimental.pallas import tpu as pltpu

F32 = jnp.float32
BF16 = jnp.bfloat16

D_MODEL = 1024
HEAD_DIM = 64
N_HEADS = 6
N_KV = 2
Q_W = N_HEADS * HEAD_DIM
KV_W = N_KV * HEAD_DIM
HY_W = 256
GRID_W = 64
WINDOW = 128
ROPE_THETA = 10000.0
NORM_EPS = 1e-6
NEG_INF = -1e30
HY_BANDS = 16
HY_FAST_DECAY = 0.3
HY_SLOW_DECAY = 1.5
HY_TARGET = 1e-2
LANES = 128
TOEP = 256
FEAT_PAD = 128
ROPE_FREQS = HEAD_DIM // 4
ROPE_PAIR = 2 * ROPE_FREQS

PROJ_TM = 1024
PROJ_SUB = 256
ATTN_TQ = 256
ATTN_TILES = 8
KEY_CHUNK = 768
HY_CW = 8
HY_UNROLL = 4
VMEM_LIMIT = 56 * 1024 * 1024

_NT = (((1,), (1,)), ((), ()))
LOG2E = 1.4426950408889634
Q_SCALE = LOG2E / math.sqrt(HEAD_DIM)


def _cparams(sem):
    return pltpu.CompilerParams(dimension_semantics=sem, vmem_limit_bytes=VMEM_LIMIT)


def _silu(x):
    return x * (1.0 / (1.0 + jnp.exp(-x)))


def _mod_kernel(c_ref, w_ref, b_ref, o_ref):
    s = _silu(c_ref[...])
    o_ref[0] = jnp.dot(s, w_ref[0], precision=lax.Precision.HIGHEST,
                       preferred_element_type=F32) + b_ref[0]


def _modulation(c_all, w_mod, b_mod):
    depth, d, n = w_mod.shape
    rows = c_all.shape[0]
    nb = n // d
    return pl.pallas_call(
        _mod_kernel,
        grid=(depth, nb),
        in_specs=[pl.BlockSpec((rows, d), lambda l, j: (0, 0)),
                  pl.BlockSpec((1, d, d), lambda l, j: (l, 0, j)),
                  pl.BlockSpec((1, 1, d), lambda l, j: (l, 0, j))],
        out_specs=pl.BlockSpec((1, rows, d), lambda l, j: (l, 0, j)),
        out_shape=jax.ShapeDtypeStruct((depth, rows, n), F32),
        compiler_params=_cparams(("arbitrary", "arbitrary")),
        name="modulation",
    )(c_all, w_mod, b_mod.reshape(depth, 1, n))


def _head_sum_squares(x):
    lane = lax.broadcasted_iota(jnp.int32, (1, LANES), 1)
    low = lane < HEAD_DIM
    out = []
    for s in range(0, x.shape[1], LANES):
        sq = x[:, s:s + LANES] * x[:, s:s + LANES]
        s_low = jnp.sum(jnp.where(low, sq, 0.0), axis=-1, keepdims=True)
        s_high = jnp.sum(jnp.where(low, 0.0, sq), axis=-1, keepdims=True)
        out.append(jnp.where(low, s_low, s_high))
    return jnp.concatenate(out, axis=1)


def _head_norm_rope(qk, gain, cos, sin):
    width = qk.shape[1]
    y = qk * lax.rsqrt(_head_sum_squares(qk) * (1.0 / HEAD_DIM) + NORM_EPS) * gain
    lane = lax.broadcasted_iota(jnp.int32, (1, LANES), 1)
    first_half = (lane % ROPE_PAIR) < ROPE_FREQS
    out = []
    for s in range(0, width, LANES):
        ys = y[:, s:s + LANES]
        partner = jnp.where(first_half, pltpu.roll(ys, LANES - ROPE_FREQS, axis=1),
                            pltpu.roll(ys, ROPE_FREQS, axis=1))
        out.append(ys * cos + partner * sin)
    return jnp.concatenate(out, axis=1)


def _with_swapped(t):
    return jnp.concatenate([t, pltpu.roll(t, HEAD_DIM, axis=1)], axis=1).astype(BF16)


def _mix_out(a, b, c, w_ref):
    return jnp.dot(jnp.concatenate([a, b, c], axis=1), w_ref[...], preferred_element_type=F32)


def _proj_kernel(*refs, prev_out):
    if prev_out:
        gate_ref, a_ref, b_ref, c_ref, wo_ref = refs[:5]
        refs = refs[5:]
    (x_ref, mod_ref, ng_ref, w_ref, cos_ref, sin_ref, gq_a_ref, gq_c_ref) = refs[:8]
    outs = refs[8:]
    if prev_out:
        xo_ref, outs = outs[0], outs[1:]
    qa_ref, ka_ref, va_ref, ga_ref, hy_ref, qc_ref, kc_ref, vc_ref, gc_ref = outs
    shift = mod_ref[0, :, 0:D_MODEL]
    scale = mod_ref[0, :, D_MODEL:2 * D_MODEL]
    tm = x_ref.shape[1]
    sub = min(tm, PROJ_SUB)
    for r0 in range(0, tm, sub):
        rows = slice(r0, r0 + sub)
        x = x_ref[0, rows]
        if prev_out:
            x = x + gate_ref[0] * _mix_out(a_ref[0, rows], b_ref[0, rows], c_ref[0, rows], wo_ref)
            xo_ref[0, rows] = x
        ms = jnp.mean(x * x, axis=-1, keepdims=True)
        y = x * lax.rsqrt(ms + NORM_EPS) * ng_ref[...]
        h = (y * (1.0 + scale) + shift).astype(BF16)
        cos = cos_ref[rows]
        sin = sin_ref[rows]

        def attn_branch(col0, gain_ref, q_ref, k_ref, v_ref, g_ref):
            acc = jnp.dot(h, w_ref[:, col0:col0 + D_MODEL], preferred_element_type=F32)
            qk = _head_norm_rope(acc[:, 0:Q_W + KV_W], gain_ref[...], cos, sin)
            q_ref[0, rows] = (qk[:, 0:Q_W] * Q_SCALE).astype(BF16)
            k_ref[0, rows] = _with_swapped(qk[:, Q_W:Q_W + KV_W])
            v_ref[0, rows] = _with_swapped(acc[:, Q_W + KV_W:Q_W + 2 * KV_W])
            g_ref[0, rows] = _silu(acc[:, Q_W + 2 * KV_W:]).astype(BF16)

        attn_branch(0, gq_a_ref, qa_ref, ka_ref, va_ref, ga_ref)
        attn_branch(2 * D_MODEL, gq_c_ref, qc_ref, kc_ref, vc_ref, gc_ref)
        hy_ref[0, rows] = jnp.dot(h, w_ref[:, D_MODEL:2 * D_MODEL],
                                  preferred_element_type=F32).astype(BF16)


def _projection(x, mod, norm_g, w_in, cos, sin, gq_a, gq_c, tm, prev=None):
    b, t, d = x.shape
    tok = lambda w: pl.BlockSpec((1, tm, w), lambda i, j: (i, j, 0))
    once = lambda a: pl.BlockSpec(a.shape, lambda i, j: (0,) * a.ndim,
                                  pipeline_mode=pl.Buffered(1))
    per_batch = lambda a: pl.BlockSpec((1, 1, a.shape[2]), (lambda i, j: (i, 0, 0))
                                       if a.shape[0] > 1 else (lambda i, j: (0, 0, 0)))
    widths = (Q_W, 2 * LANES, 2 * LANES, Q_W, D_MODEL, Q_W, 2 * LANES, 2 * LANES, Q_W)
    args = [x, mod, norm_g, w_in, cos, sin, gq_a, gq_c]
    in_specs = [tok(d), per_batch(mod), once(norm_g), once(w_in),
                pl.BlockSpec((tm, LANES), lambda i, j: (j, 0)),
                pl.BlockSpec((tm, LANES), lambda i, j: (j, 0)),
                once(gq_a), once(gq_c)]
    out_specs = [tok(w) for w in widths]
    out_shape = [jax.ShapeDtypeStruct((b, t, w), BF16) for w in widths]
    if prev is not None:
        gate, a, bmix, c, w_out = prev
        args = [gate, a, bmix, c, w_out] + args
        in_specs = [per_batch(gate), tok(Q_W), tok(HY_W), tok(Q_W), once(w_out)] + in_specs
        out_specs = [tok(d)] + out_specs
        out_shape = [jax.ShapeDtypeStruct((b, t, d), F32)] + out_shape
    return pl.pallas_call(
        functools.partial(_proj_kernel, prev_out=prev is not None),
        grid=(b, t // tm),
        in_specs=in_specs,
        out_specs=out_specs,
        out_shape=out_shape,
        compiler_params=_cparams(("parallel", "arbitrary")),
        name="in_projection",
    )(*args)


def _proj_kv_kernel(x_ref, mod_ref, ng_ref, w_ref, gk_ref, ka_ref, va_ref, kc_ref, vc_ref):
    x = x_ref[0]
    ms = jnp.mean(x * x, axis=-1, keepdims=True)
    y = x * lax.rsqrt(ms + NORM_EPS) * ng_ref[...]
    h = (y * (1.0 + mod_ref[0, :, D_MODEL:2 * D_MODEL]) + mod_ref[0, :, 0:D_MODEL]).astype(BF16)
    acc = jnp.dot(h, w_ref[...], preferred_element_type=F32)
    k = jnp.concatenate([acc[:, 0:KV_W], acc[:, 2 * KV_W:3 * KV_W]], axis=1)
    k = k * lax.rsqrt(_head_sum_squares(k) * (1.0 / HEAD_DIM) + NORM_EPS) * gk_ref[...]
    ka_ref[0] = _with_swapped(k[:, 0:KV_W])
    kc_ref[0] = _with_swapped(k[:, KV_W:])
    va_ref[0] = _with_swapped(acc[:, KV_W:2 * KV_W])
    vc_ref[0] = _with_swapped(acc[:, 3 * KV_W:])


def _kv_projection(x, mod, norm_g, w_kv, gk, tm):
    b, t, d = x.shape
    tok = lambda w: pl.BlockSpec((1, tm, w), lambda i, j: (i, j, 0))
    full = lambda a: pl.BlockSpec(a.shape, lambda i, j: (0,) * a.ndim)
    return pl.pallas_call(
        _proj_kv_kernel,
        grid=(b, t // tm),
        in_specs=[tok(d), full(mod), full(norm_g), full(w_kv), full(gk)],
        out_specs=[tok(2 * LANES)] * 4,
        out_shape=[jax.ShapeDtypeStruct((b, t, 2 * LANES), BF16)] * 4,
        compiler_params=_cparams(("parallel", "arbitrary")),
        name="kv_projection",
    )(x, mod, norm_g, w_kv, gk)


def _attn_kernel(*refs, tq, n_sub, lat_mode, has_sink, n_ctx, lat_len):
    refs = list(refs)
    sink_ref = refs.pop(0) if has_sink else None
    q_ref, g_ref, kc_ref, vc_ref = refs[:4]
    refs = refs[4:]
    if lat_mode is not None:
        kl_ref, vl_ref = refs[:2]
        refs = refs[2:]
    o_ref, kpad_s, vaug_s = refs
    i = pl.program_id(1)

    @pl.when(i == 0)
    def _stage_keys():
        lo = lax.broadcasted_iota(jnp.int32, (1, LANES), 1) < HEAD_DIM

        def fill(row0, n, k_ref, v_ref):
            rows = slice(row0, row0 + n)
            for dst, src, width in ((kpad_s, k_ref, LANES), (vaug_s, v_ref, 2 * LANES)):
                t = src[0, :, 0:LANES]
                sw = src[0, :, LANES:2 * LANES]
                zero = jnp.zeros_like(t)
                for blk, val in enumerate((jnp.where(lo, t, zero), jnp.where(lo, zero, sw),
                                           jnp.where(lo, sw, zero), jnp.where(lo, zero, t))):
                    dst[rows, blk * width:blk * width + LANES] = val

        fill(0, n_ctx, kc_ref, vc_ref)
        if lat_mode is not None:
            fill(n_ctx, lat_len, kl_ref, vl_ref)

    @pl.when((i == 0) & (pl.program_id(0) == 0))
    def _stage_ones():
        for blk in range(2 * N_KV):
            vaug_s[:, (2 * blk + 1) * LANES:(2 * blk + 2) * LANES] = jnp.ones(
                (vaug_s.shape[0], LANES), BF16)

    n_keys = n_ctx + lat_len
    for sub, pair in [(s_, p_) for s_ in range(n_sub) for p_ in range(N_HEADS // 2)]:
        q_rows = slice(sub * tq, (sub + 1) * tq)
        if lat_mode == "window":
            if pair == 0:
                span = tq + 2 * WINDOW
                q0 = (i * n_sub + sub) * tq
                start = jnp.clip(q0 - WINDOW, 0, lat_len - span)
                qpos = q0 + lax.broadcasted_iota(jnp.int32, (tq, span), 0)
                kpos = start + lax.broadcasted_iota(jnp.int32, (tq, span), 1)
                valid = jnp.abs(qpos - kpos) <= WINDOW
                band = pl.ds(pl.multiple_of(n_ctx + start, WINDOW), span)
            key_rows = [(slice(0, n_ctx), None), (band, valid)]
        else:
            key_rows = [(slice(r, min(r + KEY_CHUNK, n_keys)), None)
                        for r in range(0, n_keys, KEY_CHUNK)]
        lanes = slice(pair * LANES, (pair + 1) * LANES)
        qp = q_ref[0, q_rows, lanes]
        acc = jnp.zeros((tq, LANES), F32)
        for parity in range(2):
            head = 2 * pair + parity
            blk = 2 * (head // (N_HEADS // N_KV)) + parity
            kcols = slice(blk * LANES, (blk + 1) * LANES)
            vcols = slice(2 * blk * LANES, 2 * (blk + 1) * LANES)
            scores = []
            for rows, mask in key_rows:
                s = lax.dot_general(qp, kpad_s[rows, kcols], _NT, preferred_element_type=F32)
                scores.append(s if mask is None else jnp.where(mask, s, NEG_INF))
            m = functools.reduce(jnp.maximum,
                                 [jnp.max(s, axis=-1, keepdims=True) for s in scores])
            if has_sink:
                sink = sink_ref[head] * LOG2E
                m = jnp.maximum(m, sink)
            o = functools.reduce(jnp.add, [
                jnp.dot(jnp.exp2(s - m).astype(BF16), vaug_s[rows, vcols],
                        preferred_element_type=F32)
                for s, (rows, _) in zip(scores, key_rows)])
            den = o[:, LANES:]
            if has_sink:
                den = den + jnp.exp2(sink - m)
            acc = acc + o[:, :LANES] / den
        o_ref[0, q_rows, lanes] = (acc * g_ref[0, q_rows, lanes].astype(F32)).astype(BF16)


def _attention(q, g, k_ctx, v_ctx, k_lat, v_lat, sink, lat_mode, tq, n_sub, name):
    b, t, _ = q.shape
    n_ctx = k_ctx.shape[1]
    has_sink = sink is not None
    tok = pl.BlockSpec((1, n_sub * tq, Q_W), lambda i, j: (i, j, 0))
    whole = lambda n: pl.BlockSpec((1, n, 2 * LANES), lambda i, j: (i, 0, 0))
    args, specs = [], []
    if has_sink:
        args.append(sink)
        specs.append(pl.BlockSpec(memory_space=pltpu.SMEM))
    args += [q, g, k_ctx, v_ctx]
    specs += [tok, tok, whole(n_ctx), whole(n_ctx)]
    lat_len = 0
    if lat_mode is not None:
        lat_len = k_lat.shape[1]
        args += [k_lat, v_lat]
        specs += [whole(lat_len), whole(lat_len)]
    n_keys = n_ctx + lat_len
    return pl.pallas_call(
        functools.partial(_attn_kernel, tq=tq, n_sub=n_sub, lat_mode=lat_mode,
                          has_sink=has_sink, n_ctx=n_ctx, lat_len=lat_len),
        grid=(b, t // (n_sub * tq)),
        in_specs=specs,
        out_specs=tok,
        out_shape=jax.ShapeDtypeStruct((b, t, Q_W), BF16),
        scratch_shapes=[pltpu.VMEM((n_keys, 2 * N_KV * LANES), BF16),
                        pltpu.VMEM((n_keys, 4 * N_KV * LANES), BF16)],
        compiler_params=_cparams(("arbitrary", "arbitrary")),
        name=name,
    )(*args)


def _filter_kernel(z_ref, w1_ref, b1_ref, w2_ref, b2_ref, w3_ref, fr_ref, dl_ref, o_ref, *, lf):
    hp = lax.Precision.HIGHEST
    z = z_ref[...]
    h = jnp.sin(fr_ref[0:1, :] * (jnp.dot(z, w1_ref[...], precision=hp,
                                          preferred_element_type=F32) + b1_ref[...]))
    h = jnp.sin(fr_ref[1:2, :] * (jnp.dot(h, w2_ref[...], precision=hp,
                                          preferred_element_type=F32) + b2_ref[...]))
    decay = jnp.exp(-z[:, 0:1] * dl_ref[...])
    first_row = lax.broadcasted_iota(jnp.int32, decay.shape, 0) == 0
    for order in range(2):
        c0 = 2 * order * HY_W
        bwd = jnp.dot(h[:lf], w3_ref[:, c0 + HY_W:c0 + 2 * HY_W], precision=hp,
                      preferred_element_type=F32)
        fwd = jnp.dot(h[lf:], w3_ref[:, c0:c0 + HY_W], precision=hp,
                      preferred_element_type=F32)
        taps = jnp.where(first_row, 0.0, jnp.concatenate([bwd, fwd], axis=0) * decay)
        taps = taps / jnp.sum(jnp.abs(taps), axis=0, keepdims=True)
        t = taps.T.astype(BF16).astype(F32)
        lo = pltpu.bitcast(t, jnp.uint32) >> 16
        hi = pltpu.bitcast(pltpu.roll(t, 1, axis=1), jnp.uint32) & jnp.uint32(0xFFFF0000)
        o_ref[order] = pltpu.bitcast(lo | hi, jnp.int32)


def _hyena_filters(lf, w1, b1, w2, b2, w3, freq):
    t = jnp.linspace(0.0, 1.0, lf, dtype=F32)[:, None]
    bands = jnp.linspace(1e-4, HY_BANDS - 1, HY_BANDS, dtype=F32)
    w = 2.0 * math.pi * jnp.arange(lf, dtype=F32)[:, None] / lf
    z = jnp.concatenate([t, jnp.cos(bands * w), jnp.sin(bands * w)], axis=-1)
    z = jnp.concatenate([z[:1], z[:0:-1], z], axis=0)
    z = jnp.pad(z, ((0, 0), (0, FEAT_PAD - z.shape[1])))
    w1p = jnp.pad(w1, ((0, FEAT_PAD - w1.shape[0]), (0, 0)))
    min_decay = math.log(HY_TARGET) / HY_SLOW_DECAY
    max_decay = math.log(HY_TARGET) / HY_FAST_DECAY
    deltas = jnp.abs(jnp.linspace(min_decay, max_decay, HY_W, dtype=F32))[None, :]
    full = lambda a: pl.BlockSpec(a.shape, lambda o: (0,) * a.ndim)
    ins = (z, w1p, b1[None, :], w2, b2[None, :], w3, freq, deltas)
    return pl.pallas_call(
        functools.partial(_filter_kernel, lf=lf),
        grid=(1,),
        in_specs=[full(a) for a in ins],
        out_specs=pl.BlockSpec((2, HY_W, 2 * lf), lambda o: (0, 0, 0)),
        out_shape=jax.ShapeDtypeStruct((2, HY_W, 2 * lf), jnp.int32),
        compiler_params=_cparams(("arbitrary",)),
        name="hyena_filter",
    )(*ins)


def _hyena_kernel(cw_ref, cb_ref, db_ref, v_ref, x1_ref, x2_ref, g_ref, taps_ref, shift_ref,
                  o_ref, *, cw, n_blk, batch):
    rows = n_blk * batch
    c0 = pl.program_id(0) * cw
    lane = lax.broadcasted_iota(jnp.int32, (1, TOEP), 1)
    zero_blk = jnp.zeros((batch, TOEP), F32)

    def short_conv(p_b, ch):
        shifted = jnp.dot(p_b, shift_ref[...], preferred_element_type=F32)
        p = p_b.astype(F32)
        prev = shifted[:, :TOEP]
        nxt = shifted[:, TOEP:]
        if n_blk > 1:
            prev_edge = jnp.concatenate([zero_blk, prev[:rows - batch]], axis=0)
            next_edge = jnp.concatenate([nxt[batch:], zero_blk], axis=0)
        else:
            prev_edge = jnp.zeros_like(p)
            next_edge = jnp.zeros_like(p)
        prev = jnp.where(lane == 0, prev_edge, prev)
        nxt = jnp.where(lane == TOEP - 1, next_edge, nxt)
        return cb_ref[ch] + cw_ref[0, ch] * prev + cw_ref[1, ch] * p + cw_ref[2, ch] * nxt

    def long_conv(z, order, ci):
        words = taps_ref[order, ci]
        skew = pltpu.roll(jnp.broadcast_to(words, (TOEP // 2, words.shape[1])), 0, axis=1,
                          stride=2, stride_axis=0)
        skew = pltpu.bitcast(skew, BF16)
        zb = z.astype(BF16)
        acc = [jnp.zeros((batch, TOEP), F32) for _ in range(n_blk)]
        for d in range(-(n_blk - 1), n_blk):
            n_out = n_blk - abs(d)
            src = max(0, -d) * batch
            col = (n_blk + d) * TOEP
            part = jnp.dot(zb[src:src + n_out * batch], skew[:, col:col + TOEP],
                           preferred_element_type=F32)
            for k in range(n_out):
                acc[max(0, d) + k] = acc[max(0, d) + k] + part[k * batch:(k + 1) * batch]
        return jnp.concatenate(acc, axis=0) if n_blk > 1 else acc[0]

    def body(k, carry):
        cis = [k * HY_UNROLL + u for u in range(HY_UNROLL)]
        zs = [short_conv(v_ref[ci], c0 + ci) for ci in cis]
        for order, x_ref in enumerate((x1_ref, x2_ref)):
            xs = [short_conv(x_ref[ci], (order + 1) * HY_W + c0 + ci) for ci in cis]
            ys = [long_conv(z, order, ci) for z, ci in zip(zs, cis)]
            zs = [x * (y + z * db_ref[order, c0 + ci]) for x, y, z, ci in zip(xs, ys, zs, cis)]
        for z, ci in zip(zs, cis):
            o_ref[ci] = (z * _silu(g_ref[ci].astype(F32))).astype(o_ref.dtype)
        return carry

    lax.fori_loop(0, cw // HY_UNROLL, body, 0)


def _hyena(hy_t, taps, conv_w, conv_b, d_bias, n_blk, batch, cw=HY_CW):
    rows = n_blk * batch
    smem = pl.BlockSpec(memory_space=pltpu.SMEM)
    slab = lambda off: pl.BlockSpec((cw, rows, TOEP), lambda c: (off // cw + c, 0, 0))
    taps4 = taps.reshape(2, HY_W, 1, taps.shape[-1])
    pos = jnp.arange(TOEP)
    shift = jnp.concatenate([pos[:, None] == (pos[None, :] - 1) % TOEP,
                             pos[:, None] == (pos[None, :] + 1) % TOEP], axis=1).astype(BF16)
    return pl.pallas_call(
        functools.partial(_hyena_kernel, cw=cw, n_blk=n_blk, batch=batch),
        grid=(HY_W // cw,),
        in_specs=[smem, smem, smem, slab(0), slab(HY_W), slab(2 * HY_W), slab(3 * HY_W),
                  pl.BlockSpec((2, cw, 1, taps.shape[-1]), lambda c: (0, c, 0, 0)),
                  pl.BlockSpec((TOEP, 2 * TOEP), lambda c: (0, 0))],
        out_specs=pl.BlockSpec((cw, rows, TOEP), lambda c: (c, 0, 0)),
        out_shape=jax.ShapeDtypeStruct((HY_W, rows, TOEP), BF16),
        compiler_params=_cparams(("arbitrary",)),
        name="hyena_mixer",
    )(conv_w, conv_b, d_bias, hy_t, hy_t, hy_t, hy_t, taps4, shift)


def _out_kernel(x_ref, gate_ref, a_ref, b_ref, c_ref, w_ref, o_ref):
    o_ref[0] = x_ref[0] + gate_ref[0] * _mix_out(a_ref[0], b_ref[0], c_ref[0], w_ref)


def _out_projection(x, gate, a, bmix, c, w_out, tm):
    b, t, d = x.shape
    per_batch = gate.shape[0] > 1
    tok = lambda w: pl.BlockSpec((1, tm, w), lambda i, j: (i, j, 0))
    return pl.pallas_call(
        _out_kernel,
        grid=(b, t // tm),
        in_specs=[tok(d),
                  pl.BlockSpec((1, 1, d), (lambda i, j: (i, 0, 0)) if per_batch
                               else (lambda i, j: (0, 0, 0))),
                  tok(Q_W), tok(HY_W), tok(Q_W),
                  pl.BlockSpec(w_out.shape, lambda i, j: (0, 0))],
        out_specs=tok(d),
        out_shape=jax.ShapeDtypeStruct((b, t, d), F32),
        compiler_params=_cparams(("parallel", "arbitrary")),
        name="out_projection",
    )(x, gate, a, bmix, c, w_out)


def _rope_tables(t_len):
    pos = jnp.arange(t_len)
    n_freq = ROPE_FREQS
    inv_freq = ROPE_THETA ** (-jnp.arange(n_freq, dtype=F32) / n_freq)
    ang = jnp.stack([(pos // GRID_W).astype(F32)[:, None] * inv_freq,
                     (pos % GRID_W).astype(F32)[:, None] * inv_freq], axis=1)
    cos = jnp.cos(ang)[:, :, None, :]
    sin = jnp.sin(ang)[:, :, None, :]
    cos = jnp.broadcast_to(cos, (t_len, 2, 2, n_freq)).reshape(t_len, HEAD_DIM)
    sin = jnp.concatenate([-sin, sin], axis=2).reshape(t_len, HEAD_DIM)
    return jnp.tile(cos, (1, 2)), jnp.tile(sin, (1, 2))


def _to_channel_major(hy, n_blk):
    b, t, c = hy.shape
    return hy.reshape(b, n_blk, TOEP, c).transpose(3, 1, 0, 2).reshape(c, n_blk * b, TOEP)


def _to_token_major(y, n_blk, batch):
    c = y.shape[0]
    return y.reshape(c, n_blk, batch, TOEP).transpose(2, 1, 3, 0).reshape(batch, n_blk * TOEP, c)


def kernel(x, c, ctx, c_ctx, norm_g, w_mod, b_mod, w_in, w_out, qn_a, kn_a, qn_c, kn_c, sink_c,
           hy_conv_w, hy_conv_b, hy_w1, hy_b1, hy_w2, hy_b2, hy_w3, hy_freq, hy_bias):
    depth = w_in.shape[0]
    batch, seq, _ = x.shape
    n_ctx = ctx.shape[1]
    n_blk = seq // TOEP
    assert x.shape[2] == D_MODEL and seq % (ATTN_TQ * ATTN_TILES) == 0 and seq % GRID_W == 0
    assert seq % TOEP == 0 and n_ctx == TOEP and seq >= ATTN_TQ + 2 * WINDOW
    assert batch % 16 == 0 and HY_W % HY_CW == 0 and HY_CW % HY_UNROLL == 0
    assert KEY_CHUNK % TOEP == 0

    rows = -(-(batch + 1) // 8) * 8
    c_all = jnp.concatenate([c, c_ctx[None]], axis=0)
    c_all = jnp.pad(c_all, ((0, rows - batch - 1), (0, 0)))
    mod = _modulation(c_all, w_mod, b_mod)

    cos_l, sin_l = _rope_tables(seq)
    grp = max(1, math.gcd(batch, PROJ_TM // n_ctx))
    fold = lambda t: t.reshape(batch // grp, grp * n_ctx, t.shape[-1])
    unfold = lambda t: t.reshape(batch, n_ctx, t.shape[-1])
    cos_c = jnp.ones((grp * n_ctx, LANES), F32)
    sin_c = jnp.zeros((grp * n_ctx, LANES), F32)
    w_in_b = w_in.astype(BF16)
    w_out_b = w_out.astype(BF16)

    pending = None
    for l in range(depth):
        last = l == depth - 1
        mod_x = mod[l, :batch, None, :]
        mod_c = mod[l, batch:batch + 1, None, :]
        ng = norm_g[l][None, :]
        gq_a = jnp.concatenate([jnp.tile(qn_a[l], N_HEADS), jnp.tile(kn_a[l], N_KV)])[None, :]
        gq_c = jnp.concatenate([jnp.tile(qn_c[l], N_HEADS), jnp.tile(kn_c[l], N_KV)])[None, :]

        outs = _projection(x, mod_x, ng, w_in_b[l], cos_l, sin_l, gq_a, gq_c,
                           tm=min(seq, PROJ_TM), prev=pending)
        if pending is not None:
            x, outs = outs[0], outs[1:]
        qa, ka, va, ga, hy, qc, kc, vc, gc = outs
        if last:
            c_off = 2 * D_MODEL + Q_W
            w_kv = jnp.concatenate([w_in_b[l][:, Q_W:Q_W + 2 * KV_W],
                                    w_in_b[l][:, c_off:c_off + 2 * KV_W]], axis=1)
            gk = jnp.concatenate([jnp.tile(kn_a[l], N_KV), jnp.tile(kn_c[l], N_KV)])[None, :]
            ka_c, va_c, kc_c, vc_c = map(unfold, _kv_projection(
                fold(ctx), mod_c, ng, w_kv, gk, tm=grp * n_ctx))
        else:
            qa_c, ka_c, va_c, ga_c, hy_c, qc_c, kc_c, vc_c, gc_c = map(unfold, _projection(
                fold(ctx), mod_c, ng, w_in_b[l], cos_c, sin_c, gq_a, gq_c, tm=grp * n_ctx))

        a_out = _attention(qa, ga, ka_c, va_c, ka, va, None, "full", ATTN_TQ, ATTN_TILES,
                           "attn_global")
        c_out = _attention(qc, gc, kc_c, vc_c, kc, vc, sink_c[l], "window", ATTN_TQ, ATTN_TILES,
                           "attn_window")
        taps = _hyena_filters(seq, hy_w1[l], hy_b1[l], hy_w2[l], hy_b2[l], hy_w3[l], hy_freq[l])
        b_out = _hyena(_to_channel_major(hy, n_blk), taps, hy_conv_w[l], hy_conv_b[l],
                       hy_bias[l], n_blk, batch)
        b_out = _to_token_major(b_out, n_blk, batch)
        pending = (mod_x[:, :, 2 * D_MODEL:], a_out, b_out, c_out, w_out_b[l])

        if not last:
            a_c = _attention(qa_c, ga_c, ka_c, va_c, None, None, None, None, n_ctx, 1,
                             "attn_ctx_a")
            c_c = _attention(qc_c, gc_c, kc_c, vc_c, None, None, sink_c[l], None, n_ctx, 1,
                             "attn_ctx_c")
            taps_c = _hyena_filters(n_ctx, hy_w1[l], hy_b1[l], hy_w2[l], hy_b2[l], hy_w3[l],
                                    hy_freq[l])
            b_c = _hyena(_to_channel_major(hy_c, 1), taps_c, hy_conv_w[l], hy_conv_b[l],
                         hy_bias[l], 1, batch)
            b_c = _to_token_major(b_c, 1, batch)
            ctx = unfold(_out_projection(fold(ctx), mod_c[:, :, 2 * D_MODEL:], fold(a_c),
                                         fold(b_c), fold(c_c), w_out_b[l], grp * n_ctx))
    gate, a_out, b_out, c_out, w_o = pending
    return _out_projection(x, gate, a_out, b_out, c_out, w_o, min(seq, PROJ_TM))
```
